```python
import math
import jax, jax.numpy as jnp
from jax import lax
import numpy as np

D_MODEL = 1024
BATCH = 4
SEQ = 4096
DEPTH = 2
DEC_BATCH = 32
DEC_SEQ = 8
PAST_LEN = 16384
PAGE_SIZE = 128

N_A_LAYERS = DEPTH // 2
N_B_LAYERS = DEPTH - N_A_LAYERS
N_DENSE_LAYERS = (DEPTH + 1) // 2
N_MOE_LAYERS = DEPTH // 2

POOL_WINDOWS = (2, 4, 8, 16)
N_POOL_GROUPS = len(POOL_WINDOWS)
POOL_GROUP_DIM = D_MODEL // N_POOL_GROUPS
POOL_CTX = max(POOL_WINDOWS) - 1

HEAD_DIM = 64
N_HEADS = D_MODEL // HEAD_DIM
ATTN_GROUPS = ((128, 1), (512, 4), (2048, 16))
N_ATTN_GROUPS = len(ATTN_GROUPS)
BAND = 128
MAX_WINDOW = max(w for w, _ in ATTN_GROUPS)
ROT_DIM = HEAD_DIM // 4
ROPE_THETA = 500000.0
ATTN_SCALE = HEAD_DIM ** -0.5

D_FF = 2816
N_EXPERTS = 8
TOP_K = 2
D_FF_EXPERT = 3584
RMS_EPS = 1e-5

kernel_name = "yoco_pool_dilated_swa_moe_step"


def _rmsnorm(x, g):
    xf = x.astype(jnp.float32)
    y = xf * lax.rsqrt(jnp.mean(xf * xf, axis=-1, keepdims=True) + RMS_EPS)
    return (y * g.astype(jnp.float32)).astype(x.dtype)


def _rope(x, pos):
    half = ROT_DIM // 2
    inv_freq = jnp.exp(-math.log(ROPE_THETA) * jnp.arange(half, dtype=jnp.float32) / half)
    ang = pos.astype(jnp.float32)[:, None] * inv_freq[None, :]
    cos = jnp.cos(ang)[None, :, None, :]
    sin = jnp.sin(ang)[None, :, None, :]
    xf = x.astype(jnp.float32)
    x1 = xf[..., :half]
    x2 = xf[..., half:ROT_DIM]
    out = jnp.concatenate([x1 * cos - x2 * sin, x2 * cos + x1 * sin, xf[..., ROT_DIM:]], axis=-1)
    return out.astype(x.dtype)


def _swiglu(x, w_gu, w_down):
    g, up = jnp.split(x @ w_gu, 2, axis=-1)
    return (jax.nn.silu(g) * up) @ w_down


def _moe_swiglu(u, w_router, w_gu, w_down):
    n, t, c = u.shape
    xt = u.reshape(n * t, c)
    logits = (xt @ w_router).astype(jnp.float32)
    top_val, top_idx = lax.top_k(logits, TOP_K)
    gates = jax.nn.softmax(top_val, axis=-1)
    gate_full = jnp.sum(jax.nn.one_hot(top_idx, N_EXPERTS, dtype=jnp.float32) * gates[..., None], axis=1)
    y = jnp.zeros((n * t, c), jnp.float32)
    for e in range(N_EXPERTS):
        y = y + gate_full[:, e:e + 1] * _swiglu(xt, w_gu[e], w_down[e]).astype(jnp.float32)
    return y.astype(u.dtype).reshape(n, t, c)


def _pool_mix(u_ext, n_ctx, w_pool, scale):
    n, length, c = u_ext.shape
    t = length - n_ctx
    uf = u_ext.astype(jnp.float32)
    cs = jnp.concatenate([jnp.zeros_like(uf[:, :1]), jnp.cumsum(uf, axis=1)], axis=1)
    hi = np.arange(n_ctx + 1, length + 1)
    diffs = []
    for g, w in enumerate(POOL_WINDOWS):
        lo = np.maximum(hi - w, 0)
        cnt = jnp.asarray((hi - lo).astype(np.float32))
        sl = slice(g * POOL_GROUP_DIM, (g + 1) * POOL_GROUP_DIM)
        mean = (cs[:, hi, sl] - cs[:, lo, sl]) / cnt[None, :, None]
        diffs.append(mean - uf[:, n_ctx:, sl])
    d = jnp.stack(diffs, axis=2).astype(u_ext.dtype)
    y = jnp.einsum('ntgc,gcd->ntgd', d, w_pool).reshape(n, t, c)
    return y * scale


def _merge_groups(outs, lses):
    alpha = jax.nn.softmax(jnp.stack(lses), axis=0)
    return jnp.sum(alpha[..., None] * jnp.stack(outs), axis=0)


def _dilated_attn_full(q, k, v):
    n, s = k.shape[:2]
    outs, lses = [], []
    for g, (win, dil) in enumerate(ATTN_GROUPS):
        n_steps = win // dil
        length = s // dil
        nb = -(-length // BAND)
        lp = nb * BAND

        def to_res(x):
            x = x.reshape(n, length, dil, N_HEADS, HEAD_DIM).transpose(0, 2, 1, 3, 4)
            return jnp.pad(x, ((0, 0), (0, 0), (0, lp - length), (0, 0), (0, 0)))

        def key_blocks(x):
            xr = jnp.pad(to_res(x), ((0, 0), (0, 0), (BAND, 0), (0, 0), (0, 0)))
            xr = xr.reshape(n, dil, nb + 1, BAND, N_HEADS, HEAD_DIM)
            return jnp.concatenate([xr[:, :, :-1], xr[:, :, 1:]], axis=3)

        qr = to_res(q[:, :, g]).reshape(n, dil, nb, BAND, N_HEADS, HEAD_DIM)
        kb = key_blocks(k)
        vb = key_blocks(v)
        sc = jnp.einsum('brnqhd,brnkhd->brnhqk', qr, kb, preferred_element_type=jnp.float32) * ATTN_SCALE
        qi = np.arange(nb)[:, None, None] * BAND + np.arange(BAND)[None, :, None]
        ki = np.arange(nb)[:, None, None] * BAND - BAND + np.arange(2 * BAND)[None, None, :]
        dist = qi - ki
        valid = jnp.asarray((dist >= 0) & (dist <= n_steps) & (ki >= 0))
        sc = jnp.where(valid[None, None, :, None], sc, -jnp.inf)
        m = jnp.max(sc, axis=-1, keepdims=True)
        p = jnp.exp(sc - m)
        den = jnp.sum(p, axis=-1)
        o = jnp.einsum('brnhqk,brnkhd->brnqhd', p, vb.astype(jnp.float32))
        den_t = den.transpose(0, 1, 2, 4, 3)
        o = o / den_t[..., None]
        lse = (m[..., 0] + jnp.log(den)).transpose(0, 1, 2, 4, 3)
        o = o.reshape(n, dil, lp, N_HEADS, HEAD_DIM)[:, :, :length]
        o = o.transpose(0, 2, 1, 3, 4).reshape(n, s, N_HEADS, HEAD_DIM)
        lse = lse.reshape(n, dil, lp, N_HEADS)[:, :, :length].transpose(0, 2, 1, 3).reshape(n, s, N_HEADS)
        outs.append(o)
        lses.append(lse)
    return _merge_groups(outs, lses)


def _dilated_attn_cached(q, k_ext, v_ext, n_ctx):
    t = q.shape[1]
    outs, lses = [], []
    for g, (win, dil) in enumerate(ATTN_GROUPS):
        n_steps = win // dil
        idx = n_ctx + np.arange(t)[:, None] - (np.arange(n_steps + 1) * dil)[None, :]
        valid = jnp.asarray(idx >= 0)
        idxc = np.maximum(idx, 0)
        kg = k_ext[:, idxc]
        vg = v_ext[:, idxc]
        sc = jnp.einsum('nthd,ntkhd->nthk', q[:, :, g], kg, preferred_element_type=jnp.float32) * ATTN_SCALE
        sc = jnp.where(valid[None, :, None, :], sc, -jnp.inf)
        m = jnp.max(sc, axis=-1, keepdims=True)
        p = jnp.exp(sc - m)
        den = jnp.sum(p, axis=-1)
        o = jnp.einsum('nthk,ntkhd->nthd', p, vg.astype(jnp.float32)) / den[..., None]
        outs.append(o)
        lses.append(m[..., 0] + jnp.log(den))
    return _merge_groups(outs, lses)


def _trunk(x, pos, pool_ctx, k_ctx, v_ctx, norm_mix, norm_ffn, pool_w, pool_scale, norm_kv, w_kv,
           w_q, w_o, w_ffn_gu, w_ffn_down, w_router, w_exp_gu, w_exp_down, norm_final):
    n, t, _ = x.shape
    h = x
    new_pool = []
    k_all = v_all = None
    n_ctx_kv = 0
    for i in range(DEPTH):
        if i == N_A_LAYERS:
            kv = (_rmsnorm(h, norm_kv) @ w_kv).reshape(n, t, 2, N_HEADS, HEAD_DIM)
            k_new = _rope(kv[:, :, 0], pos)
            v_new = kv[:, :, 1]
            if k_ctx is None:
                k_all, v_all = k_new, v_new
            else:
                k_all = jnp.concatenate([k_ctx.astype(k_new.dtype), k_new], axis=1)
                v_all = jnp.concatenate([v_ctx.astype(v_new.dtype), v_new], axis=1)
                n_ctx_kv = k_ctx.shape[1]
        u = _rmsnorm(h, norm_mix[i])
        if i < N_A_LAYERS:
            a = i
            if pool_ctx is None:
                ext, n_ctx = u, 0
            else:
                ext, n_ctx = jnp.concatenate([pool_ctx[a].astype(u.dtype), u], axis=1), POOL_CTX
            h = h + _pool_mix(ext, n_ctx, pool_w[a], pool_scale[a])
            new_pool.append(ext[:, -POOL_CTX:])
        else:
            b = i - N_A_LAYERS
            q = (u @ w_q[b]).reshape(n, t, N_ATTN_GROUPS * N_HEADS, HEAD_DIM)
            q = _rope(q, pos).reshape(n, t, N_ATTN_GROUPS, N_HEADS, HEAD_DIM)
            if k_ctx is None:
                o = _dilated_attn_full(q, k_all, v_all)
            else:
                o = _dilated_attn_cached(q, k_all, v_all, n_ctx_kv)
            h = h + o.reshape(n, t, N_HEADS * HEAD_DIM).astype(h.dtype) @ w_o[b]
        u = _rmsnorm(h, norm_ffn[i])
        if i % 2 == 0:
            h = h + _swiglu(u, w_ffn_gu[i // 2], w_ffn_down[i // 2])
        else:
            h = h + _moe_swiglu(u, w_router[i // 2], w_exp_gu[i // 2], w_exp_down[i // 2])
    keep = min(MAX_WINDOW, k_all.shape[1])
    return _rmsnorm(h, norm_final), jnp.stack(new_pool), k_all[:, -keep:], v_all[:, -keep:]


def setup_inputs(seed: int = 0) -> dict:
    key = jax.random.key(seed)
    ks = jax.random.split(key, 24)
    f32 = jnp.float32
    buf = min(MAX_WINDOW, PAST_LEN)
    nrm = lambda k, shape: jax.random.normal(k, shape, f32)
    return {
        "x_prompt": nrm(ks[0], (BATCH, SEQ, D_MODEL)),
        "x_sample": nrm(ks[1], (DEC_BATCH, DEC_SEQ, D_MODEL)),
        "state_pool": nrm(ks[2], (N_A_LAYERS, DEC_BATCH, POOL_CTX, D_MODEL)),
        "cache_k_win": nrm(ks[3], (DEC_BATCH, buf, N_HEADS, HEAD_DIM)),
        "cache_v_win": nrm(ks[4], (DEC_BATCH, buf, N_HEADS, HEAD_DIM)),
        "norm_mix": 1.0 + 0.02 * nrm(ks[5], (DEPTH, D_MODEL)),
        "norm_ffn": 1.0 + 0.02 * nrm(ks[6], (DEPTH, D_MODEL)),
        "pool_w": nrm(ks[7], (N_A_LAYERS, N_POOL_GROUPS, POOL_GROUP_DIM, POOL_GROUP_DIM)) * POOL_GROUP_DIM ** -0.5,
        "pool_scale": 1.0 + 0.1 * nrm(ks[8], (N_A_LAYERS, D_MODEL)),
        "norm_kv": 1.0 + 0.02 * nrm(ks[9], (D_MODEL,)),
        "w_kv": nrm(ks[10], (D_MODEL, 2 * N_HEADS * HEAD_DIM)) * D_MODEL ** -0.5,
        "w_q": nrm(ks[11], (N_B_LAYERS, D_MODEL, N_ATTN_GROUPS * N_HEADS * HEAD_DIM)) * D_MODEL ** -0.5,
        "w_o": nrm(ks[12], (N_B_LAYERS, N_HEADS * HEAD_DIM, D_MODEL)) * (N_HEADS * HEAD_DIM) ** -0.5,
        "w_ffn_gu": nrm(ks[13], (N_DENSE_LAYERS, D_MODEL, 2 * D_FF)) * D_MODEL ** -0.5,
        "w_ffn_down": nrm(ks[14], (N_DENSE_LAYERS, D_FF, D_MODEL)) * D_FF ** -0.5,
        "w_router": nrm(ks[15], (N_MOE_LAYERS, D_MODEL, N_EXPERTS)) * D_MODEL ** -0.5,
        "w_exp_gu": nrm(ks[16], (N_MOE_LAYERS, N_EXPERTS, D_MODEL, 2 * D_FF_EXPERT)) * D_MODEL ** -0.5,
        "w_exp_down": nrm(ks[17], (N_MOE_LAYERS, N_EXPERTS, D_FF_EXPERT, D_MODEL)) * D_FF_EXPERT ** -0.5,
        "norm_final": 1.0 + 0.02 * nrm(ks[18], (D_MODEL,)),
    }


def reference(x_prompt, x_sample, state_pool, cache_k_win, cache_v_win, norm_mix, norm_ffn, pool_w,
              pool_scale, norm_kv, w_kv, w_q, w_o, w_ffn_gu, w_ffn_down, w_router, w_exp_gu,
              w_exp_down, norm_final):
    pos_p = jnp.arange(x_prompt.shape[1], dtype=jnp.int32)
    pos_s = PAST_LEN + jnp.arange(x_sample.shape[1], dtype=jnp.int32)
    y_prompt, pool_prompt, k_win_prompt, v_win_prompt = _trunk(
        x_prompt, pos_p, None, None, None, norm_mix, norm_ffn, pool_w, pool_scale, norm_kv, w_kv,
        w_q, w_o, w_ffn_gu, w_ffn_down, w_router, w_exp_gu, w_exp_down, norm_final)
    y_sample, pool_sample, k_win_sample, v_win_sample = _trunk(
        x_sample, pos_s, state_pool, cache_k_win, cache_v_win, norm_mix, norm_ffn, pool_w, pool_scale,
        norm_kv, w_kv, w_q, w_o, w_ffn_gu, w_ffn_down, w_router, w_exp_gu, w_exp_down, norm_final)
    return (y_prompt, y_sample, pool_prompt, pool_sample, k_win_prompt, v_win_prompt, k_win_sample, v_win_sample)
```

```python
import functools
import math

import numpy as np
import jax
import jax.numpy as jnp
from jax import lax
from jax.experimental import pallas as pl
from jax.experimental.pallas import tpu as pltpu

D_MODEL = 1024
PAST_LEN = 16384
POOL_WINDOWS = (2, 4, 8, 16)
POOL_GROUP_DIM = D_MODEL // len(POOL_WINDOWS)
POOL_CTX = max(POOL_WINDOWS) - 1
HEAD_DIM = 64
N_HEADS = D_MODEL // HEAD_DIM
ATTN_GROUPS = ((128, 1), (512, 4), (2048, 16))
N_ATTN_GROUPS = len(ATTN_GROUPS)
BAND = 128
MAX_WINDOW = max(w for w, _ in ATTN_GROUPS)
ROT_DIM = HEAD_DIM // 4
ROPE_THETA = 500000.0
ATTN_SCALE = HEAD_DIM ** -0.5
N_EXPERTS = 8
RMS_EPS = 1e-5

LANES = 128
HALO = 16
MASK_VALUE = -1e30
M_INIT = -1e20
VMEM_LIMIT = 52 * 1024 * 1024

_F32 = jnp.float32
_BF16 = jnp.bfloat16
_NT = (((1,), (1,)), ((), ()))


def _params(semantics):
    return pltpu.CompilerParams(dimension_semantics=semantics, vmem_limit_bytes=VMEM_LIMIT)


def _rms_unit(x):
    return x * lax.rsqrt(jnp.mean(x * x, axis=-1, keepdims=True) + RMS_EPS)


def _dot(a, b):
    return jnp.dot(a, b, preferred_element_type=_F32)


def _pool_kernel(*refs, tm, n_ctx, bn):
    if n_ctx:
        x_ref, ctx_ref, g_ref, w_ref, sc_ref, h_ref, pool_ref, ext_ref = refs
    else:
        x_ref, g_ref, w_ref, sc_ref, h_ref, pool_ref, ext_ref = refs
        ctx_ref = None
    s = pl.program_id(1)
    pos = n_ctx + s * tm + lax.broadcasted_iota(jnp.int32, (tm, 1), 0)
    for b in range(bn):
        x = x_ref[b]
        u = _rms_unit(x) * g_ref[...]

        @pl.when(s == 0)
        def _():
            if n_ctx:
                ext_ref[b, 0:HALO - n_ctx, :] = jnp.zeros((HALO - n_ctx, D_MODEL), _F32)
                ext_ref[b, HALO - n_ctx:HALO, :] = ctx_ref[b]
            else:
                ext_ref[b, 0:HALO, :] = jnp.zeros((HALO, D_MODEL), _F32)

        ext_ref[b, HALO:HALO + tm, :] = u
        for g, w in enumerate(POOL_WINDOWS):
            sl = slice(g * POOL_GROUP_DIM, (g + 1) * POOL_GROUP_DIM)
            acc = u[:, sl]
            for k in range(1, w):
                acc = acc + ext_ref[b, HALO - k:HALO - k + tm, sl]
            cnt = jnp.minimum(pos + 1, w).astype(_F32)
            d = acc / cnt - u[:, sl]
            y = _dot(d.astype(_BF16), w_ref[g])
            h_ref[b, :, sl] = x[:, sl] + y * sc_ref[:, sl]

        @pl.when(s == pl.num_programs(1) - 1)
        def _():
            pool_ref[0, b] = ext_ref[b, HALO + tm - POOL_CTX:HALO + tm, :]

        ext_ref[b, 0:HALO, :] = ext_ref[b, tm:tm + HALO, :]


def _pool_layer(x, ctx, gain, w_pool, scale, tm, bn):
    n, t, _ = x.shape
    n_ctx = 0 if ctx is None else POOL_CTX
    grid = (n // bn, t // tm)
    row = lambda i, s: (i, s, 0)
    const2 = lambda i, s: (0, 0)
    in_specs = [pl.BlockSpec((bn, tm, D_MODEL), row)]
    args = [x]
    if ctx is not None:
        in_specs.append(pl.BlockSpec((bn, POOL_CTX, D_MODEL), lambda i, s: (i, 0, 0)))
        args.append(ctx)
    in_specs += [
        pl.BlockSpec((1, D_MODEL), const2),
        pl.BlockSpec((len(POOL_WINDOWS), POOL_GROUP_DIM, POOL_GROUP_DIM), lambda i, s: (0, 0, 0)),
        pl.BlockSpec((1, D_MODEL), const2),
    ]
    args += [gain.reshape(1, D_MODEL), w_pool.astype(_BF16), scale.reshape(1, D_MODEL)]
    return pl.pallas_call(
        functools.partial(_pool_kernel, tm=tm, n_ctx=n_ctx, bn=bn),
        grid=grid,
        in_specs=in_specs,
        out_specs=[
            pl.BlockSpec((bn, tm, D_MODEL), row),
            pl.BlockSpec((1, bn, POOL_CTX, D_MODEL), lambda i, s: (0, i, 0, 0)),
        ],
        out_shape=[
            jax.ShapeDtypeStruct((n, t, D_MODEL), _F32),
            jax.ShapeDtypeStruct((1, n, POOL_CTX, D_MODEL), _F32),
        ],
        scratch_shapes=[pltpu.VMEM((bn, HALO + tm, D_MODEL), _F32)],
        compiler_params=_params(("parallel", "arbitrary")),
        name="pool_layer",
    )(*args)


def _ffn_kernel(*refs, use_gate, final_norm):
    refs = list(refs)
    h_ref, g_ref = refs[:2]
    del refs[:2]
    gate_ref = refs.pop(0) if use_gate else None
    wg_ref, wu_ref, wd_ref = refs[:3]
    del refs[:3]
    gf_ref = refs.pop(0) if final_norm else None
    out_ref, u_sc, acc_sc = refs[:3]
    tot_sc = refs[3] if use_gate else None
    e = pl.program_id(1)
    c = pl.program_id(2)
    last_c = c == pl.num_programs(2) - 1
    last_e = e == pl.num_programs(1) - 1

    @pl.when((e == 0) & (c == 0))
    def _():
        u_sc[...] = (_rms_unit(h_ref[...]) * g_ref[...]).astype(_BF16)
        if use_gate:
            tot_sc[...] = jnp.zeros_like(tot_sc)

    @pl.when(c == 0)
    def _():
        acc_sc[...] = jnp.zeros_like(acc_sc)

    u = u_sc[...]
    g = _dot(u, wg_ref[...])
    up = _dot(u, wu_ref[...])
    act = (g * jax.nn.sigmoid(g) * up).astype(_BF16)
    acc_sc[...] += _dot(act, wd_ref[...])

    if use_gate:
        @pl.when(last_c)
        def _():
            lane = lax.broadcasted_iota(jnp.int32, gate_ref.shape, 1)
            gcol = jnp.sum(jnp.where(lane == e, gate_ref[...], 0.0), axis=1, keepdims=True)
            tot_sc[...] += gcol * acc_sc[...]

    @pl.when(last_c & last_e)
    def _():
        y = tot_sc[...] if use_gate else acc_sc[...]
        hn = h_ref[...] + y
        if final_norm:
            hn = _rms_unit(hn) * gf_ref[...]
        out_ref[...] = hn


def _ffn_layer(h, gain, w_gu, w_down, tm, tf, gate=None, final_gain=None):
    m = h.shape[0]
    n_exp, _, f2 = w_gu.shape
    f = f2 // 2
    nc = f // tf
    row = lambda i, e, c: (i, 0)
    const2 = lambda i, e, c: (0, 0)
    in_specs = [pl.BlockSpec((tm, D_MODEL), row), pl.BlockSpec((1, D_MODEL), const2)]
    args = [h, gain.reshape(1, D_MODEL)]
    if gate is not None:
        in_specs.append(pl.BlockSpec((tm, LANES), row))
        args.append(gate)
    in_specs += [
        pl.BlockSpec((None, D_MODEL, tf), lambda i, e, c: (e, 0, c)),
        pl.BlockSpec((None, D_MODEL, tf), lambda i, e, c: (e, 0, nc + c)),
        pl.BlockSpec((None, tf, D_MODEL), lambda i, e, c: (e, c, 0)),
    ]
    args += [w_gu, w_gu, w_down]
    if final_gain is not None:
        in_specs.append(pl.BlockSpec((1, D_MODEL), const2))
        args.append(final_gain.reshape(1, D_MODEL))
    scratch = [pltpu.VMEM((tm, D_MODEL), _BF16), pltpu.VMEM((tm, D_MODEL), _F32)]
    if gate is not None:
        scratch.append(pltpu.VMEM((tm, D_MODEL), _F32))
    return pl.pallas_call(
        functools.partial(_ffn_kernel, use_gate=gate is not None, final_norm=final_gain is not None),
        grid=(m // tm, n_exp, nc),
        in_specs=in_specs,
        out_specs=pl.BlockSpec((tm, D_MODEL), row),
        out_shape=jax.ShapeDtypeStruct((m, D_MODEL), _F32),
        scratch_shapes=scratch,
        compiler_params=_params(("parallel", "arbitrary", "arbitrary")),
        name="swiglu_experts" if gate is not None else "swiglu_dense",
    )(*args)


def _rope_tables(pos):
    half = ROT_DIM // 2
    inv_freq = jnp.exp(-math.log(ROPE_THETA) * jnp.arange(half, dtype=_F32) / half)
    ang = pos.astype(_F32)[:, None] * inv_freq[None, :]
    cos, sin = jnp.cos(ang), jnp.sin(ang)
    t = pos.shape[0]
    rest = HEAD_DIM - ROT_DIM
    zero_h = jnp.zeros((t, half), _F32)
    zero_r = jnp.zeros((t, rest), _F32)
    c = jnp.concatenate([cos, cos, jnp.ones((t, rest), _F32)], axis=1)
    s1 = jnp.concatenate([zero_h, sin, zero_r], axis=1)
    s2 = jnp.concatenate([-sin, zero_h, zero_r], axis=1)
    rep = LANES // HEAD_DIM
    return tuple(jnp.tile(a, (1, rep)) for a in (c, s1, s2))


def _qkv_kernel(h_ref, gkv_ref, gq_ref, wkv_ref, wq_ref, c_ref, s1_ref, s2_ref,
                q_ref, k_ref, v_ref, kf_ref, vf_ref):
    hn = _rms_unit(h_ref[...])
    kv = _dot((hn * gkv_ref[...]).astype(_BF16), wkv_ref[...])
    q = _dot((hn * gq_ref[...]).astype(_BF16), wq_ref[...])
    c, s1, s2 = c_ref[...], s1_ref[...], s2_ref[...]
    half = ROT_DIM // 2

    def rope(t):
        return t * c + pltpu.roll(t, half, 1) * s1 + pltpu.roll(t, LANES - half, 1) * s2

    for j in range(D_MODEL // LANES):
        sl = slice(j * LANES, (j + 1) * LANES)
        t = rope(kv[:, sl])
        kf_ref[:, sl] = t
        k_ref[:, sl] = t.astype(_BF16)
    v = kv[:, D_MODEL:]
    vf_ref[...] = v
    v_ref[...] = v.astype(_BF16)
    for j in range(N_ATTN_GROUPS * D_MODEL // LANES):
        sl = slice(j * LANES, (j + 1) * LANES)
        q_ref[:, sl] = (rope(q[:, sl]) * ATTN_SCALE).astype(_BF16)


def _qkv_layer(h, pos, norm_kv, norm_q, w_kv, w_q, tm, keep):
    n, t, _ = h.shape
    off = (t - keep) // tm
    tables = _rope_tables(pos)
    row = lambda i, s: (i, s, 0)
    const2 = lambda i, s: (0, 0)
    tab = pl.BlockSpec((tm, LANES), lambda i, s: (s, 0))
    win = lambda i, s: (i, jnp.maximum(s - off, 0), 0)
    dq = N_ATTN_GROUPS * D_MODEL
    return pl.pallas_call(
        _qkv_kernel,
        grid=(n, t // tm),
        in_specs=[
            pl.BlockSpec((None, tm, D_MODEL), row),
            pl.BlockSpec((1, D_MODEL), const2),
            pl.BlockSpec((1, D_MODEL), const2),
            pl.BlockSpec((D_MODEL, 2 * D_MODEL), const2),
            pl.BlockSpec((D_MODEL, dq), const2),
            tab, tab, tab,
        ],
        out_specs=[
            pl.BlockSpec((None, tm, dq), row),
            pl.BlockSpec((None, tm, D_MODEL), row),
            pl.BlockSpec((None, tm, D_MODEL), row),
            pl.BlockSpec((None, tm, D_MODEL), win),
            pl.BlockSpec((None, tm, D_MODEL), win),
        ],
        out_shape=[
            jax.ShapeDtypeStruct((n, t, dq), _BF16),
            jax.ShapeDtypeStruct((n, t, D_MODEL), _BF16),
            jax.ShapeDtypeStruct((n, t, D_MODEL), _BF16),
            jax.ShapeDtypeStruct((n, keep, D_MODEL), _F32),
            jax.ShapeDtypeStruct((n, keep, D_MODEL), _F32),
        ],
        compiler_params=_params(("parallel", "arbitrary")),
        name="qkv_rope",
    )(h, norm_kv.reshape(1, D_MODEL), norm_q.reshape(1, D_MODEL), w_kv, w_q, *tables)


def _band_attn_kernel(q_ref, kh_ref, k_ref, vh_ref, v_ref, o_ref, lse_ref, kx_ref, vx_ref, *, tb):
    m_id = pl.program_id(2)
    kx_ref[0:BAND, :] = kh_ref[...]
    kx_ref[BAND:, :] = k_ref[...]
    vx_ref[0:BAND, :] = vh_ref[...]
    vx_ref[BAND:, :] = v_ref[...]
    row = lax.broadcasted_iota(jnp.int32, (BAND, 2 * BAND), 0)
    col = lax.broadcasted_iota(jnp.int32, (BAND, 2 * BAND), 1)
    dist = row - col + BAND
    bias = jnp.where(dist >= 0, jnp.where(dist <= BAND, 0.0, MASK_VALUE), MASK_VALUE)
    no_prev = jnp.where(m_id > 0, 0.0, MASK_VALUE)
    bias_first = jnp.where(col < BAND, bias + no_prev, bias)
    lane = lax.broadcasted_iota(jnp.int32, (BAND, LANES), 1)
    low = lane < HEAD_DIM
    for b in range(tb):
        bb = bias_first if b == 0 else bias
        rows = slice(b * BAND, (b + 1) * BAND)
        lse_t = jnp.zeros((BAND, LANES), _F32)
        for p in range(D_MODEL // LANES):
            sl = slice(p * LANES, (p + 1) * LANES)
            q2 = q_ref[rows, sl].astype(_F32)
            k2 = kx_ref[b * BAND:(b + 2) * BAND, sl]
            v2 = vx_ref[b * BAND:(b + 2) * BAND, sl]
            outs = []
            for hh in range(2):
                qm = (jnp.where(low, q2, 0.0) if hh == 0 else jnp.where(low, 0.0, q2)).astype(_BF16)
                s = lax.dot_general(qm, k2, _NT, preferred_element_type=_F32) + bb
                mx = jnp.max(s, axis=1, keepdims=True)
                pr = jnp.exp(s - mx)
                den = jnp.sum(pr, axis=1, keepdims=True)
                o = _dot(pr.astype(_BF16), v2)
                outs.append(o * (1.0 / den))
                lse_t = jnp.where(lane == 2 * p + hh, mx + jnp.log(den), lse_t)
            o_ref[rows, sl] = jnp.where(low, outs[0], outs[1]).astype(_BF16)
        lse_ref[rows, :] = lse_t


def _band_attention(q, k, v, group, tb):
    n, s, _ = k.shape
    _, dil = ATTN_GROUPS[group]
    length = s // dil
    rows = tb * BAND
    qv = q.reshape(n, length, dil * N_ATTN_GROUPS * D_MODEL)
    kv = k.reshape(n, length, dil * D_MODEL)
    vv = v.reshape(n, length, dil * D_MODEL)
    cur = lambda i, r, m: (i, m, r)
    halo = lambda i, r, m: (i, jnp.maximum(m * tb - 1, 0), r)
    o, lse = pl.pallas_call(
        functools.partial(_band_attn_kernel, tb=tb),
        grid=(n, dil, length // rows),
        in_specs=[
            pl.BlockSpec((None, rows, D_MODEL), lambda i, r, m: (i, m, r * N_ATTN_GROUPS + group)),
            pl.BlockSpec((None, BAND, D_MODEL), halo),
            pl.BlockSpec((None, rows, D_MODEL), cur),
            pl.BlockSpec((None, BAND, D_MODEL), halo),
            pl.BlockSpec((None, rows, D_MODEL), cur),
        ],
        out_specs=[
            pl.BlockSpec((None, rows, D_MODEL), cur),
            pl.BlockSpec((None, rows, LANES), cur),
        ],
        out_shape=[
            jax.ShapeDtypeStruct((n, length, dil * D_MODEL), _BF16),
            jax.ShapeDtypeStruct((n, length, dil * LANES), _F32),
        ],
        scratch_shapes=[pltpu.VMEM((BAND + rows, D_MODEL), _BF16), pltpu.VMEM((BAND + rows, D_MODEL), _BF16)],
        compiler_params=_params(("parallel", "parallel", "arbitrary")),
        name="band_attn_g%d" % group,
    )(qv, kv, kv, vv, vv)
    return o.reshape(n, s, D_MODEL), lse.reshape(n, s, LANES)


def _head_expand_matrix():
    e = np.zeros((LANES, N_ATTN_GROUPS * D_MODEL), np.float32)
    for part in range(2):
        for g in range(N_ATTN_GROUPS):
            for h in range(N_HEADS):
                r = part * N_ATTN_GROUPS * N_HEADS + g * N_HEADS + h
                e[r, g * D_MODEL + h * HEAD_DIM:g * D_MODEL + (h + 1) * HEAD_DIM] = 1.0
    return jnp.asarray(e, _BF16)


def _merge_proj_kernel(o0_ref, o1_ref, o2_ref, l0_ref, l1_ref, l2_ref, e_ref, h_ref, wo_ref, out_ref):
    l0, l1, l2 = l0_ref[...], l1_ref[...], l2_ref[...]
    mx = jnp.maximum(jnp.maximum(l0, l1), l2)
    e0, e1, e2 = jnp.exp(l0 - mx), jnp.exp(l1 - mx), jnp.exp(l2 - mx)
    inv = 1.0 / (e0 + e1 + e2)
    lane = lax.broadcasted_iota(jnp.int32, l0.shape, 1)
    head = lane < N_HEADS
    a = (jnp.where(head, e0 * inv, 0.0)
         + pltpu.roll(jnp.where(head, e1 * inv, 0.0), N_HEADS, 1)
         + pltpu.roll(jnp.where(head, e2 * inv, 0.0), 2 * N_HEADS, 1))
    hi = a.astype(_BF16).astype(_F32)
    a2 = hi + pltpu.roll(a - hi, N_ATTN_GROUPS * N_HEADS, 1)
    w = _dot(a2.astype(_BF16), e_ref[...])
    o = (w[:, 0:D_MODEL] * o0_ref[...].astype(_F32)
         + w[:, D_MODEL:2 * D_MODEL] * o1_ref[...].astype(_F32)
         + w[:, 2 * D_MODEL:] * o2_ref[...].astype(_F32))
    out_ref[...] = h_ref[...] + _dot(o.astype(_BF16), wo_ref[...])


def _merge_proj(outs, lses, h, w_o, tm):
    m = h.shape[0]
    row = lambda i: (i, 0)
    const = lambda i: (0, 0)
    o_spec = pl.BlockSpec((tm, D_MODEL), row)
    l_spec = pl.BlockSpec((tm, LANES), row)
    return pl.pallas_call(
        _merge_proj_kernel,
        grid=(m // tm,),
        in_specs=[o_spec] * 3 + [l_spec] * 3 + [
            pl.BlockSpec((LANES, N_ATTN_GROUPS * D_MODEL), const),
            pl.BlockSpec((tm, D_MODEL), row),
            pl.BlockSpec((D_MODEL, D_MODEL), const),
        ],
        out_specs=pl.BlockSpec((tm, D_MODEL), row),
        out_shape=jax.ShapeDtypeStruct((m, D_MODEL), _F32),
        compiler_params=_params(("parallel",)),
        name="merge_out_proj",
    )(*outs, *lses, _head_expand_matrix(), h, w_o)


def _proj_kernel(o_ref, h_ref, wo_ref, out_ref):
    out_ref[...] = h_ref[...] + _dot(o_ref[...].astype(_BF16), wo_ref[...])


def _out_proj(o, h, w_o):
    m = h.shape[0]
    full = lambda shape: pl.BlockSpec(shape, lambda i: (0, 0))
    return pl.pallas_call(
        _proj_kernel,
        grid=(1,),
        in_specs=[full((m, D_MODEL)), full((m, D_MODEL)), full((D_MODEL, D_MODEL))],
        out_specs=full((m, D_MODEL)),
        out_shape=jax.ShapeDtypeStruct((m, D_MODEL), _F32),
        compiler_params=_params(("arbitrary",)),
        name="out_proj",
    )(o, h, w_o)


def _cached_bias(t, n_ctx):
    rows = N_ATTN_GROUPS * N_HEADS * t
    cache = np.full((rows, n_ctx), MASK_VALUE, np.float32)
    new = np.full((rows, LANES), MASK_VALUE, np.float32)
    for g, (win, dil) in enumerate(ATTN_GROUPS):
        for j in range(t):
            idx = n_ctx + j - np.arange(win // dil + 1) * dil
            idx = idx[idx >= 0]
            for h in range(N_HEADS):
                r = (g * N_HEADS + h) * t + j
                cache[r, idx[idx < n_ctx]] = 0.0
                new[r, idx[idx >= n_ctx] - n_ctx] = 0.0
    return jnp.asarray(cache), jnp.asarray(new)


def _cached_attn_kernel(q_ref, kc_ref, vc_ref, kn_ref, vn_ref, bias_ref, biasn_ref, o_ref,
                        qbd_ref, m_ref, l_ref, acc_ref, *, t):
    c = pl.program_id(1)
    gh = N_HEADS * t
    row = lax.broadcasted_iota(jnp.int32, (gh, D_MODEL), 0)
    lane = lax.broadcasted_iota(jnp.int32, (gh, D_MODEL), 1)
    own = jnp.right_shift(row, int(math.log2(t))) == jnp.right_shift(lane, int(math.log2(HEAD_DIM)))

    @pl.when(c == 0)
    def _():
        q = q_ref[...].astype(_F32)
        for g in range(N_ATTN_GROUPS):
            qg = q[:, g * D_MODEL:(g + 1) * D_MODEL]
            tiled = jnp.concatenate([qg] * N_HEADS, axis=0)
            qbd_ref[g * gh:(g + 1) * gh, :] = jnp.where(own, tiled, 0.0).astype(_BF16)
        m_ref[...] = jnp.full(m_ref.shape, M_INIT, _F32)
        l_ref[...] = jnp.zeros_like(l_ref)
        acc_ref[...] = jnp.zeros_like(acc_ref)

    def update(kb, vb, bias):
        s = lax.dot_general(qbd_ref[...], kb, _NT, preferred_element_type=_F32) + bias
        m_old = m_ref[...]
        m_new = jnp.maximum(m_old, jnp.max(s, axis=1, keepdims=True))
        alpha = jnp.exp(m_old - m_new)
        pr = jnp.exp(s - m_new)
        l_ref[...] = alpha * l_ref[...] + jnp.sum(pr, axis=1, keepdims=True)
        acc_ref[...] = alpha * acc_ref[...] + _dot(pr.astype(_BF16), vb)
        m_ref[...] = m_new

    update(kc_ref[...].astype(_BF16), vc_ref[...].astype(_BF16), bias_ref[...])

    @pl.when(c == pl.num_programs(1) - 1)
    def _():
        pad = jnp.zeros((LANES - t, D_MODEL), _F32)
        kn = jnp.concatenate([kn_ref[...].astype(_F32), pad], axis=0).astype(_BF16)
        vn = jnp.concatenate([vn_ref[...].astype(_F32), pad], axis=0).astype(_BF16)
        update(kn, vn, biasn_ref[...])
        ms = [m_ref[g * gh:(g + 1) * gh, :] for g in range(N_ATTN_GROUPS)]
        mx = jnp.maximum(jnp.maximum(ms[0], ms[1]), ms[2])
        num = jnp.zeros((gh, D_MODEL), _F32)
        den = jnp.zeros((gh, 1), _F32)
        for g in range(N_ATTN_GROUPS):
            w = jnp.exp(ms[g] - mx)
            num = num + w * acc_ref[g * gh:(g + 1) * gh, :]
            den = den + w * l_ref[g * gh:(g + 1) * gh, :]
        on = jnp.where(own, num * (1.0 / den), 0.0)
        o = on[0:t, :]
        for h in range(1, N_HEADS):
            o = o + on[h * t:(h + 1) * t, :]
        o_ref[...] = o


def _cached_attention(q, k_new, v_new, cache_k, cache_v, chunk):
    n, t, _ = k_new.shape
    n_ctx = cache_k.shape[1]
    bias, bias_new = _cached_bias(t, n_ctx)
    rows = N_ATTN_GROUPS * N_HEADS * t
    per_n = lambda i, c: (i, 0, 0)
    return pl.pallas_call(
        functools.partial(_cached_attn_kernel, t=t),
        grid=(n, n_ctx // chunk),
        in_specs=[
            pl.BlockSpec((None, t, N_ATTN_GROUPS * D_MODEL), per_n),
            pl.BlockSpec((None, chunk, D_MODEL), lambda i, c: (i, c, 0)),
            pl.BlockSpec((None, chunk, D_MODEL), lambda i, c: (i, c, 0)),
            pl.BlockSpec((None, t, D_MODEL), per_n),
            pl.BlockSpec((None, t, D_MODEL), per_n),
            pl.BlockSpec((rows, chunk), lambda i, c: (0, c)),
            pl.BlockSpec((rows, LANES), lambda i, c: (0, 0)),
        ],
        out_specs=pl.BlockSpec((None, t, D_MODEL), per_n),
        out_shape=jax.ShapeDtypeStruct((n, t, D_MODEL), _F32),
        scratch_shapes=[
            pltpu.VMEM((rows, D_MODEL), _BF16),
            pltpu.VMEM((rows, 1), _F32),
            pltpu.VMEM((rows, 1), _F32),
            pltpu.VMEM((rows, D_MODEL), _F32),
        ],
        compiler_params=_params(("parallel", "arbitrary")),
        name="cached_attn",
    )(q, cache_k, cache_v, k_new, v_new, bias, bias_new)


def _router_kernel(h_ref, g_ref, wr_ref, gate_ref):
    u = _rms_unit(h_ref[...]) * g_ref[...]
    logits = jnp.dot(u, wr_ref[...], preferred_element_type=_F32, precision=lax.Precision.HIGHEST)
    lane = lax.broadcasted_iota(jnp.int32, logits.shape, 1).astype(_F32)
    neg = -jnp.inf
    lg = jnp.where(lane < N_EXPERTS, logits, neg)
    v1 = jnp.max(lg, axis=1, keepdims=True)
    i1 = jnp.min(jnp.where(lg == v1, lane, float(LANES)), axis=1, keepdims=True)
    lg2 = jnp.where(lane == i1, neg, lg)
    v2 = jnp.max(lg2, axis=1, keepdims=True)
    i2 = jnp.min(jnp.where(lg2 == v2, lane, float(LANES)), axis=1, keepdims=True)
    e2 = jnp.exp(v2 - v1)
    den = 1.0 + e2
    gate_ref[...] = jnp.where(lane == i1, 1.0 / den, 0.0) + jnp.where(lane == i2, e2 / den, 0.0)


def _router(h, gain, w_router, tm):
    m = h.shape[0]
    wr = jnp.zeros((D_MODEL, LANES), _F32).at[:, :N_EXPERTS].set(w_router)
    return pl.pallas_call(
        _router_kernel,
        grid=(m // tm,),
        in_specs=[
            pl.BlockSpec((tm, D_MODEL), lambda i: (i, 0)),
            pl.BlockSpec((1, D_MODEL), lambda i: (0, 0)),
            pl.BlockSpec((D_MODEL, LANES), lambda i: (0, 0)),
        ],
        out_specs=pl.BlockSpec((tm, LANES), lambda i: (i, 0)),
        out_shape=jax.ShapeDtypeStruct((m, LANES), _F32),
        compiler_params=_params(("parallel",)),
        name="router",
    )(h, gain.reshape(1, D_MODEL), wr)


def _tile(m, target):
    return min(m, target)


def kernel(x_prompt, x_sample, state_pool, cache_k_win, cache_v_win, norm_mix, norm_ffn, pool_w, pool_scale,
           norm_kv, w_kv, w_q, w_o, w_ffn_gu, w_ffn_down, w_router, w_exp_gu, w_exp_down, norm_final):
    nb, seq, _ = x_prompt.shape
    ns, ts, _ = x_sample.shape
    n_ctx = cache_k_win.shape[1]
    w_kv_b = w_kv.astype(_BF16)
    w_q_b = w_q[0].astype(_BF16)
    w_o_b = w_o[0].astype(_BF16)
    w_ffn_gu_b = w_ffn_gu.astype(_BF16)
    w_ffn_down_b = w_ffn_down.astype(_BF16)
    w_exp_gu_b = w_exp_gu[0].astype(_BF16)
    w_exp_down_b = w_exp_down[0].astype(_BF16)

    def token_layers_tail(h, tm):
        gate = _router(h, norm_ffn[1], w_router[0], tm)
        return _ffn_layer(h, norm_ffn[1], w_exp_gu_b, w_exp_down_b, tm, w_exp_down_b.shape[1] // 2,
                          gate=gate, final_gain=norm_final)

    h, pool_prompt = _pool_layer(x_prompt, None, norm_mix[0], pool_w[0], pool_scale[0], tm=512, bn=1)
    h = _ffn_layer(h.reshape(nb * seq, D_MODEL), norm_ffn[0], w_ffn_gu_b, w_ffn_down_b, 512,
                   w_ffn_down_b.shape[1] // 2)
    keep = min(MAX_WINDOW, seq)
    q, k, v, k_win_p, v_win_p = _qkv_layer(h.reshape(nb, seq, D_MODEL), jnp.arange(seq, dtype=jnp.int32),
                                           norm_kv, norm_mix[1], w_kv_b, w_q_b, tm=256, keep=keep)
    outs, lses = [], []
    for g in range(N_ATTN_GROUPS):
        o, lse = _band_attention(q, k, v, g, tb=2)
        outs.append(o.reshape(nb * seq, D_MODEL))
        lses.append(lse.reshape(nb * seq, LANES))
    h = _merge_proj(outs, lses, h, w_o_b, tm=512)
    y_prompt = token_layers_tail(h, 512).reshape(nb, seq, D_MODEL)

    hs, pool_sample = _pool_layer(x_sample, state_pool[0], norm_mix[0], pool_w[0], pool_scale[0], tm=ts, bn=8)
    ms = ns * ts
    hs = _ffn_layer(hs.reshape(ms, D_MODEL), norm_ffn[0], w_ffn_gu_b, w_ffn_down_b, ms, w_ffn_down_b.shape[1] // 2)
    pos_s = jnp.tile(PAST_LEN + jnp.arange(ts, dtype=jnp.int32), ns)
    qs, ks, vs, ks_f, vs_f = _qkv_layer(hs.reshape(1, ms, D_MODEL), pos_s, norm_kv, norm_mix[1], w_kv_b, w_q_b,
                                        tm=ms, keep=ms)
    ck = cache_k_win.reshape(ns, n_ctx, D_MODEL)
    cv = cache_v_win.reshape(ns, n_ctx, D_MODEL)
    os_ = _cached_attention(qs.reshape(ns, ts, -1), ks.reshape(ns, ts, D_MODEL), vs.reshape(ns, ts, D_MODEL),
                            ck, cv, chunk=512)
    hs = _out_proj(os_.reshape(ms, D_MODEL), hs, w_o_b)
    y_sample = token_layers_tail(hs, ms).reshape(ns, ts, D_MODEL)

    keep_s = min(MAX_WINDOW, n_ctx + ts)
    k_win_s = jnp.concatenate([ck, ks_f.reshape(ns, ts, D_MODEL)], axis=1)[:, -keep_s:]
    v_win_s = jnp.concatenate([cv, vs_f.reshape(ns, ts, D_MODEL)], axis=1)[:, -keep_s:]
    heads = lambda a: a.reshape(a.shape[0], a.shape[1], N_HEADS, HEAD_DIM)
    return (y_prompt, y_sample, pool_prompt, pool_sample, heads(k_win_p), heads(v_win_p),
            heads(k_win_s), heads(v_win_s))
```

```python
import functools
import math

import numpy as np
import jax
import jax.numpy as jnp
from jax import lax
from jax.experimental import pallas as pl
from jax.experimental.pallas import tpu as pltpu

D_MODEL = 1024
PAST_LEN = 16384
POOL_WINDOWS = (2, 4, 8, 16)
POOL_GROUP_DIM = D_MODEL // len(POOL_WINDOWS)
POOL_CTX = max(POOL_WINDOWS) - 1
HEAD_DIM = 64
N_HEADS = D_MODEL // HEAD_DIM
ATTN_GROUPS = ((128, 1), (512, 4), (2048, 16))
N_ATTN_GROUPS = len(ATTN_GROUPS)
BAND = 128
MAX_WINDOW = max(w for w, _ in ATTN_GROUPS)
ROT_DIM = HEAD_DIM // 4
ROPE_THETA = 500000.0
ATTN_SCALE = HEAD_DIM ** -0.5
N_EXPERTS = 8
TOP_K = 2
RMS_EPS = 1e-5

LANES = 128
HALO = 16
MASK_VALUE = -1e30
M_INIT = -1e20
VMEM_LIMIT = 52 * 1024 * 1024

TM_POOL = 512
TM_FFN = 512
TM_QKV = 256
TM_TOKEN = 256
TM_EXPERT = 512
ATTN_BANDS = 2
CACHE_CHUNK = 512

_F32 = jnp.float32
_BF16 = jnp.bfloat16
_NT = (((1,), (1,)), ((), ()))


def _params(semantics):
    return pltpu.CompilerParams(dimension_semantics=semantics, vmem_limit_bytes=VMEM_LIMIT)


def _rms_unit(x):
    return x * lax.rsqrt(jnp.mean(x * x, axis=-1, keepdims=True) + RMS_EPS)


def _dot(a, b):
    return jnp.dot(a, b, preferred_element_type=_F32)


def _swiglu_chunk(u, wg, wu, wd):
    g = _dot(u, wg)
    up = _dot(u, wu)
    return _dot((g * jax.nn.sigmoid(g) * up).astype(_BF16), wd)


def _pool_kernel(*refs, tm, n_ctx, bn):
    if n_ctx:
        x_ref, ctx_ref, g_ref, w_ref, sc_ref, h_ref, pool_ref, ext_ref = refs
    else:
        x_ref, g_ref, w_ref, sc_ref, h_ref, pool_ref, ext_ref = refs
        ctx_ref = None
    s = pl.program_id(1)
    pos = n_ctx + s * tm + lax.broadcasted_iota(jnp.int32, (tm, 1), 0)
    for b in range(bn):
        x = x_ref[b]
        u = _rms_unit(x) * g_ref[...]

        @pl.when(s == 0)
        def _():
            if n_ctx:
                ext_ref[b, 0:HALO - n_ctx, :] = jnp.zeros((HALO - n_ctx, D_MODEL), _F32)
                ext_ref[b, HALO - n_ctx:HALO, :] = ctx_ref[b]
            else:
                ext_ref[b, 0:HALO, :] = jnp.zeros((HALO, D_MODEL), _F32)

        ext_ref[b, HALO:HALO + tm, :] = u
        for g, w in enumerate(POOL_WINDOWS):
            sl = slice(g * POOL_GROUP_DIM, (g + 1) * POOL_GROUP_DIM)
            acc = u[:, sl]
            for k in range(1, w):
                acc = acc + ext_ref[b, HALO - k:HALO - k + tm, sl]
            cnt = jnp.minimum(pos + 1, w).astype(_F32)
            d = acc / cnt - u[:, sl]
            y = _dot(d.astype(_BF16), w_ref[g])
            h_ref[b, :, sl] = x[:, sl] + y * sc_ref[:, sl]

        @pl.when(s == pl.num_programs(1) - 1)
        def _():
            pool_ref[0, b] = ext_ref[b, HALO + tm - POOL_CTX:HALO + tm, :]

        ext_ref[b, 0:HALO, :] = ext_ref[b, tm:tm + HALO, :]


def _pool_layer(x, ctx, gain, w_pool, scale, tm, bn):
    n, t, _ = x.shape
    n_ctx = 0 if ctx is None else POOL_CTX
    grid = (n // bn, t // tm)
    row = lambda i, s: (i, s, 0)
    const2 = lambda i, s: (0, 0)
    in_specs = [pl.BlockSpec((bn, tm, D_MODEL), row)]
    args = [x]
    if ctx is not None:
        in_specs.append(pl.BlockSpec((bn, POOL_CTX, D_MODEL), lambda i, s: (i, 0, 0)))
        args.append(ctx)
    in_specs += [
        pl.BlockSpec((1, D_MODEL), const2),
        pl.BlockSpec((len(POOL_WINDOWS), POOL_GROUP_DIM, POOL_GROUP_DIM), lambda i, s: (0, 0, 0)),
        pl.BlockSpec((1, D_MODEL), const2),
    ]
    args += [gain.reshape(1, D_MODEL), w_pool.astype(_BF16), scale.reshape(1, D_MODEL)]
    return pl.pallas_call(
        functools.partial(_pool_kernel, tm=tm, n_ctx=n_ctx, bn=bn),
        grid=grid,
        in_specs=in_specs,
        out_specs=[
            pl.BlockSpec((bn, tm, D_MODEL), row),
            pl.BlockSpec((1, bn, POOL_CTX, D_MODEL), lambda i, s: (0, i, 0, 0)),
        ],
        out_shape=[
            jax.ShapeDtypeStruct((n, t, D_MODEL), _F32),
            jax.ShapeDtypeStruct((1, n, POOL_CTX, D_MODEL), _F32),
        ],
        scratch_shapes=[pltpu.VMEM((bn, HALO + tm, D_MODEL), _F32)],
        compiler_params=_params(("parallel", "arbitrary")),
        name="pool_layer",
    )(*args)


def _ffn_kernel(h_ref, g_ref, wg_ref, wu_ref, wd_ref, out_ref, u_sc, acc_sc):
    c = pl.program_id(1)

    @pl.when(c == 0)
    def _():
        u_sc[...] = (_rms_unit(h_ref[...]) * g_ref[...]).astype(_BF16)
        acc_sc[...] = jnp.zeros_like(acc_sc)

    acc_sc[...] += _swiglu_chunk(u_sc[...], wg_ref[...], wu_ref[...], wd_ref[...])

    @pl.when(c == pl.num_programs(1) - 1)
    def _():
        out_ref[...] = h_ref[...] + acc_sc[...]


def _ffn_layer(h, gain, w_gu, w_down, tm, tf):
    m = h.shape[0]
    f = w_down.shape[0]
    nc = f // tf
    row = lambda i, c: (i, 0)
    return pl.pallas_call(
        _ffn_kernel,
        grid=(m // tm, nc),
        in_specs=[
            pl.BlockSpec((tm, D_MODEL), row),
            pl.BlockSpec((1, D_MODEL), lambda i, c: (0, 0)),
            pl.BlockSpec((D_MODEL, tf), lambda i, c: (0, c)),
            pl.BlockSpec((D_MODEL, tf), lambda i, c: (0, nc + c)),
            pl.BlockSpec((tf, D_MODEL), lambda i, c: (c, 0)),
        ],
        out_specs=pl.BlockSpec((tm, D_MODEL), row),
        out_shape=jax.ShapeDtypeStruct((m, D_MODEL), _F32),
        scratch_shapes=[pltpu.VMEM((tm, D_MODEL), _BF16), pltpu.VMEM((tm, D_MODEL), _F32)],
        compiler_params=_params(("parallel", "arbitrary")),
        name="swiglu_dense",
    )(h, gain.reshape(1, D_MODEL), w_gu, w_gu, w_down)


def _rope_tables(pos):
    half = ROT_DIM // 2
    inv_freq = jnp.exp(-math.log(ROPE_THETA) * jnp.arange(half, dtype=_F32) / half)
    ang = pos.astype(_F32)[:, None] * inv_freq[None, :]
    cos, sin = jnp.cos(ang), jnp.sin(ang)
    t = pos.shape[0]
    rest = HEAD_DIM - ROT_DIM
    zero_h = jnp.zeros((t, half), _F32)
    zero_r = jnp.zeros((t, rest), _F32)
    c = jnp.concatenate([cos, cos, jnp.ones((t, rest), _F32)], axis=1)
    s1 = jnp.concatenate([zero_h, sin, zero_r], axis=1)
    s2 = jnp.concatenate([-sin, zero_h, zero_r], axis=1)
    rep = LANES // HEAD_DIM
    return tuple(jnp.tile(a, (1, rep)) for a in (c, s1, s2))


def _qkv_kernel(h_ref, gkv_ref, gq_ref, wkv_ref, wq_ref, c_ref, s1_ref, s2_ref,
                q_ref, k_ref, v_ref, kf_ref, vf_ref):
    hn = _rms_unit(h_ref[...])
    kv = _dot((hn * gkv_ref[...]).astype(_BF16), wkv_ref[...])
    q = _dot((hn * gq_ref[...]).astype(_BF16), wq_ref[...])
    c, s1, s2 = c_ref[...], s1_ref[...], s2_ref[...]
    half = ROT_DIM // 2

    def rope(t):
        return t * c + pltpu.roll(t, half, 1) * s1 + pltpu.roll(t, LANES - half, 1) * s2

    for j in range(D_MODEL // LANES):
        sl = slice(j * LANES, (j + 1) * LANES)
        t = rope(kv[:, sl])
        kf_ref[:, sl] = t
        k_ref[:, sl] = t.astype(_BF16)
    v = kv[:, D_MODEL:]
    vf_ref[...] = v
    v_ref[...] = v.astype(_BF16)
    for j in range(N_ATTN_GROUPS * D_MODEL // LANES):
        sl = slice(j * LANES, (j + 1) * LANES)
        q_ref[:, sl] = (rope(q[:, sl]) * ATTN_SCALE).astype(_BF16)


def _qkv_layer(h, pos, norm_kv, norm_q, w_kv, w_q, tm, keep):
    n, t, _ = h.shape
    off = (t - keep) // tm
    tables = _rope_tables(pos)
    row = lambda i, s: (i, s, 0)
    const2 = lambda i, s: (0, 0)
    tab = pl.BlockSpec((tm, LANES), lambda i, s: (s, 0))
    win = lambda i, s: (i, jnp.maximum(s - off, 0), 0)
    dq = N_ATTN_GROUPS * D_MODEL
    return pl.pallas_call(
        _qkv_kernel,
        grid=(n, t // tm),
        in_specs=[
            pl.BlockSpec((None, tm, D_MODEL), row),
            pl.BlockSpec((1, D_MODEL), const2),
            pl.BlockSpec((1, D_MODEL), const2),
            pl.BlockSpec((D_MODEL, 2 * D_MODEL), const2),
            pl.BlockSpec((D_MODEL, dq), const2),
            tab, tab, tab,
        ],
        out_specs=[
            pl.BlockSpec((None, tm, dq), row),
            pl.BlockSpec((None, tm, D_MODEL), row),
            pl.BlockSpec((None, tm, D_MODEL), row),
            pl.BlockSpec((None, tm, D_MODEL), win),
            pl.BlockSpec((None, tm, D_MODEL), win),
        ],
        out_shape=[
            jax.ShapeDtypeStruct((n, t, dq), _BF16),
            jax.ShapeDtypeStruct((n, t, D_MODEL), _BF16),
            jax.ShapeDtypeStruct((n, t, D_MODEL), _BF16),
            jax.ShapeDtypeStruct((n, keep, D_MODEL), _F32),
            jax.ShapeDtypeStruct((n, keep, D_MODEL), _F32),
        ],
        compiler_params=_params(("parallel", "arbitrary")),
        name="qkv_rope",
    )(h, norm_kv.reshape(1, D_MODEL), norm_q.reshape(1, D_MODEL), w_kv, w_q, *tables)


def _band_attn_kernel(q_ref, kh_ref, k_ref, vh_ref, v_ref, o_ref, lse_ref, kx_ref, vx_ref, *, tb):
    m_id = pl.program_id(2)
    kx_ref[0:BAND, :] = kh_ref[...]
    kx_ref[BAND:, :] = k_ref[...]
    vx_ref[0:BAND, :] = vh_ref[...]
    vx_ref[BAND:, :] = v_ref[...]
    row = lax.broadcasted_iota(jnp.int32, (BAND, 2 * BAND), 0)
    col = lax.broadcasted_iota(jnp.int32, (BAND, 2 * BAND), 1)
    dist = row - col + BAND
    bias = jnp.where(dist >= 0, jnp.where(dist <= BAND, 0.0, MASK_VALUE), MASK_VALUE)
    no_prev = jnp.where(m_id > 0, 0.0, MASK_VALUE)
    bias_first = jnp.where(col < BAND, bias + no_prev, bias)
    lane = lax.broadcasted_iota(jnp.int32, (BAND, LANES), 1)
    low = lane < HEAD_DIM
    for b in range(tb):
        bb = bias_first if b == 0 else bias
        rows = slice(b * BAND, (b + 1) * BAND)
        lse_t = jnp.zeros((BAND, LANES), _F32)
        for p in range(D_MODEL // LANES):
            sl = slice(p * LANES, (p + 1) * LANES)
            q2 = q_ref[rows, sl].astype(_F32)
            k2 = kx_ref[b * BAND:(b + 2) * BAND, sl]
            v2 = vx_ref[b * BAND:(b + 2) * BAND, sl]
            outs = []
            for hh in range(2):
                qm = (jnp.where(low, q2, 0.0) if hh == 0 else jnp.where(low, 0.0, q2)).astype(_BF16)
                s = lax.dot_general(qm, k2, _NT, preferred_element_type=_F32) + bb
                mx = jnp.max(s, axis=1, keepdims=True)
                pr = jnp.exp(s - mx)
                den = jnp.sum(pr, axis=1, keepdims=True)
                o = _dot(pr.astype(_BF16), v2)
                outs.append(o * (1.0 / den))
                lse_t = jnp.where(lane == 2 * p + hh, mx + jnp.log(den), lse_t)
            o_ref[rows, sl] = jnp.where(low, outs[0], outs[1]).astype(_BF16)
        lse_ref[rows, :] = lse_t


def _band_attention(q, k, v, group, tb):
    n, s, _ = k.shape
    _, dil = ATTN_GROUPS[group]
    length = s // dil
    rows = tb * BAND
    qv = q.reshape(n, length, dil * N_ATTN_GROUPS * D_MODEL)
    kv = k.reshape(n, length, dil * D_MODEL)
    vv = v.reshape(n, length, dil * D_MODEL)
    cur = lambda i, r, m: (i, m, r)
    halo = lambda i, r, m: (i, jnp.maximum(m * tb - 1, 0), r)
    o, lse = pl.pallas_call(
        functools.partial(_band_attn_kernel, tb=tb),
        grid=(n, dil, length // rows),
        in_specs=[
            pl.BlockSpec((None, rows, D_MODEL), lambda i, r, m: (i, m, r * N_ATTN_GROUPS + group)),
            pl.BlockSpec((None, BAND, D_MODEL), halo),
            pl.BlockSpec((None, rows, D_MODEL), cur),
            pl.BlockSpec((None, BAND, D_MODEL), halo),
            pl.BlockSpec((None, rows, D_MODEL), cur),
        ],
        out_specs=[
            pl.BlockSpec((None, rows, D_MODEL), cur),
            pl.BlockSpec((None, rows, LANES), cur),
        ],
        out_shape=[
            jax.ShapeDtypeStruct((n, length, dil * D_MODEL), _BF16),
            jax.ShapeDtypeStruct((n, length, dil * LANES), _F32),
        ],
        scratch_shapes=[pltpu.VMEM((BAND + rows, D_MODEL), _BF16), pltpu.VMEM((BAND + rows, D_MODEL), _BF16)],
        compiler_params=_params(("parallel", "parallel", "arbitrary")),
        name="band_attn_g%d" % group,
    )(qv, kv, kv, vv, vv)
    return o.reshape(n, s, D_MODEL), lse.reshape(n, s, LANES)


def _head_expand_matrix():
    e = np.zeros((LANES, N_ATTN_GROUPS * D_MODEL), np.float32)
    for part in range(2):
        for g in range(N_ATTN_GROUPS):
            for h in range(N_HEADS):
                r = part * N_ATTN_GROUPS * N_HEADS + g * N_HEADS + h
                e[r, g * D_MODEL + h * HEAD_DIM:g * D_MODEL + (h + 1) * HEAD_DIM] = 1.0
    return jnp.asarray(e, _BF16)


def _merge_proj_kernel(o0_ref, o1_ref, o2_ref, l0_ref, l1_ref, l2_ref, e_ref, h_ref, os_ref, hs_ref, wo_ref,
                       out_ref, *, n_merge):
    i = pl.program_id(0)

    @pl.when(i < n_merge)
    def _():
        l0, l1, l2 = l0_ref[...], l1_ref[...], l2_ref[...]
        mx = jnp.maximum(jnp.maximum(l0, l1), l2)
        e0, e1, e2 = jnp.exp(l0 - mx), jnp.exp(l1 - mx), jnp.exp(l2 - mx)
        inv = 1.0 / (e0 + e1 + e2)
        lane = lax.broadcasted_iota(jnp.int32, l0.shape, 1)
        head = lane < N_HEADS
        a = (jnp.where(head, e0 * inv, 0.0)
             + pltpu.roll(jnp.where(head, e1 * inv, 0.0), N_HEADS, 1)
             + pltpu.roll(jnp.where(head, e2 * inv, 0.0), 2 * N_HEADS, 1))
        hi = a.astype(_BF16).astype(_F32)
        a2 = hi + pltpu.roll(a - hi, N_ATTN_GROUPS * N_HEADS, 1)
        w = _dot(a2.astype(_BF16), e_ref[...])
        o = (w[:, 0:D_MODEL] * o0_ref[...].astype(_F32)
             + w[:, D_MODEL:2 * D_MODEL] * o1_ref[...].astype(_F32)
             + w[:, 2 * D_MODEL:] * o2_ref[...].astype(_F32))
        out_ref[...] = h_ref[...] + _dot(o.astype(_BF16), wo_ref[...])

    @pl.when(i >= n_merge)
    def _():
        out_ref[...] = hs_ref[...] + _dot(os_ref[...].astype(_BF16), wo_ref[...])


def _merge_proj(outs, lses, h, o_sample, h_sample, w_o, tm):
    m, ms = h.shape[0], h_sample.shape[0]
    n_merge = m // tm
    row = lambda i: (jnp.minimum(i, n_merge - 1), 0)
    srow = lambda i: (jnp.maximum(i - n_merge, 0), 0)
    const = lambda i: (0, 0)
    o_spec = pl.BlockSpec((tm, D_MODEL), row)
    l_spec = pl.BlockSpec((tm, LANES), row)
    return pl.pallas_call(
        functools.partial(_merge_proj_kernel, n_merge=n_merge),
        grid=((m + ms) // tm,),
        in_specs=[o_spec] * 3 + [l_spec] * 3 + [
            pl.BlockSpec((LANES, N_ATTN_GROUPS * D_MODEL), const),
            pl.BlockSpec((tm, D_MODEL), row),
            pl.BlockSpec((tm, D_MODEL), srow),
            pl.BlockSpec((tm, D_MODEL), srow),
            pl.BlockSpec((D_MODEL, D_MODEL), const),
        ],
        out_specs=pl.BlockSpec((tm, D_MODEL), lambda i: (i, 0)),
        out_shape=jax.ShapeDtypeStruct((m + ms, D_MODEL), _F32),
        compiler_params=_params(("parallel",)),
        name="merge_out_proj",
    )(*outs, *lses, _head_expand_matrix(), h, o_sample, h_sample, w_o)


def _cached_bias(t, n_ctx):
    rows = N_ATTN_GROUPS * N_HEADS * t
    cache = np.full((rows, n_ctx), MASK_VALUE, np.float32)
    new = np.full((rows, LANES), MASK_VALUE, np.float32)
    for g, (win, dil) in enumerate(ATTN_GROUPS):
        for j in range(t):
            idx = n_ctx + j - np.arange(win // dil + 1) * dil
            idx = idx[idx >= 0]
            for h in range(N_HEADS):
                r = (g * N_HEADS + h) * t + j
                cache[r, idx[idx < n_ctx]] = 0.0
                new[r, idx[idx >= n_ctx] - n_ctx] = 0.0
    return jnp.asarray(cache), jnp.asarray(new)


def _cached_attn_kernel(q_ref, kc_ref, vc_ref, kn_ref, vn_ref, bias_ref, biasn_ref, o_ref,
                        qbd_ref, m_ref, l_ref, acc_ref, *, t):
    c = pl.program_id(1)
    gh = N_HEADS * t
    row = lax.broadcasted_iota(jnp.int32, (gh, D_MODEL), 0)
    lane = lax.broadcasted_iota(jnp.int32, (gh, D_MODEL), 1)
    own = jnp.right_shift(row, int(math.log2(t))) == jnp.right_shift(lane, int(math.log2(HEAD_DIM)))

    @pl.when(c == 0)
    def _():
        q = q_ref[...].astype(_F32)
        for g in range(N_ATTN_GROUPS):
            qg = q[:, g * D_MODEL:(g + 1) * D_MODEL]
            tiled = jnp.concatenate([qg] * N_HEADS, axis=0)
            qbd_ref[g * gh:(g + 1) * gh, :] = jnp.where(own, tiled, 0.0).astype(_BF16)
        m_ref[...] = jnp.full(m_ref.shape, M_INIT, _F32)
        l_ref[...] = jnp.zeros_like(l_ref)
        acc_ref[...] = jnp.zeros_like(acc_ref)

    def update(s, weighted_values):
        m_old = m_ref[...]
        m_new = jnp.maximum(m_old, jnp.max(s, axis=1, keepdims=True))
        alpha = jnp.exp(m_old - m_new)
        pr = jnp.exp(s - m_new)
        l_ref[...] = alpha * l_ref[...] + jnp.sum(pr, axis=1, keepdims=True)
        acc_ref[...] = alpha * acc_ref[...] + weighted_values(pr.astype(_BF16))
        m_ref[...] = m_new

    vct = vc_ref[...].astype(_BF16)
    update(_dot(qbd_ref[...], kc_ref[...].astype(_BF16)) + bias_ref[...],
           lambda pr: lax.dot_general(pr, vct, _NT, preferred_element_type=_F32))

    @pl.when(c == pl.num_programs(1) - 1)
    def _():
        pad = jnp.zeros((LANES - t, D_MODEL), _F32)
        kn = jnp.concatenate([kn_ref[...].astype(_F32), pad], axis=0).astype(_BF16)
        vn = jnp.concatenate([vn_ref[...].astype(_F32), pad], axis=0).astype(_BF16)
        update(lax.dot_general(qbd_ref[...], kn, _NT, preferred_element_type=_F32) + biasn_ref[...],
               lambda pr: _dot(pr, vn))
        ms = [m_ref[g * gh:(g + 1) * gh, :] for g in range(N_ATTN_GROUPS)]
        mx = jnp.maximum(jnp.maximum(ms[0], ms[1]), ms[2])
        num = jnp.zeros((gh, D_MODEL), _F32)
        den = jnp.zeros((gh, 1), _F32)
        for g in range(N_ATTN_GROUPS):
            w = jnp.exp(ms[g] - mx)
            num = num + w * acc_ref[g * gh:(g + 1) * gh, :]
            den = den + w * l_ref[g * gh:(g + 1) * gh, :]
        on = jnp.where(own, num * (1.0 / den), 0.0)
        o = on[0:t, :]
        for h in range(1, N_HEADS):
            o = o + on[h * t:(h + 1) * t, :]
        o_ref[...] = o


def _cached_attention(q, k_new, v_new, cache_kt, cache_vt, chunk):
    n, t, _ = k_new.shape
    n_ctx = cache_kt.shape[2]
    bias, bias_new = _cached_bias(t, n_ctx)
    rows = N_ATTN_GROUPS * N_HEADS * t
    per_n = lambda i, c: (i, 0, 0)
    return pl.pallas_call(
        functools.partial(_cached_attn_kernel, t=t),
        grid=(n, n_ctx // chunk),
        in_specs=[
            pl.BlockSpec((None, t, N_ATTN_GROUPS * D_MODEL), per_n),
            pl.BlockSpec((None, D_MODEL, chunk), lambda i, c: (i, 0, c)),
            pl.BlockSpec((None, D_MODEL, chunk), lambda i, c: (i, 0, c)),
            pl.BlockSpec((None, t, D_MODEL), per_n),
            pl.BlockSpec((None, t, D_MODEL), per_n),
            pl.BlockSpec((rows, chunk), lambda i, c: (0, c)),
            pl.BlockSpec((rows, LANES), lambda i, c: (0, 0)),
        ],
        out_specs=pl.BlockSpec((None, t, D_MODEL), per_n),
        out_shape=jax.ShapeDtypeStruct((n, t, D_MODEL), _F32),
        scratch_shapes=[
            pltpu.VMEM((rows, D_MODEL), _BF16),
            pltpu.VMEM((rows, 1), _F32),
            pltpu.VMEM((rows, 1), _F32),
            pltpu.VMEM((rows, D_MODEL), _F32),
        ],
        compiler_params=_params(("parallel", "arbitrary")),
        name="cached_attn",
    )(q, cache_kt, cache_vt, k_new, v_new, bias, bias_new)


def _router_kernel(h_ref, g_ref, wr_ref, idx_ref, g1_ref, g2_ref):
    u = _rms_unit(h_ref[...]) * g_ref[...]
    logits = jnp.dot(u, wr_ref[...], preferred_element_type=_F32, precision=lax.Precision.HIGHEST)
    lane = lax.broadcasted_iota(jnp.int32, logits.shape, 1).astype(_F32)
    neg = -jnp.inf
    lg = jnp.where(lane < N_EXPERTS, logits, neg)
    v1 = jnp.max(lg, axis=1, keepdims=True)
    i1 = jnp.min(jnp.where(lg == v1, lane, float(LANES)), axis=1, keepdims=True)
    lg2 = jnp.where(lane == i1, neg, lg)
    v2 = jnp.max(lg2, axis=1, keepdims=True)
    i2 = jnp.min(jnp.where(lg2 == v2, lane, float(LANES)), axis=1, keepdims=True)
    e2 = jnp.exp(v2 - v1)
    den = 1.0 + e2
    idx_ref[...] = jnp.where(lane < LANES // 2, i1, i2)
    g1_ref[...] = jnp.broadcast_to(1.0 / den, g1_ref.shape)
    g2_ref[...] = jnp.broadcast_to(e2 / den, g2_ref.shape)


def _router(h, gain, w_router, tm):
    m = h.shape[0]
    wr = jnp.zeros((D_MODEL, LANES), _F32).at[:, :N_EXPERTS].set(w_router)
    tile = pl.BlockSpec((tm, LANES), lambda i: (i, 0))
    shape = jax.ShapeDtypeStruct((m, LANES), _F32)
    return pl.pallas_call(
        _router_kernel,
        grid=(m // tm,),
        in_specs=[
            pl.BlockSpec((tm, D_MODEL), lambda i: (i, 0)),
            pl.BlockSpec((1, D_MODEL), lambda i: (0, 0)),
            pl.BlockSpec((D_MODEL, LANES), lambda i: (0, 0)),
        ],
        out_specs=[tile, tile, tile],
        out_shape=[shape, shape, shape],
        compiler_params=_params(("parallel",)),
        name="router",
    )(h, gain.reshape(1, D_MODEL), wr)


def _routing_tables(idx_tile, tm, n_tiles):
    m = idx_tile.shape[0]
    e = jnp.stack([idx_tile[:, 0], idx_tile[:, LANES // 2]], axis=1).reshape(TOP_K * m).astype(jnp.int32)
    onehot = (e[:, None] == jnp.arange(N_EXPERTS, dtype=jnp.int32)[None, :]).astype(jnp.int32)
    running = jnp.cumsum(onehot, axis=0)
    counts = running[-1]
    rank = jnp.sum(onehot * running, axis=1) - 1
    padded = (counts + tm - 1) // tm * tm
    ends = jnp.cumsum(padded)
    starts = ends - padded
    pos = jnp.sum(onehot * starts[None, :], axis=1) + rank
    token = jnp.arange(TOP_K * m, dtype=jnp.int32) // TOP_K
    src = jnp.zeros((n_tiles * tm,), jnp.int32).at[pos].set(token, unique_indices=True)
    n_act = ends[-1] // tm
    tile_id = jnp.arange(n_tiles, dtype=jnp.int32)
    tile_expert = jnp.sum((tile_id[:, None] * tm >= ends[None, :]).astype(jnp.int32), axis=1)
    tile_expert = jnp.where(tile_id < n_act, tile_expert, jnp.take(tile_expert, n_act - 1))
    return src.reshape(n_tiles, 1, tm), tile_expert, n_act.reshape(1), pos.reshape(m, TOP_K)


def _expert_kernel(te_ref, na_ref, rows_ref, rows_next_ref, h_hbm, g_ref, wg_ref, wu_ref, wd_ref, out_ref,
                   xbuf, sem, u_sc, acc_sc, *, tm):
    del te_ref
    i = pl.program_id(0)
    c = pl.program_id(1)
    n_act = na_ref[0]
    slot = lax.rem(i, 2)
    active = i < n_act

    def row_copy(rows, r, dst_slot):
        return pltpu.make_async_copy(h_hbm.at[pl.ds(rows[0, 0, r], 1)], xbuf.at[dst_slot, pl.ds(r, 1)],
                                     sem.at[dst_slot])

    def start_gather(rows, dst_slot):
        def body(r, carry):
            row_copy(rows, r, dst_slot).start()
            return carry
        lax.fori_loop(0, tm, body, 0, unroll=8)

    @pl.when((c == 0) & (i == 0))
    def _():
        start_gather(rows_ref, 0)

    @pl.when((c == 0) & (i + 1 < n_act))
    def _():
        start_gather(rows_next_ref, 1 - slot)

    @pl.when((c == 0) & active)
    def _():
        pltpu.make_async_copy(h_hbm.at[pl.ds(0, tm)], xbuf.at[slot], sem.at[slot]).wait()
        u_sc[...] = (_rms_unit(xbuf[slot]) * g_ref[...]).astype(_BF16)
        acc_sc[...] = jnp.zeros_like(acc_sc)

    @pl.when(active)
    def _():
        acc_sc[...] += _swiglu_chunk(u_sc[...], wg_ref[...], wu_ref[...], wd_ref[...])

    @pl.when(c == pl.num_programs(1) - 1)
    def _():
        out_ref[...] = jnp.where(active, acc_sc[...], 0.0)


def _expert_layer(h, src_rows, tile_expert, n_act, gain, w_gu, w_down, tm, tf):
    n_tiles = src_rows.shape[0]
    f = w_down.shape[1]
    nc = f // tf
    smem_rows = lambda index_map: pl.BlockSpec((1, 1, tm), index_map, memory_space=pltpu.SMEM)
    grid_spec = pltpu.PrefetchScalarGridSpec(
        num_scalar_prefetch=2,
        grid=(n_tiles, nc),
        in_specs=[
            smem_rows(lambda i, c, te, na: (i, 0, 0)),
            smem_rows(lambda i, c, te, na: (jnp.minimum(i + 1, n_tiles - 1), 0, 0)),
            pl.BlockSpec(memory_space=pl.ANY),
            pl.BlockSpec((1, D_MODEL), lambda i, c, te, na: (0, 0)),
            pl.BlockSpec((None, D_MODEL, tf), lambda i, c, te, na: (te[i], 0, c)),
            pl.BlockSpec((None, D_MODEL, tf), lambda i, c, te, na: (te[i], 0, nc + c)),
            pl.BlockSpec((None, tf, D_MODEL), lambda i, c, te, na: (te[i], c, 0)),
        ],
        out_specs=pl.BlockSpec((tm, D_MODEL), lambda i, c, te, na: (i, 0)),
        scratch_shapes=[
            pltpu.VMEM((2, tm, D_MODEL), _F32),
            pltpu.SemaphoreType.DMA((2,)),
            pltpu.VMEM((tm, D_MODEL), _BF16),
            pltpu.VMEM((tm, D_MODEL), _F32),
        ],
    )
    return pl.pallas_call(
        functools.partial(_expert_kernel, tm=tm),
        grid_spec=grid_spec,
        out_shape=jax.ShapeDtypeStruct((n_tiles * tm, D_MODEL), _F32),
        compiler_params=_params(("arbitrary", "arbitrary")),
        name="swiglu_routed",
    )(tile_expert, n_act, src_rows, src_rows, h, gain.reshape(1, D_MODEL), w_gu, w_gu, w_down)


def _combine_kernel(pos_ref, pos_next_ref, ys_hbm, h_ref, g1_ref, g2_ref, gf_ref, out_ref, buf, sem, *, tq):
    j = pl.program_id(0)
    slot = lax.rem(j, 2)

    def start_gather(pos, dst_slot):
        for k in range(TOP_K):
            def body(r, carry):
                pltpu.make_async_copy(ys_hbm.at[pl.ds(pos[0, k, r], 1)], buf.at[dst_slot, k, pl.ds(r, 1)],
                                      sem.at[dst_slot]).start()
                return carry
            lax.fori_loop(0, tq, body, 0, unroll=8)

    @pl.when(j == 0)
    def _():
        start_gather(pos_ref, 0)

    @pl.when(j + 1 < pl.num_programs(0))
    def _():
        start_gather(pos_next_ref, 1 - slot)

    for k in range(TOP_K):
        pltpu.make_async_copy(ys_hbm.at[pl.ds(0, tq)], buf.at[slot, k], sem.at[slot]).wait()
    rep = D_MODEL // LANES
    y = jnp.tile(g1_ref[...], (1, rep)) * buf[slot, 0] + jnp.tile(g2_ref[...], (1, rep)) * buf[slot, 1]
    out_ref[...] = _rms_unit(h_ref[...] + y) * gf_ref[...]


def _moe_combine(ys, h, g1, g2, pos, final_gain, tq, tile_off, n_tiles):
    pos3 = pos[tile_off * tq:(tile_off + n_tiles) * tq].reshape(n_tiles, tq, TOP_K).transpose(0, 2, 1)
    smem_pos = lambda index_map: pl.BlockSpec((1, TOP_K, tq), index_map, memory_space=pltpu.SMEM)
    tok = lambda width: pl.BlockSpec((tq, width), lambda j: (j + tile_off, 0))
    return pl.pallas_call(
        functools.partial(_combine_kernel, tq=tq),
        grid=(n_tiles,),
        in_specs=[
            smem_pos(lambda j: (j, 0, 0)),
            smem_pos(lambda j: (jnp.minimum(j + 1, n_tiles - 1), 0, 0)),
            pl.BlockSpec(memory_space=pl.ANY),
            tok(D_MODEL), tok(LANES), tok(LANES),
            pl.BlockSpec((1, D_MODEL), lambda j: (0, 0)),
        ],
        out_specs=pl.BlockSpec((tq, D_MODEL), lambda j: (j, 0)),
        out_shape=jax.ShapeDtypeStruct((n_tiles * tq, D_MODEL), _F32),
        scratch_shapes=[pltpu.VMEM((2, TOP_K, tq, D_MODEL), _F32), pltpu.SemaphoreType.DMA((2,))],
        compiler_params=_params(("arbitrary",)),
        name="moe_combine",
    )(pos3, pos3, ys, h, g1, g2, final_gain.reshape(1, D_MODEL))


def kernel(x_prompt, x_sample, state_pool, cache_k_win, cache_v_win, norm_mix, norm_ffn, pool_w, pool_scale,
           norm_kv, w_kv, w_q, w_o, w_ffn_gu, w_ffn_down, w_router, w_exp_gu, w_exp_down, norm_final):
    nb, seq, _ = x_prompt.shape
    ns, ts, _ = x_sample.shape
    n_ctx = cache_k_win.shape[1]
    mp, ms = nb * seq, ns * ts
    w_kv_b = w_kv.astype(_BF16)
    w_q_b = w_q[0].astype(_BF16)
    w_o_b = w_o[0].astype(_BF16)
    w_ffn_gu_b = w_ffn_gu[0].astype(_BF16)
    w_ffn_down_b = w_ffn_down[0].astype(_BF16)
    w_exp_gu_b = w_exp_gu[0].astype(_BF16)
    w_exp_down_b = w_exp_down[0].astype(_BF16)
    tf_ffn = w_ffn_down_b.shape[0] // 2
    tf_exp = w_exp_down_b.shape[1] // 2

    h, pool_prompt = _pool_layer(x_prompt, None, norm_mix[0], pool_w[0], pool_scale[0], tm=TM_POOL, bn=1)
    h = _ffn_layer(h.reshape(mp, D_MODEL), norm_ffn[0], w_ffn_gu_b, w_ffn_down_b, TM_FFN, tf_ffn)
    keep = min(MAX_WINDOW, seq)
    q, k, v, k_win_p, v_win_p = _qkv_layer(h.reshape(nb, seq, D_MODEL), jnp.arange(seq, dtype=jnp.int32),
                                           norm_kv, norm_mix[1], w_kv_b, w_q_b, tm=TM_QKV, keep=keep)
    outs, lses = [], []
    for g in range(N_ATTN_GROUPS):
        o, lse = _band_attention(q, k, v, g, tb=ATTN_BANDS)
        outs.append(o.reshape(mp, D_MODEL))
        lses.append(lse.reshape(mp, LANES))

    hs, pool_sample = _pool_layer(x_sample, state_pool[0], norm_mix[0], pool_w[0], pool_scale[0], tm=ts, bn=8)
    hs = _ffn_layer(hs.reshape(ms, D_MODEL), norm_ffn[0], w_ffn_gu_b, w_ffn_down_b, ms, tf_ffn)
    pos_s = jnp.tile(PAST_LEN + jnp.arange(ts, dtype=jnp.int32), ns)
    qs, ks, vs, ks_f, vs_f = _qkv_layer(hs.reshape(1, ms, D_MODEL), pos_s, norm_kv, norm_mix[1], w_kv_b, w_q_b,
                                        tm=ms, keep=ms)
    ckt = cache_k_win.transpose(0, 2, 3, 1).reshape(ns, D_MODEL, n_ctx)
    cvt = cache_v_win.transpose(0, 2, 3, 1).reshape(ns, D_MODEL, n_ctx)
    os_ = _cached_attention(qs.reshape(ns, ts, -1), ks.reshape(ns, ts, D_MODEL), vs.reshape(ns, ts, D_MODEL),
                            ckt, cvt, chunk=CACHE_CHUNK)
    h_all = _merge_proj(outs, lses, h, os_.reshape(ms, D_MODEL), hs, w_o_b, tm=TM_TOKEN)

    m_all = mp + ms
    n_tiles = -(-TOP_K * m_all // TM_EXPERT) + N_EXPERTS
    idx_tile, g1, g2 = _router(h_all, norm_ffn[1], w_router[0], TM_TOKEN)
    src_rows, tile_expert, n_act, pos = _routing_tables(idx_tile, TM_EXPERT, n_tiles)
    ys = _expert_layer(h_all, src_rows, tile_expert, n_act, norm_ffn[1], w_exp_gu_b, w_exp_down_b,
                       TM_EXPERT, tf_exp)
    y_prompt = _moe_combine(ys, h_all, g1, g2, pos, norm_final, TM_TOKEN, 0, mp // TM_TOKEN)
    y_sample = _moe_combine(ys, h_all, g1, g2, pos, norm_final, TM_TOKEN, mp // TM_TOKEN, ms // TM_TOKEN)

    keep_s = min(MAX_WINDOW, n_ctx + ts)

    def window(cache_t, new_rows):
        new_t = new_rows.reshape(ns, ts, D_MODEL).transpose(0, 2, 1)
        full = jnp.concatenate([cache_t, new_t], axis=2)[:, :, -keep_s:]
        return full.reshape(ns, N_HEADS, HEAD_DIM, keep_s).transpose(0, 3, 1, 2)

    heads = lambda a: a.reshape(a.shape[0], a.shape[1], N_HEADS, HEAD_DIM)
    return (y_prompt.reshape(nb, seq, D_MODEL), y_sample.reshape(ns, ts, D_MODEL), pool_prompt, pool_sample,
            heads(k_win_p), heads(v_win_p), window(ckt, ks_f), window(cvt, vs_f))
```

```python
import functools
import math

import numpy as np
import jax
import jax.numpy as jnp
from jax import lax
from jax.experimental import pallas as pl
from jax.experimental.pallas import tpu as pltpu

D_MODEL = 1024
PAST_LEN = 16384
POOL_WINDOWS = (2, 4, 8, 16)
POOL_GROUP_DIM = D_MODEL // len(POOL_WINDOWS)
POOL_CTX = max(POOL_WINDOWS) - 1
HEAD_DIM = 64
N_HEADS = D_MODEL // HEAD_DIM
ATTN_GROUPS = ((128, 1), (512, 4), (2048, 16))
N_ATTN_GROUPS = len(ATTN_GROUPS)
BAND = 128
MAX_WINDOW = max(w for w, _ in ATTN_GROUPS)
ROT_DIM = HEAD_DIM // 4
ROPE_THETA = 500000.0
ATTN_SCALE = HEAD_DIM ** -0.5
N_EXPERTS = 8
TOP_K = 2
RMS_EPS = 1e-5

LANES = 128
HALO = 16
MASK_VALUE = -1e30
M_INIT = -1e20
VMEM_LIMIT = 52 * 1024 * 1024

TM_POOL = 512
TM_FFN = 512
TM_QKV = 256
TM_TOKEN = 256
TM_EXPERT = 512
ATTN_BANDS = 2
CACHE_CHUNK = 512

_F32 = jnp.float32
_BF16 = jnp.bfloat16
_NT = (((1,), (1,)), ((), ()))


def _params(semantics):
    return pltpu.CompilerParams(dimension_semantics=semantics, vmem_limit_bytes=VMEM_LIMIT)


def _rms_unit(x):
    return x * lax.rsqrt(jnp.mean(x * x, axis=-1, keepdims=True) + RMS_EPS)


def _dot(a, b):
    return jnp.dot(a, b, preferred_element_type=_F32)


def _swiglu_chunk(u, wg, wu, wd):
    g = _dot(u, wg)
    up = _dot(u, wu)
    return _dot((g * jax.nn.sigmoid(g) * up).astype(_BF16), wd)


def _pool_kernel(*refs, tm, n_ctx, bn):
    if n_ctx:
        x_ref, ctx_ref, g_ref, w_ref, sc_ref, h_ref, pool_ref, ext_ref = refs
    else:
        x_ref, g_ref, w_ref, sc_ref, h_ref, pool_ref, ext_ref = refs
        ctx_ref = None
    s = pl.program_id(1)
    pos = n_ctx + s * tm + lax.broadcasted_iota(jnp.int32, (tm, 1), 0)
    for b in range(bn):
        x = x_ref[b]
        u = _rms_unit(x) * g_ref[...]

        @pl.when(s == 0)
        def _():
            if n_ctx:
                ext_ref[b, 0:HALO - n_ctx, :] = jnp.zeros((HALO - n_ctx, D_MODEL), _F32)
                ext_ref[b, HALO - n_ctx:HALO, :] = ctx_ref[b]
            else:
                ext_ref[b, 0:HALO, :] = jnp.zeros((HALO, D_MODEL), _F32)

        ext_ref[b, HALO:HALO + tm, :] = u
        for g, w in enumerate(POOL_WINDOWS):
            sl = slice(g * POOL_GROUP_DIM, (g + 1) * POOL_GROUP_DIM)
            acc = u[:, sl]
            for k in range(1, w):
                acc = acc + ext_ref[b, HALO - k:HALO - k + tm, sl]
            cnt = jnp.minimum(pos + 1, w).astype(_F32)
            d = acc / cnt - u[:, sl]
            y = _dot(d.astype(_BF16), w_ref[g])
            h_ref[b, :, sl] = x[:, sl] + y * sc_ref[:, sl]

        @pl.when(s == pl.num_programs(1) - 1)
        def _():
            pool_ref[0, b] = ext_ref[b, HALO + tm - POOL_CTX:HALO + tm, :]

        ext_ref[b, 0:HALO, :] = ext_ref[b, tm:tm + HALO, :]


def _pool_layer(x, ctx, gain, w_pool, scale, tm, bn):
    n, t, _ = x.shape
    n_ctx = 0 if ctx is None else POOL_CTX
    grid = (n // bn, t // tm)
    row = lambda i, s: (i, s, 0)
    const2 = lambda i, s: (0, 0)
    in_specs = [pl.BlockSpec((bn, tm, D_MODEL), row)]
    args = [x]
    if ctx is not None:
        in_specs.append(pl.BlockSpec((bn, POOL_CTX, D_MODEL), lambda i, s: (i, 0, 0)))
        args.append(ctx)
    in_specs += [
        pl.BlockSpec((1, D_MODEL), const2),
        pl.BlockSpec((len(POOL_WINDOWS), POOL_GROUP_DIM, POOL_GROUP_DIM), lambda i, s: (0, 0, 0)),
        pl.BlockSpec((1, D_MODEL), const2),
    ]
    args += [gain.reshape(1, D_MODEL), w_pool.astype(_BF16), scale.reshape(1, D_MODEL)]
    return pl.pallas_call(
        functools.partial(_pool_kernel, tm=tm, n_ctx=n_ctx, bn=bn),
        grid=grid,
        in_specs=in_specs,
        out_specs=[
            pl.BlockSpec((bn, tm, D_MODEL), row),
            pl.BlockSpec((1, bn, POOL_CTX, D_MODEL), lambda i, s: (0, i, 0, 0)),
        ],
        out_shape=[
            jax.ShapeDtypeStruct((n, t, D_MODEL), _F32),
            jax.ShapeDtypeStruct((1, n, POOL_CTX, D_MODEL), _F32),
        ],
        scratch_shapes=[pltpu.VMEM((bn, HALO + tm, D_MODEL), _F32)],
        compiler_params=_params(("parallel", "arbitrary")),
        name="pool_layer",
    )(*args)


def _ffn_kernel(h_ref, g_ref, wg_ref, wu_ref, wd_ref, out_ref, u_sc, acc_sc):
    c = pl.program_id(1)

    @pl.when(c == 0)
    def _():
        u_sc[...] = (_rms_unit(h_ref[...]) * g_ref[...]).astype(_BF16)
        acc_sc[...] = jnp.zeros_like(acc_sc)

    acc_sc[...] += _swiglu_chunk(u_sc[...], wg_ref[...], wu_ref[...], wd_ref[...])

    @pl.when(c == pl.num_programs(1) - 1)
    def _():
        out_ref[...] = h_ref[...] + acc_sc[...]


def _ffn_layer(h, gain, w_gu, w_down, tm, tf):
    m = h.shape[0]
    f = w_down.shape[0]
    nc = f // tf
    row = lambda i, c: (i, 0)
    return pl.pallas_call(
        _ffn_kernel,
        grid=(m // tm, nc),
        in_specs=[
            pl.BlockSpec((tm, D_MODEL), row),
            pl.BlockSpec((1, D_MODEL), lambda i, c: (0, 0)),
            pl.BlockSpec((D_MODEL, tf), lambda i, c: (0, c)),
            pl.BlockSpec((D_MODEL, tf), lambda i, c: (0, nc + c)),
            pl.BlockSpec((tf, D_MODEL), lambda i, c: (c, 0)),
        ],
        out_specs=pl.BlockSpec((tm, D_MODEL), row),
        out_shape=jax.ShapeDtypeStruct((m, D_MODEL), _F32),
        scratch_shapes=[pltpu.VMEM((tm, D_MODEL), _BF16), pltpu.VMEM((tm, D_MODEL), _F32)],
        compiler_params=_params(("parallel", "arbitrary")),
        name="swiglu_dense",
    )(h, gain.reshape(1, D_MODEL), w_gu, w_gu, w_down)


def _rope_tables(pos):
    half = ROT_DIM // 2
    inv_freq = jnp.exp(-math.log(ROPE_THETA) * jnp.arange(half, dtype=_F32) / half)
    ang = pos.astype(_F32)[:, None] * inv_freq[None, :]
    cos, sin = jnp.cos(ang), jnp.sin(ang)
    t = pos.shape[0]
    rest = HEAD_DIM - ROT_DIM
    zero_h = jnp.zeros((t, half), _F32)
    zero_r = jnp.zeros((t, rest), _F32)
    c = jnp.concatenate([cos, cos, jnp.ones((t, rest), _F32)], axis=1)
    s1 = jnp.concatenate([zero_h, sin, zero_r], axis=1)
    s2 = jnp.concatenate([-sin, zero_h, zero_r], axis=1)
    rep = LANES // HEAD_DIM
    return tuple(jnp.tile(a, (1, rep)) for a in (c, s1, s2))


def _store_by_class(t, j, scr, slot, targets, tm):
    sl = slice(j * LANES, (j + 1) * LANES)
    if any(d > 1 for d, _ in targets):
        scr[slot] = t
    for d, ref in targets:
        if d == 1:
            ref[0, :, sl] = t.astype(_BF16)
        else:
            for r in range(d):
                ref[r, :, sl] = scr[slot, pl.ds(r, tm // d, stride=d), :].astype(_BF16)


def _qkv_kernel(*refs, q_dils, kv_dils, tm):
    h_ref, gkv_ref, gq_ref, wkv_ref, wq_ref, c_ref, s1_ref, s2_ref = refs[:8]
    outs = list(refs[8:])
    q_refs = [outs.pop(0) for _ in q_dils]
    k_refs = [outs.pop(0) for _ in kv_dils]
    v_refs = [outs.pop(0) for _ in kv_dils]
    kf_ref, vf_ref, q_scr, k_scr, v_scr = outs
    hn = _rms_unit(h_ref[...])
    kv = _dot((hn * gkv_ref[...]).astype(_BF16), wkv_ref[...])
    q = _dot((hn * gq_ref[...]).astype(_BF16), wq_ref[...])
    c, s1, s2 = c_ref[...], s1_ref[...], s2_ref[...]
    half = ROT_DIM // 2
    tiles = D_MODEL // LANES

    def rope(t):
        return t * c + pltpu.roll(t, half, 1) * s1 + pltpu.roll(t, LANES - half, 1) * s2

    for j in range(tiles):
        sl = slice(j * LANES, (j + 1) * LANES)
        t = rope(kv[:, sl])
        kf_ref[:, sl] = t
        _store_by_class(t, j, k_scr, j, list(zip(kv_dils, k_refs)), tm)
        t = kv[:, D_MODEL + j * LANES:D_MODEL + (j + 1) * LANES]
        vf_ref[:, sl] = t
        _store_by_class(t, j, v_scr, j, list(zip(kv_dils, v_refs)), tm)
    for g, d in enumerate(q_dils):
        for j in range(tiles):
            t = rope(q[:, g * D_MODEL + j * LANES:g * D_MODEL + (j + 1) * LANES]) * ATTN_SCALE
            _store_by_class(t, j, q_scr, g * tiles + j, [(d, q_refs[g])], tm)


def _qkv_layer(h, pos, norm_kv, norm_q, w_kv, w_q, tm, keep, q_dils):
    n, t, _ = h.shape
    off = (t - keep) // tm
    kv_dils = tuple(sorted(set(q_dils)))
    tables = _rope_tables(pos)
    const2 = lambda i, s: (0, 0)
    tab = pl.BlockSpec((tm, LANES), lambda i, s: (s, 0))
    win = lambda i, s: (i, jnp.maximum(s - off, 0), 0)
    by_class = lambda d: pl.BlockSpec((None, d, tm // d, D_MODEL), lambda i, s: (i, 0, s, 0))
    class_shape = lambda d: jax.ShapeDtypeStruct((n, d, t // d, D_MODEL), _BF16)
    out_dils = tuple(q_dils) + kv_dils + kv_dils
    tiles = D_MODEL // LANES
    res = pl.pallas_call(
        functools.partial(_qkv_kernel, q_dils=tuple(q_dils), kv_dils=kv_dils, tm=tm),
        grid=(n, t // tm),
        in_specs=[
            pl.BlockSpec((None, tm, D_MODEL), lambda i, s: (i, s, 0)),
            pl.BlockSpec((1, D_MODEL), const2),
            pl.BlockSpec((1, D_MODEL), const2),
            pl.BlockSpec((D_MODEL, 2 * D_MODEL), const2),
            pl.BlockSpec((D_MODEL, len(q_dils) * D_MODEL), const2),
            tab, tab, tab,
        ],
        out_specs=[by_class(d) for d in out_dils] + [pl.BlockSpec((None, tm, D_MODEL), win)] * 2,
        out_shape=[class_shape(d) for d in out_dils] + [jax.ShapeDtypeStruct((n, keep, D_MODEL), _F32)] * 2,
        scratch_shapes=[
            pltpu.VMEM((len(q_dils) * tiles, tm, LANES), _F32),
            pltpu.VMEM((tiles, tm, LANES), _F32),
            pltpu.VMEM((tiles, tm, LANES), _F32),
        ],
        compiler_params=_params(("parallel", "arbitrary")),
        name="qkv_rope",
    )(h, norm_kv.reshape(1, D_MODEL), norm_q.reshape(1, D_MODEL), w_kv, w_q, *tables)
    ng, nk = len(q_dils), len(kv_dils)
    qs = res[:ng]
    ks = dict(zip(kv_dils, res[ng:ng + nk]))
    vs = dict(zip(kv_dils, res[ng + nk:ng + 2 * nk]))
    return qs, ks, vs, res[-2], res[-1]


def _band_attn_kernel(q_ref, kh_ref, k_ref, vh_ref, v_ref, o_ref, lse_ref, kx_ref, vx_ref, *, tb):
    m_id = pl.program_id(2)
    kx_ref[0:BAND, :] = kh_ref[...]
    kx_ref[BAND:, :] = k_ref[...]
    vx_ref[0:BAND, :] = vh_ref[...]
    vx_ref[BAND:, :] = v_ref[...]
    row = lax.broadcasted_iota(jnp.int32, (BAND, 2 * BAND), 0)
    col = lax.broadcasted_iota(jnp.int32, (BAND, 2 * BAND), 1)
    dist = row - col + BAND
    bias = jnp.where(dist >= 0, jnp.where(dist <= BAND, 0.0, MASK_VALUE), MASK_VALUE)
    no_prev = jnp.where(m_id > 0, 0.0, MASK_VALUE)
    bias_first = jnp.where(col < BAND, bias + no_prev, bias)
    lane = lax.broadcasted_iota(jnp.int32, (BAND, LANES), 1)
    low = lane < HEAD_DIM
    for b in range(tb):
        bb = bias_first if b == 0 else bias
        rows = slice(b * BAND, (b + 1) * BAND)
        lse_t = jnp.zeros((BAND, LANES), _F32)
        for p in range(D_MODEL // LANES):
            sl = slice(p * LANES, (p + 1) * LANES)
            q2 = q_ref[rows, sl].astype(_F32)
            k2 = kx_ref[b * BAND:(b + 2) * BAND, sl]
            v2 = vx_ref[b * BAND:(b + 2) * BAND, sl]
            outs = []
            for hh in range(2):
                qm = (jnp.where(low, q2, 0.0) if hh == 0 else jnp.where(low, 0.0, q2)).astype(_BF16)
                s = lax.dot_general(qm, k2, _NT, preferred_element_type=_F32) + bb
                mx = jnp.max(s, axis=1, keepdims=True)
                pr = jnp.exp(s - mx)
                den = jnp.sum(pr, axis=1, keepdims=True)
                o = _dot(pr.astype(_BF16), v2)
                outs.append(o * (1.0 / den))
                lse_t = jnp.where(lane == 2 * p + hh, mx + jnp.log(den), lse_t)
            o_ref[rows, sl] = jnp.where(low, outs[0], outs[1]).astype(_BF16)
        lse_ref[rows, :] = lse_t


def _band_attention(q, k, v, group, tb):
    n, dil, length, _ = k.shape
    rows = tb * BAND
    cur = lambda i, r, m: (i, r, m, 0)
    halo = lambda i, r, m: (i, r, jnp.maximum(m * tb - 1, 0), 0)
    band = pl.BlockSpec((None, None, rows, D_MODEL), cur)
    prev = pl.BlockSpec((None, None, BAND, D_MODEL), halo)
    return pl.pallas_call(
        functools.partial(_band_attn_kernel, tb=tb),
        grid=(n, dil, length // rows),
        in_specs=[band, prev, band, prev, band],
        out_specs=[band, pl.BlockSpec((None, None, rows, LANES), cur)],
        out_shape=[
            jax.ShapeDtypeStruct((n, dil, length, D_MODEL), _BF16),
            jax.ShapeDtypeStruct((n, dil, length, LANES), _F32),
        ],
        scratch_shapes=[pltpu.VMEM((BAND + rows, D_MODEL), _BF16), pltpu.VMEM((BAND + rows, D_MODEL), _BF16)],
        compiler_params=_params(("parallel", "parallel", "arbitrary")),
        name="band_attn_g%d" % group,
    )(q, k, k, v, v)


def _head_expand_matrix():
    e = np.zeros((LANES, N_ATTN_GROUPS * D_MODEL), np.float32)
    for part in range(2):
        for g in range(N_ATTN_GROUPS):
            for h in range(N_HEADS):
                r = part * N_ATTN_GROUPS * N_HEADS + g * N_HEADS + h
                e[r, g * D_MODEL + h * HEAD_DIM:g * D_MODEL + (h + 1) * HEAD_DIM] = 1.0
    return jnp.asarray(e, _BF16)


def _merge_proj_kernel(o0_ref, o1_ref, o2_ref, l0_ref, l1_ref, l2_ref, e_ref, h_ref, os_ref, hs_ref, wo_ref,
                       out_ref, o1_scr, o2_scr, l1_scr, l2_scr, o_scr, *, n_merge, tm):
    i = pl.program_id(0)
    tiles = D_MODEL // LANES

    @pl.when(i < n_merge)
    def _():
        for o_ref, l_ref, o_dst, l_dst in ((o1_ref, l1_ref, o1_scr, l1_scr), (o2_ref, l2_ref, o2_scr, l2_scr)):
            d = o_ref.shape[0]
            for r in range(d):
                l_dst[pl.ds(r, tm // d, stride=d), :] = l_ref[r]
                for j in range(tiles):
                    o_dst[j, pl.ds(r, tm // d, stride=d), :] = o_ref[r, :, j * LANES:(j + 1) * LANES].astype(_F32)
        l0, l1, l2 = l0_ref[0], l1_scr[...], l2_scr[...]
        mx = jnp.maximum(jnp.maximum(l0, l1), l2)
        e0, e1, e2 = jnp.exp(l0 - mx), jnp.exp(l1 - mx), jnp.exp(l2 - mx)
        inv = 1.0 / (e0 + e1 + e2)
        lane = lax.broadcasted_iota(jnp.int32, l0.shape, 1)
        head = lane < N_HEADS
        a = (jnp.where(head, e0 * inv, 0.0)
             + pltpu.roll(jnp.where(head, e1 * inv, 0.0), N_HEADS, 1)
             + pltpu.roll(jnp.where(head, e2 * inv, 0.0), 2 * N_HEADS, 1))
        hi = a.astype(_BF16).astype(_F32)
        a2 = hi + pltpu.roll(a - hi, N_ATTN_GROUPS * N_HEADS, 1)
        w = _dot(a2.astype(_BF16), e_ref[...])
        for j in range(tiles):
            sl = slice(j * LANES, (j + 1) * LANES)
            o = (w[:, sl] * o0_ref[0, :, sl].astype(_F32)
                 + w[:, D_MODEL + j * LANES:D_MODEL + (j + 1) * LANES] * o1_scr[j]
                 + w[:, 2 * D_MODEL + j * LANES:2 * D_MODEL + (j + 1) * LANES] * o2_scr[j])
            o_scr[:, sl] = o.astype(_BF16)
        out_ref[...] = h_ref[...] + _dot(o_scr[...], wo_ref[...])

    @pl.when(i >= n_merge)
    def _():
        out_ref[...] = hs_ref[...] + _dot(os_ref[...].astype(_BF16), wo_ref[...])


def _merge_proj(outs, lses, h, o_sample, h_sample, w_o, tm):
    m, ms = h.shape[0], h_sample.shape[0]
    n, _, seq, _ = outs[0].shape
    n_merge = m // tm
    per_n = seq // tm
    row = lambda i: (jnp.minimum(i, n_merge - 1), 0)
    srow = lambda i: (jnp.maximum(i - n_merge, 0), 0)
    const = lambda i: (0, 0)

    def by_class(a):
        d, width = a.shape[1], a.shape[3]

        def index(i):
            ic = jnp.minimum(i, n_merge - 1)
            return (ic // per_n, 0, ic % per_n, 0)

        return pl.BlockSpec((None, d, tm // d, width), index)

    tiles = D_MODEL // LANES
    return pl.pallas_call(
        functools.partial(_merge_proj_kernel, n_merge=n_merge, tm=tm),
        grid=((m + ms) // tm,),
        in_specs=[by_class(a) for a in outs] + [by_class(a) for a in lses] + [
            pl.BlockSpec((LANES, N_ATTN_GROUPS * D_MODEL), const),
            pl.BlockSpec((tm, D_MODEL), row),
            pl.BlockSpec((tm, D_MODEL), srow),
            pl.BlockSpec((tm, D_MODEL), srow),
            pl.BlockSpec((D_MODEL, D_MODEL), const),
        ],
        out_specs=pl.BlockSpec((tm, D_MODEL), lambda i: (i, 0)),
        out_shape=jax.ShapeDtypeStruct((m + ms, D_MODEL), _F32),
        scratch_shapes=[
            pltpu.VMEM((tiles, tm, LANES), _F32),
            pltpu.VMEM((tiles, tm, LANES), _F32),
            pltpu.VMEM((tm, LANES), _F32),
            pltpu.VMEM((tm, LANES), _F32),
            pltpu.VMEM((tm, D_MODEL), _BF16),
        ],
        compiler_params=_params(("parallel",)),
        name="merge_out_proj",
    )(*outs, *lses, _head_expand_matrix(), h, o_sample, h_sample, w_o)


def _cached_bias(t, n_ctx):
    rows = N_ATTN_GROUPS * N_HEADS * t
    cache = np.full((rows, n_ctx), MASK_VALUE, np.float32)
    new = np.full((rows, LANES), MASK_VALUE, np.float32)
    for g, (win, dil) in enumerate(ATTN_GROUPS):
        for j in range(t):
            idx = n_ctx + j - np.arange(win // dil + 1) * dil
            idx = idx[idx >= 0]
            for h in range(N_HEADS):
                r = (g * N_HEADS + h) * t + j
                cache[r, idx[idx < n_ctx]] = 0.0
                new[r, idx[idx >= n_ctx] - n_ctx] = 0.0
    return cache, new


def _cached_attn_kernel(q0_ref, q1_ref, q2_ref, kc_ref, vc_ref, kx_ref, vx_ref, kn_ref, vn_ref, knt_ref, vnt_ref,
                        bias_ref, biasn_ref, o_ref, kw_ref, vw_ref, qbd_ref, m_ref, l_ref, acc_ref,
                        *, t, first_chunk):
    c = pl.program_id(1)
    last = pl.num_programs(1) - 1
    gh = N_HEADS * t
    row = lax.broadcasted_iota(jnp.int32, (gh, D_MODEL), 0)
    lane = lax.broadcasted_iota(jnp.int32, (gh, D_MODEL), 1)
    own = jnp.right_shift(row, int(math.log2(t))) == jnp.right_shift(lane, int(math.log2(HEAD_DIM)))

    @pl.when(c == 0)
    def _():
        for g, q_ref in enumerate((q0_ref, q1_ref, q2_ref)):
            tiled = jnp.concatenate([q_ref[...].astype(_F32)] * N_HEADS, axis=0)
            qbd_ref[g * gh:(g + 1) * gh, :] = jnp.where(own, tiled, 0.0).astype(_BF16)
        m_ref[...] = jnp.full(m_ref.shape, M_INIT, _F32)
        l_ref[...] = jnp.zeros_like(l_ref)
        acc_ref[...] = jnp.zeros_like(acc_ref)

    def update(rows, s, weighted_values):
        m_old = m_ref[rows, :]
        m_new = jnp.maximum(m_old, jnp.max(s, axis=1, keepdims=True))
        alpha = jnp.exp(m_old - m_new)
        pr = jnp.exp(s - m_new)
        l_ref[rows, :] = alpha * l_ref[rows, :] + jnp.sum(pr, axis=1, keepdims=True)
        acc_ref[rows, :] = alpha * acc_ref[rows, :] + weighted_values(pr.astype(_BF16))
        m_ref[rows, :] = m_new

    kc, vc = kc_ref[...], vc_ref[...]
    kct, vct = kc.astype(_BF16), vc.astype(_BF16)
    for g in range(N_ATTN_GROUPS):
        rows = slice(g * gh, (g + 1) * gh)

        @pl.when(c >= first_chunk[g])
        def _():
            update(rows, _dot(qbd_ref[rows, :], kct) + bias_ref[rows, :],
                   lambda pr: lax.dot_general(pr, vct, _NT, preferred_element_type=_F32))

    chunk = kc.shape[1]
    for cur, nxt, new_t, w_ref in ((kc, kx_ref, knt_ref, kw_ref), (vc, vx_ref, vnt_ref, vw_ref)):
        tail = jnp.where(c == last, new_t[...], nxt[...])
        ext = jnp.concatenate([cur, tail], axis=1)
        w_ref[...] = pltpu.roll(ext, chunk + LANES - t, 1)[:, 0:chunk]

    @pl.when(c == last)
    def _():
        pad = jnp.zeros((LANES - t, D_MODEL), _F32)
        kn = jnp.concatenate([kn_ref[...].astype(_F32), pad], axis=0).astype(_BF16)
        vn = jnp.concatenate([vn_ref[...].astype(_F32), pad], axis=0).astype(_BF16)
        update(slice(0, N_ATTN_GROUPS * gh),
               lax.dot_general(qbd_ref[...], kn, _NT, preferred_element_type=_F32) + biasn_ref[...],
               lambda pr: _dot(pr, vn))
        ms = [m_ref[g * gh:(g + 1) * gh, :] for g in range(N_ATTN_GROUPS)]
        mx = jnp.maximum(jnp.maximum(ms[0], ms[1]), ms[2])
        num = jnp.zeros((gh, D_MODEL), _F32)
        den = jnp.zeros((gh, 1), _F32)
        for g in range(N_ATTN_GROUPS):
            w = jnp.exp(ms[g] - mx)
            num = num + w * acc_ref[g * gh:(g + 1) * gh, :]
            den = den + w * l_ref[g * gh:(g + 1) * gh, :]
        on = jnp.where(own, num * (1.0 / den), 0.0)
        o = on[0:t, :]
        for h in range(1, N_HEADS):
            o = o + on[h * t:(h + 1) * t, :]
        o_ref[...] = o


def _cached_attention(qs, k_new, v_new, k_new_f32, v_new_f32, cache_kt, cache_vt, chunk):
    n, t, _ = k_new.shape
    n_ctx = cache_kt.shape[2]
    assert n_ctx == MAX_WINDOW and n_ctx % chunk == 0 and chunk % LANES == 0 and t <= LANES
    bias_np, bias_new = _cached_bias(t, n_ctx)
    gh = N_HEADS * t
    rows = N_ATTN_GROUPS * gh
    n_chunks = n_ctx // chunk
    first_chunk = tuple(
        min(c for c in range(n_chunks) if (bias_np[g * gh:(g + 1) * gh, c * chunk:(c + 1) * chunk] == 0.0).any())
        for g in range(N_ATTN_GROUPS))
    new_t = lambda a: jnp.pad(a.transpose(0, 2, 1), ((0, 0), (0, 0), (0, LANES - t)))
    per_n = lambda i, c: (i, 0, 0)
    cur = pl.BlockSpec((None, D_MODEL, chunk), lambda i, c: (i, 0, c))
    ahead = pl.BlockSpec((None, D_MODEL, LANES),
                         lambda i, c: (i, 0, jnp.minimum((c + 1) * (chunk // LANES), n_ctx // LANES - 1)))
    small = pl.BlockSpec((None, t, D_MODEL), per_n)
    small_t = pl.BlockSpec((None, D_MODEL, LANES), per_n)
    window = jax.ShapeDtypeStruct((n, D_MODEL, n_ctx), _F32)
    return pl.pallas_call(
        functools.partial(_cached_attn_kernel, t=t, first_chunk=first_chunk),
        grid=(n, n_chunks),
        in_specs=[
            small, small, small,
            cur, cur, ahead, ahead,
            small, small, small_t, small_t,
            pl.BlockSpec((rows, chunk), lambda i, c: (0, c)),
            pl.BlockSpec((rows, LANES), lambda i, c: (0, 0)),
        ],
        out_specs=[small, cur, cur],
        out_shape=[jax.ShapeDtypeStruct((n, t, D_MODEL), _F32), window, window],
        scratch_shapes=[
            pltpu.VMEM((rows, D_MODEL), _BF16),
            pltpu.VMEM((rows, 1), _F32),
            pltpu.VMEM((rows, 1), _F32),
            pltpu.VMEM((rows, D_MODEL), _F32),
        ],
        compiler_params=_params(("parallel", "arbitrary")),
        name="cached_attn",
    )(*qs, cache_kt, cache_vt, cache_kt, cache_vt, k_new, v_new, new_t(k_new_f32), new_t(v_new_f32),
      jnp.asarray(bias_np), jnp.asarray(bias_new))


def _router_kernel(h_ref, g_ref, wr_ref, idx_ref, g1_ref, g2_ref):
    u = _rms_unit(h_ref[...]) * g_ref[...]
    logits = jnp.dot(u, wr_ref[...], preferred_element_type=_F32, precision=lax.Precision.HIGHEST)
    lane = lax.broadcasted_iota(jnp.int32, logits.shape, 1).astype(_F32)
    neg = -jnp.inf
    lg = jnp.where(lane < N_EXPERTS, logits, neg)
    v1 = jnp.max(lg, axis=1, keepdims=True)
    i1 = jnp.min(jnp.where(lg == v1, lane, float(LANES)), axis=1, keepdims=True)
    lg2 = jnp.where(lane == i1, neg, lg)
    v2 = jnp.max(lg2, axis=1, keepdims=True)
    i2 = jnp.min(jnp.where(lg2 == v2, lane, float(LANES)), axis=1, keepdims=True)
    e2 = jnp.exp(v2 - v1)
    den = 1.0 + e2
    idx_ref[...] = jnp.where(lane < LANES // 2, i1, i2)
    g1_ref[...] = jnp.broadcast_to(1.0 / den, g1_ref.shape)
    g2_ref[...] = jnp.broadcast_to(e2 / den, g2_ref.shape)


def _router(h, gain, w_router, tm):
    m = h.shape[0]
    wr = jnp.zeros((D_MODEL, LANES), _F32).at[:, :N_EXPERTS].set(w_router)
    tile = pl.BlockSpec((tm, LANES), lambda i: (i, 0))
    shape = jax.ShapeDtypeStruct((m, LANES), _F32)
    return pl.pallas_call(
        _router_kernel,
        grid=(m // tm,),
        in_specs=[
            pl.BlockSpec((tm, D_MODEL), lambda i: (i, 0)),
            pl.BlockSpec((1, D_MODEL), lambda i: (0, 0)),
            pl.BlockSpec((D_MODEL, LANES), lambda i: (0, 0)),
        ],
        out_specs=[tile, tile, tile],
        out_shape=[shape, shape, shape],
        compiler_params=_params(("parallel",)),
        name="router",
    )(h, gain.reshape(1, D_MODEL), wr)


def _routing_tables(idx_tile, tm, n_tiles):
    m = idx_tile.shape[0]
    e = jnp.stack([idx_tile[:, 0], idx_tile[:, LANES // 2]], axis=1).reshape(TOP_K * m).astype(jnp.int32)
    onehot = (e[:, None] == jnp.arange(N_EXPERTS, dtype=jnp.int32)[None, :]).astype(jnp.int32)
    running = jnp.cumsum(onehot, axis=0)
    counts = running[-1]
    rank = jnp.sum(onehot * running, axis=1) - 1
    padded = (counts + tm - 1) // tm * tm
    ends = jnp.cumsum(padded)
    starts = ends - padded
    pos = jnp.sum(onehot * starts[None, :], axis=1) + rank
    token = jnp.arange(TOP_K * m, dtype=jnp.int32) // TOP_K
    src = jnp.zeros((n_tiles * tm,), jnp.int32).at[pos].set(token, unique_indices=True)
    n_act = ends[-1] // tm
    tile_id = jnp.arange(n_tiles, dtype=jnp.int32)
    tile_expert = jnp.sum((tile_id[:, None] * tm >= ends[None, :]).astype(jnp.int32), axis=1)
    tile_expert = jnp.where(tile_id < n_act, tile_expert, jnp.take(tile_expert, n_act - 1))
    return src.reshape(n_tiles, 1, tm), tile_expert, n_act.reshape(1), pos.reshape(m, TOP_K)


def _expert_kernel(te_ref, na_ref, rows_ref, rows_next_ref, h_hbm, g_ref, wg_ref, wu_ref, wd_ref, out_ref,
                   xbuf, sem, u_sc, acc_sc, *, tm, n_chunks):
    del te_ref
    i = pl.program_id(0)
    c = pl.program_id(1)
    n_act = na_ref[0]
    slot = lax.rem(i, 2)
    active = i < n_act

    def row_copy(rows, r, dst_slot):
        return pltpu.make_async_copy(h_hbm.at[pl.ds(rows[0, 0, r], 1)], xbuf.at[dst_slot, pl.ds(r, 1)],
                                     sem.at[dst_slot])

    @pl.when((c == 0) & (i == 0))
    def _():
        def body(r, carry):
            row_copy(rows_ref, r, 0).start()
            return carry
        lax.fori_loop(0, tm, body, 0, unroll=8)

    @pl.when((c == 0) & (i <= n_act))
    def _():
        pltpu.make_async_copy(h_hbm.at[pl.ds(0, tm)], xbuf.at[slot], sem.at[slot]).wait()

    per_step = tm // n_chunks
    for step in range(n_chunks):
        @pl.when(active & (c == step))
        def _():
            if step == 0:
                u_sc[...] = (_rms_unit(xbuf[slot]) * g_ref[...]).astype(_BF16)
            y = _swiglu_chunk(u_sc[...], wg_ref[...], wu_ref[...], wd_ref[...])
            acc_sc[...] = y if step == 0 else acc_sc[...] + y
            for r in range(step * per_step, (step + 1) * per_step):
                row_copy(rows_next_ref, r, 1 - slot).start()

    @pl.when(c == n_chunks - 1)
    def _():
        out_ref[...] = jnp.where(active, acc_sc[...], 0.0)


def _expert_layer(h, src_rows, tile_expert, n_act, gain, w_gu, w_down, tm, tf):
    n_tiles = src_rows.shape[0]
    f = w_down.shape[1]
    nc = f // tf
    smem_rows = lambda index_map: pl.BlockSpec((1, 1, tm), index_map, memory_space=pltpu.SMEM)
    grid_spec = pltpu.PrefetchScalarGridSpec(
        num_scalar_prefetch=2,
        grid=(n_tiles, nc),
        in_specs=[
            smem_rows(lambda i, c, te, na: (i, 0, 0)),
            smem_rows(lambda i, c, te, na: (jnp.minimum(i + 1, n_tiles - 1), 0, 0)),
            pl.BlockSpec(memory_space=pl.ANY),
            pl.BlockSpec((1, D_MODEL), lambda i, c, te, na: (0, 0)),
            pl.BlockSpec((None, D_MODEL, tf), lambda i, c, te, na: (te[i], 0, c)),
            pl.BlockSpec((None, D_MODEL, tf), lambda i, c, te, na: (te[i], 0, nc + c)),
            pl.BlockSpec((None, tf, D_MODEL), lambda i, c, te, na: (te[i], c, 0)),
        ],
        out_specs=pl.BlockSpec((tm, D_MODEL), lambda i, c, te, na: (i, 0)),
        scratch_shapes=[
            pltpu.VMEM((2, tm, D_MODEL), _F32),
            pltpu.SemaphoreType.DMA((2,)),
            pltpu.VMEM((tm, D_MODEL), _BF16),
            pltpu.VMEM((tm, D_MODEL), _F32),
        ],
    )
    return pl.pallas_call(
        functools.partial(_expert_kernel, tm=tm, n_chunks=nc),
        grid_spec=grid_spec,
        out_shape=jax.ShapeDtypeStruct((n_tiles * tm, D_MODEL), _F32),
        compiler_params=_params(("arbitrary", "arbitrary")),
        name="swiglu_routed",
    )(tile_expert, n_act, src_rows, src_rows, h, gain.reshape(1, D_MODEL), w_gu, w_gu, w_down)


def _combine_kernel(pos_ref, pos_next_ref, ys_hbm, h_ref, g1_ref, g2_ref, gf_ref, out_ref, buf, sem, *, tq):
    j = pl.program_id(0)
    slot = lax.rem(j, 2)

    def start_gather(pos, dst_slot):
        for k in range(TOP_K):
            def body(r, carry):
                pltpu.make_async_copy(ys_hbm.at[pl.ds(pos[0, k, r], 1)], buf.at[dst_slot, k, pl.ds(r, 1)],
                                      sem.at[dst_slot]).start()
                return carry
            lax.fori_loop(0, tq, body, 0, unroll=8)

    @pl.when(j == 0)
    def _():
        start_gather(pos_ref, 0)

    @pl.when(j + 1 < pl.num_programs(0))
    def _():
        start_gather(pos_next_ref, 1 - slot)

    for k in range(TOP_K):
        pltpu.make_async_copy(ys_hbm.at[pl.ds(0, tq)], buf.at[slot, k], sem.at[slot]).wait()
    rep = D_MODEL // LANES
    y = jnp.tile(g1_ref[...], (1, rep)) * buf[slot, 0] + jnp.tile(g2_ref[...], (1, rep)) * buf[slot, 1]
    out_ref[...] = _rms_unit(h_ref[...] + y) * gf_ref[...]


def _moe_combine(ys, h, g1, g2, pos, final_gain, tq, tile_off, n_tiles):
    pos3 = pos[tile_off * tq:(tile_off + n_tiles) * tq].reshape(n_tiles, tq, TOP_K).transpose(0, 2, 1)
    smem_pos = lambda index_map: pl.BlockSpec((1, TOP_K, tq), index_map, memory_space=pltpu.SMEM)
    tok = lambda width: pl.BlockSpec((tq, width), lambda j: (j + tile_off, 0))
    return pl.pallas_call(
        functools.partial(_combine_kernel, tq=tq),
        grid=(n_tiles,),
        in_specs=[
            smem_pos(lambda j: (j, 0, 0)),
            smem_pos(lambda j: (jnp.minimum(j + 1, n_tiles - 1), 0, 0)),
            pl.BlockSpec(memory_space=pl.ANY),
            tok(D_MODEL), tok(LANES), tok(LANES),
            pl.BlockSpec((1, D_MODEL), lambda j: (0, 0)),
        ],
        out_specs=pl.BlockSpec((tq, D_MODEL), lambda j: (j, 0)),
        out_shape=jax.ShapeDtypeStruct((n_tiles * tq, D_MODEL), _F32),
        scratch_shapes=[pltpu.VMEM((2, TOP_K, tq, D_MODEL), _F32), pltpu.SemaphoreType.DMA((2,))],
        compiler_params=_params(("arbitrary",)),
        name="moe_combine",
    )(pos3, pos3, ys, h, g1, g2, final_gain.reshape(1, D_MODEL))


def kernel(x_prompt, x_sample, state_pool, cache_k_win, cache_v_win, norm_mix, norm_ffn, pool_w, pool_scale,
           norm_kv, w_kv, w_q, w_o, w_ffn_gu, w_ffn_down, w_router, w_exp_gu, w_exp_down, norm_final):
    nb, seq, _ = x_prompt.shape
    ns, ts, _ = x_sample.shape
    n_ctx = cache_k_win.shape[1]
    mp, ms = nb * seq, ns * ts
    w_kv_b = w_kv.astype(_BF16)
    w_q_b = w_q[0].astype(_BF16)
    w_o_b = w_o[0].astype(_BF16)
    w_ffn_gu_b = w_ffn_gu[0].astype(_BF16)
    w_ffn_down_b = w_ffn_down[0].astype(_BF16)
    w_exp_gu_b = w_exp_gu[0].astype(_BF16)
    w_exp_down_b = w_exp_down[0].astype(_BF16)
    tf_ffn = w_ffn_down_b.shape[0] // 2
    tf_exp = w_exp_down_b.shape[1] // 2

    dils = tuple(d for _, d in ATTN_GROUPS)
    h, pool_prompt = _pool_layer(x_prompt, None, norm_mix[0], pool_w[0], pool_scale[0], tm=TM_POOL, bn=1)
    h = _ffn_layer(h.reshape(mp, D_MODEL), norm_ffn[0], w_ffn_gu_b, w_ffn_down_b, TM_FFN, tf_ffn)
    keep = min(MAX_WINDOW, seq)
    q, k, v, k_win_p, v_win_p = _qkv_layer(h.reshape(nb, seq, D_MODEL), jnp.arange(seq, dtype=jnp.int32),
                                           norm_kv, norm_mix[1], w_kv_b, w_q_b, tm=TM_QKV, keep=keep, q_dils=dils)
    outs, lses = [], []
    for g, d in enumerate(dils):
        o, lse = _band_attention(q[g], k[d], v[d], g, tb=ATTN_BANDS)
        outs.append(o)
        lses.append(lse)

    hs, pool_sample = _pool_layer(x_sample, state_pool[0], norm_mix[0], pool_w[0], pool_scale[0], tm=ts, bn=8)
    hs = _ffn_layer(hs.reshape(ms, D_MODEL), norm_ffn[0], w_ffn_gu_b, w_ffn_down_b, ms, tf_ffn)
    pos_s = jnp.tile(PAST_LEN + jnp.arange(ts, dtype=jnp.int32), ns)
    qs, ks, vs, ks_f, vs_f = _qkv_layer(hs.reshape(1, ms, D_MODEL), pos_s, norm_kv, norm_mix[1], w_kv_b, w_q_b,
                                        tm=ms, keep=ms, q_dils=(1,) * N_ATTN_GROUPS)
    ckt = cache_k_win.transpose(0, 2, 3, 1).reshape(ns, D_MODEL, n_ctx)
    cvt = cache_v_win.transpose(0, 2, 3, 1).reshape(ns, D_MODEL, n_ctx)
    per_sample = lambda a: a.reshape(ns, ts, D_MODEL)
    os_, kwt, vwt = _cached_attention([per_sample(a) for a in qs], per_sample(ks[1]), per_sample(vs[1]),
                                      per_sample(ks_f), per_sample(vs_f), ckt, cvt, chunk=CACHE_CHUNK)
    h_all = _merge_proj(outs, lses, h, os_.reshape(ms, D_MODEL), hs, w_o_b, tm=TM_TOKEN)

    m_all = mp + ms
    n_tiles = -(-TOP_K * m_all // TM_EXPERT) + N_EXPERTS + 1
    idx_tile, g1, g2 = _router(h_all, norm_ffn[1], w_router[0], TM_TOKEN)
    src_rows, tile_expert, n_act, pos = _routing_tables(idx_tile, TM_EXPERT, n_tiles)
    ys = _expert_layer(h_all, src_rows, tile_expert, n_act, norm_ffn[1], w_exp_gu_b, w_exp_down_b,
                       TM_EXPERT, tf_exp)
    y_prompt = _moe_combine(ys, h_all, g1, g2, pos, norm_final, TM_TOKEN, 0, mp // TM_TOKEN)
    y_sample = _moe_combine(ys, h_all, g1, g2, pos, norm_final, TM_TOKEN, mp // TM_TOKEN, ms // TM_TOKEN)

    heads = lambda a: a.reshape(a.shape[0], a.shape[1], N_HEADS, HEAD_DIM)
    heads_t = lambda a: a.reshape(ns, N_HEADS, HEAD_DIM, a.shape[2]).transpose(0, 3, 1, 2)
    return (y_prompt.reshape(nb, seq, D_MODEL), y_sample.reshape(ns, ts, D_MODEL), pool_prompt, pool_sample,
            heads(k_win_p), heads(v_win_p), heads_t(kwt), heads_t(vwt))
```

```python
import functools
import math

import numpy as np
import jax
import jax.numpy as jnp
from jax import lax
from jax.experimental import pallas as pl
from jax.experimental.pallas import tpu as pltpu

D_MODEL = 1024
PAST_LEN = 16384
POOL_WINDOWS = (2, 4, 8, 16)
POOL_GROUP_DIM = D_MODEL // len(POOL_WINDOWS)
POOL_CTX = max(POOL_WINDOWS) - 1
HEAD_DIM = 64
N_HEADS = D_MODEL // HEAD_DIM
ATTN_GROUPS = ((128, 1), (512, 4), (2048, 16))
N_ATTN_GROUPS = len(ATTN_GROUPS)
BAND = 128
MAX_WINDOW = max(w for w, _ in ATTN_GROUPS)
ROT_DIM = HEAD_DIM // 4
ROPE_THETA = 500000.0
ATTN_SCALE = HEAD_DIM ** -0.5
N_EXPERTS = 8
TOP_K = 2
RMS_EPS = 1e-5

LANES = 128
HALO = 16
MASK_VALUE = -1e30
M_INIT = -1e20
VMEM_LIMIT = 52 * 1024 * 1024

TM_POOL = 512
TM_FFN = 512
TM_QKV = 256
TM_TOKEN = 256
TM_EXPERT = 512
EXPERT_SUB_COLS = 256
GATHER_DMA_QUEUE = 1
ATTN_BANDS = 2
CACHE_CHUNK = 512

_F32 = jnp.float32
_BF16 = jnp.bfloat16
_NT = (((1,), (1,)), ((), ()))


def _params(semantics):
    return pltpu.CompilerParams(dimension_semantics=semantics, vmem_limit_bytes=VMEM_LIMIT)


def _rms_unit(x):
    return x * lax.rsqrt(jnp.mean(x * x, axis=-1, keepdims=True) + RMS_EPS)


def _dot(a, b):
    return jnp.dot(a, b, preferred_element_type=_F32)


def _swiglu_chunk(u, wg, wu, wd):
    g = _dot(u, wg)
    up = _dot(u, wu)
    return _dot((g * jax.nn.sigmoid(g) * up).astype(_BF16), wd)


def _pool_kernel(*refs, tm, n_ctx, bn):
    if n_ctx:
        x_ref, ctx_ref, g_ref, w_ref, sc_ref, h_ref, pool_ref, ext_ref = refs
    else:
        x_ref, g_ref, w_ref, sc_ref, h_ref, pool_ref, ext_ref = refs
        ctx_ref = None
    s = pl.program_id(1)
    pos = n_ctx + s * tm + lax.broadcasted_iota(jnp.int32, (tm, 1), 0)
    for b in range(bn):
        x = x_ref[b]
        u = _rms_unit(x) * g_ref[...]

        @pl.when(s == 0)
        def _():
            if n_ctx:
                ext_ref[b, 0:HALO - n_ctx, :] = jnp.zeros((HALO - n_ctx, D_MODEL), _F32)
                ext_ref[b, HALO - n_ctx:HALO, :] = ctx_ref[b]
            else:
                ext_ref[b, 0:HALO, :] = jnp.zeros((HALO, D_MODEL), _F32)

        ext_ref[b, HALO:HALO + tm, :] = u
        for g, w in enumerate(POOL_WINDOWS):
            sl = slice(g * POOL_GROUP_DIM, (g + 1) * POOL_GROUP_DIM)
            acc = u[:, sl]
            for k in range(1, w):
                acc = acc + ext_ref[b, HALO - k:HALO - k + tm, sl]
            cnt = jnp.minimum(pos + 1, w).astype(_F32)
            d = acc / cnt - u[:, sl]
            y = _dot(d.astype(_BF16), w_ref[g])
            h_ref[b, :, sl] = x[:, sl] + y * sc_ref[:, sl]

        @pl.when(s == pl.num_programs(1) - 1)
        def _():
            pool_ref[0, b] = ext_ref[b, HALO + tm - POOL_CTX:HALO + tm, :]

        ext_ref[b, 0:HALO, :] = ext_ref[b, tm:tm + HALO, :]


def _pool_layer(x, ctx, gain, w_pool, scale, tm, bn):
    n, t, _ = x.shape
    n_ctx = 0 if ctx is None else POOL_CTX
    grid = (n // bn, t // tm)
    row = lambda i, s: (i, s, 0)
    const2 = lambda i, s: (0, 0)
    in_specs = [pl.BlockSpec((bn, tm, D_MODEL), row)]
    args = [x]
    if ctx is not None:
        in_specs.append(pl.BlockSpec((bn, POOL_CTX, D_MODEL), lambda i, s: (i, 0, 0)))
        args.append(ctx)
    in_specs += [
        pl.BlockSpec((1, D_MODEL), const2),
        pl.BlockSpec((len(POOL_WINDOWS), POOL_GROUP_DIM, POOL_GROUP_DIM), lambda i, s: (0, 0, 0)),
        pl.BlockSpec((1, D_MODEL), const2),
    ]
    args += [gain.reshape(1, D_MODEL), w_pool.astype(_BF16), scale.reshape(1, D_MODEL)]
    return pl.pallas_call(
        functools.partial(_pool_kernel, tm=tm, n_ctx=n_ctx, bn=bn),
        grid=grid,
        in_specs=in_specs,
        out_specs=[
            pl.BlockSpec((bn, tm, D_MODEL), row),
            pl.BlockSpec((1, bn, POOL_CTX, D_MODEL), lambda i, s: (0, i, 0, 0)),
        ],
        out_shape=[
            jax.ShapeDtypeStruct((n, t, D_MODEL), _F32),
            jax.ShapeDtypeStruct((1, n, POOL_CTX, D_MODEL), _F32),
        ],
        scratch_shapes=[pltpu.VMEM((bn, HALO + tm, D_MODEL), _F32)],
        compiler_params=_params(("parallel", "arbitrary")),
        name="pool_layer",
    )(*args)


def _ffn_kernel(h_ref, g_ref, wg_ref, wu_ref, wd_ref, out_ref, u_sc, acc_sc):
    c = pl.program_id(1)

    @pl.when(c == 0)
    def _():
        u_sc[...] = (_rms_unit(h_ref[...]) * g_ref[...]).astype(_BF16)
        acc_sc[...] = jnp.zeros_like(acc_sc)

    acc_sc[...] += _swiglu_chunk(u_sc[...], wg_ref[...], wu_ref[...], wd_ref[...])

    @pl.when(c == pl.num_programs(1) - 1)
    def _():
        out_ref[...] = h_ref[...] + acc_sc[...]


def _ffn_layer(h, gain, w_gu, w_down, tm, tf):
    m = h.shape[0]
    f = w_down.shape[0]
    nc = f // tf
    row = lambda i, c: (i, 0)
    return pl.pallas_call(
        _ffn_kernel,
        grid=(m // tm, nc),
        in_specs=[
            pl.BlockSpec((tm, D_MODEL), row),
            pl.BlockSpec((1, D_MODEL), lambda i, c: (0, 0)),
            pl.BlockSpec((D_MODEL, tf), lambda i, c: (0, c)),
            pl.BlockSpec((D_MODEL, tf), lambda i, c: (0, nc + c)),
            pl.BlockSpec((tf, D_MODEL), lambda i, c: (c, 0)),
        ],
        out_specs=pl.BlockSpec((tm, D_MODEL), row),
        out_shape=jax.ShapeDtypeStruct((m, D_MODEL), _F32),
        scratch_shapes=[pltpu.VMEM((tm, D_MODEL), _BF16), pltpu.VMEM((tm, D_MODEL), _F32)],
        compiler_params=_params(("parallel", "arbitrary")),
        name="swiglu_dense",
    )(h, gain.reshape(1, D_MODEL), w_gu, w_gu, w_down)


def _rope_tables(pos):
    half = ROT_DIM // 2
    inv_freq = jnp.exp(-math.log(ROPE_THETA) * jnp.arange(half, dtype=_F32) / half)
    ang = pos.astype(_F32)[:, None] * inv_freq[None, :]
    cos, sin = jnp.cos(ang), jnp.sin(ang)
    t = pos.shape[0]
    rest = HEAD_DIM - ROT_DIM
    zero_h = jnp.zeros((t, half), _F32)
    zero_r = jnp.zeros((t, rest), _F32)
    c = jnp.concatenate([cos, cos, jnp.ones((t, rest), _F32)], axis=1)
    s1 = jnp.concatenate([zero_h, sin, zero_r], axis=1)
    s2 = jnp.concatenate([-sin, zero_h, zero_r], axis=1)
    rep = LANES // HEAD_DIM
    return tuple(jnp.tile(a, (1, rep)) for a in (c, s1, s2))


def _store_by_class(t, j, scr, slot, targets, tm):
    sl = slice(j * LANES, (j + 1) * LANES)
    if any(d > 1 for d, _ in targets):
        scr[slot] = t
    for d, ref in targets:
        if d == 1:
            ref[0, :, sl] = t.astype(_BF16)
        else:
            for r in range(d):
                ref[r, :, sl] = scr[slot, pl.ds(r, tm // d, stride=d), :].astype(_BF16)


def _qkv_kernel(*refs, q_dils, kv_dils, tm):
    h_ref, gkv_ref, gq_ref, wkv_ref, wq_ref, c_ref, s1_ref, s2_ref = refs[:8]
    outs = list(refs[8:])
    q_refs = [outs.pop(0) for _ in q_dils]
    k_refs = [outs.pop(0) for _ in kv_dils]
    v_refs = [outs.pop(0) for _ in kv_dils]
    kf_ref, vf_ref, q_scr, k_scr, v_scr = outs
    hn = _rms_unit(h_ref[...])
    kv = _dot((hn * gkv_ref[...]).astype(_BF16), wkv_ref[...])
    q = _dot((hn * gq_ref[...]).astype(_BF16), wq_ref[...])
    c, s1, s2 = c_ref[...], s1_ref[...], s2_ref[...]
    half = ROT_DIM // 2
    tiles = D_MODEL // LANES

    def rope(t):
        return t * c + pltpu.roll(t, half, 1) * s1 + pltpu.roll(t, LANES - half, 1) * s2

    for j in range(tiles):
        sl = slice(j * LANES, (j + 1) * LANES)
        t = rope(kv[:, sl])
        kf_ref[:, sl] = t
        _store_by_class(t, j, k_scr, j, list(zip(kv_dils, k_refs)), tm)
        t = kv[:, D_MODEL + j * LANES:D_MODEL + (j + 1) * LANES]
        vf_ref[:, sl] = t
        _store_by_class(t, j, v_scr, j, list(zip(kv_dils, v_refs)), tm)
    for g, d in enumerate(q_dils):
        for j in range(tiles):
            t = rope(q[:, g * D_MODEL + j * LANES:g * D_MODEL + (j + 1) * LANES]) * ATTN_SCALE
            _store_by_class(t, j, q_scr, g * tiles + j, [(d, q_refs[g])], tm)


def _qkv_layer(h, pos, norm_kv, norm_q, w_kv, w_q, tm, keep, q_dils):
    n, t, _ = h.shape
    off = (t - keep) // tm
    kv_dils = tuple(sorted(set(q_dils)))
    tables = _rope_tables(pos)
    const2 = lambda i, s: (0, 0)
    tab = pl.BlockSpec((tm, LANES), lambda i, s: (s, 0))
    win = lambda i, s: (i, jnp.maximum(s - off, 0), 0)
    by_class = lambda d: pl.BlockSpec((None, d, tm // d, D_MODEL), lambda i, s: (i, 0, s, 0))
    class_shape = lambda d: jax.ShapeDtypeStruct((n, d, t // d, D_MODEL), _BF16)
    out_dils = tuple(q_dils) + kv_dils + kv_dils
    tiles = D_MODEL // LANES
    res = pl.pallas_call(
        functools.partial(_qkv_kernel, q_dils=tuple(q_dils), kv_dils=kv_dils, tm=tm),
        grid=(n, t // tm),
        in_specs=[
            pl.BlockSpec((None, tm, D_MODEL), lambda i, s: (i, s, 0)),
            pl.BlockSpec((1, D_MODEL), const2),
            pl.BlockSpec((1, D_MODEL), const2),
            pl.BlockSpec((D_MODEL, 2 * D_MODEL), const2),
            pl.BlockSpec((D_MODEL, len(q_dils) * D_MODEL), const2),
            tab, tab, tab,
        ],
        out_specs=[by_class(d) for d in out_dils] + [pl.BlockSpec((None, tm, D_MODEL), win)] * 2,
        out_shape=[class_shape(d) for d in out_dils] + [jax.ShapeDtypeStruct((n, keep, D_MODEL), _F32)] * 2,
        scratch_shapes=[
            pltpu.VMEM((len(q_dils) * tiles, tm, LANES), _F32),
            pltpu.VMEM((tiles, tm, LANES), _F32),
            pltpu.VMEM((tiles, tm, LANES), _F32),
        ],
        compiler_params=_params(("parallel", "arbitrary")),
        name="qkv_rope",
    )(h, norm_kv.reshape(1, D_MODEL), norm_q.reshape(1, D_MODEL), w_kv, w_q, *tables)
    ng, nk = len(q_dils), len(kv_dils)
    qs = res[:ng]
    ks = dict(zip(kv_dils, res[ng:ng + nk]))
    vs = dict(zip(kv_dils, res[ng + nk:ng + 2 * nk]))
    return qs, ks, vs, res[-2], res[-1]


def _band_attn_kernel(q_ref, kh_ref, k_ref, vh_ref, v_ref, o_ref, lse_ref, kx_ref, vx_ref, *, tb):
    m_id = pl.program_id(2)
    kx_ref[0:BAND, :] = kh_ref[...]
    kx_ref[BAND:, :] = k_ref[...]
    vx_ref[0:BAND, :] = vh_ref[...]
    vx_ref[BAND:, :] = v_ref[...]
    row = lax.broadcasted_iota(jnp.int32, (BAND, 2 * BAND), 0)
    col = lax.broadcasted_iota(jnp.int32, (BAND, 2 * BAND), 1)
    dist = row - col + BAND
    bias = jnp.where(dist >= 0, jnp.where(dist <= BAND, 0.0, MASK_VALUE), MASK_VALUE)
    no_prev = jnp.where(m_id > 0, 0.0, MASK_VALUE)
    bias_first = jnp.where(col < BAND, bias + no_prev, bias)
    lane = lax.broadcasted_iota(jnp.int32, (BAND, LANES), 1)
    low = lane < HEAD_DIM
    for b in range(tb):
        bb = bias_first if b == 0 else bias
        rows = slice(b * BAND, (b + 1) * BAND)
        lse_t = jnp.zeros((BAND, LANES), _F32)
        for p in range(D_MODEL // LANES):
            sl = slice(p * LANES, (p + 1) * LANES)
            q2 = q_ref[rows, sl].astype(_F32)
            k2 = kx_ref[b * BAND:(b + 2) * BAND, sl]
            v2 = vx_ref[b * BAND:(b + 2) * BAND, sl]
            outs = []
            for hh in range(2):
                qm = (jnp.where(low, q2, 0.0) if hh == 0 else jnp.where(low, 0.0, q2)).astype(_BF16)
                s = lax.dot_general(qm, k2, _NT, preferred_element_type=_F32) + bb
                mx = jnp.max(s, axis=1, keepdims=True)
                pr = jnp.exp(s - mx)
                den = jnp.sum(pr, axis=1, keepdims=True)
                o = _dot(pr.astype(_BF16), v2)
                outs.append(o * (1.0 / den))
                lse_t = jnp.where(lane == 2 * p + hh, mx + jnp.log(den), lse_t)
            o_ref[rows, sl] = jnp.where(low, outs[0], outs[1]).astype(_BF16)
        lse_ref[rows, :] = lse_t


def _band_attention(q, k, v, group, tb):
    n, dil, length, _ = k.shape
    rows = tb * BAND
    cur = lambda i, r, m: (i, r, m, 0)
    halo = lambda i, r, m: (i, r, jnp.maximum(m * tb - 1, 0), 0)
    band = pl.BlockSpec((None, None, rows, D_MODEL), cur)
    prev = pl.BlockSpec((None, None, BAND, D_MODEL), halo)
    return pl.pallas_call(
        functools.partial(_band_attn_kernel, tb=tb),
        grid=(n, dil, length // rows),
        in_specs=[band, prev, band, prev, band],
        out_specs=[band, pl.BlockSpec((None, None, rows, LANES), cur)],
        out_shape=[
            jax.ShapeDtypeStruct((n, dil, length, D_MODEL), _BF16),
            jax.ShapeDtypeStruct((n, dil, length, LANES), _F32),
        ],
        scratch_shapes=[pltpu.VMEM((BAND + rows, D_MODEL), _BF16), pltpu.VMEM((BAND + rows, D_MODEL), _BF16)],
        compiler_params=_params(("parallel", "parallel", "arbitrary")),
        name="band_attn_g%d" % group,
    )(q, k, k, v, v)


def _head_expand_matrix():
    e = np.zeros((LANES, N_ATTN_GROUPS * D_MODEL), np.float32)
    for part in range(2):
        for g in range(N_ATTN_GROUPS):
            for h in range(N_HEADS):
                r = part * N_ATTN_GROUPS * N_HEADS + g * N_HEADS + h
                e[r, g * D_MODEL + h * HEAD_DIM:g * D_MODEL + (h + 1) * HEAD_DIM] = 1.0
    return jnp.asarray(e, _BF16)


def _merge_proj_kernel(o0_ref, o1_ref, o2_ref, l0_ref, l1_ref, l2_ref, e_ref, h_ref, os_ref, hs_ref, wo_ref,
                       gr_ref, wrh_ref, wrl_ref, out_ref, idx_ref, g1_ref, g2_ref,
                       o1_scr, o2_scr, l1_scr, l2_scr, o_scr, *, n_merge, tm):
    i = pl.program_id(0)
    tiles = D_MODEL // LANES

    @pl.when(i < n_merge)
    def _():
        for o_ref, l_ref, o_dst, l_dst in ((o1_ref, l1_ref, o1_scr, l1_scr), (o2_ref, l2_ref, o2_scr, l2_scr)):
            d = o_ref.shape[0]
            for r in range(d):
                l_dst[pl.ds(r, tm // d, stride=d), :] = l_ref[r]
                for j in range(tiles):
                    o_dst[j, pl.ds(r, tm // d, stride=d), :] = o_ref[r, :, j * LANES:(j + 1) * LANES].astype(_F32)
        l0, l1, l2 = l0_ref[0], l1_scr[...], l2_scr[...]
        mx = jnp.maximum(jnp.maximum(l0, l1), l2)
        e0, e1, e2 = jnp.exp(l0 - mx), jnp.exp(l1 - mx), jnp.exp(l2 - mx)
        inv = 1.0 / (e0 + e1 + e2)
        lane = lax.broadcasted_iota(jnp.int32, l0.shape, 1)
        head = lane < N_HEADS
        a = (jnp.where(head, e0 * inv, 0.0)
             + pltpu.roll(jnp.where(head, e1 * inv, 0.0), N_HEADS, 1)
             + pltpu.roll(jnp.where(head, e2 * inv, 0.0), 2 * N_HEADS, 1))
        hi = a.astype(_BF16).astype(_F32)
        a2 = hi + pltpu.roll(a - hi, N_ATTN_GROUPS * N_HEADS, 1)
        w = _dot(a2.astype(_BF16), e_ref[...])
        for j in range(tiles):
            sl = slice(j * LANES, (j + 1) * LANES)
            o = (w[:, sl] * o0_ref[0, :, sl].astype(_F32)
                 + w[:, D_MODEL + j * LANES:D_MODEL + (j + 1) * LANES] * o1_scr[j]
                 + w[:, 2 * D_MODEL + j * LANES:2 * D_MODEL + (j + 1) * LANES] * o2_scr[j])
            o_scr[:, sl] = o.astype(_BF16)
        out_ref[...] = h_ref[...] + _dot(o_scr[...], wo_ref[...])

    @pl.when(i >= n_merge)
    def _():
        out_ref[...] = hs_ref[...] + _dot(os_ref[...].astype(_BF16), wo_ref[...])

    _route(out_ref[...], gr_ref[...], wrh_ref[...], wrl_ref[...], idx_ref, g1_ref, g2_ref)


def _merge_proj(outs, lses, h, o_sample, h_sample, w_o, route_gain, w_router, tm):
    m, ms = h.shape[0], h_sample.shape[0]
    n, _, seq, _ = outs[0].shape
    n_merge = m // tm
    per_n = seq // tm
    row = lambda i: (jnp.minimum(i, n_merge - 1), 0)
    srow = lambda i: (jnp.maximum(i - n_merge, 0), 0)
    const = lambda i: (0, 0)

    def by_class(a):
        d, width = a.shape[1], a.shape[3]

        def index(i):
            ic = jnp.minimum(i, n_merge - 1)
            return (ic // per_n, 0, ic % per_n, 0)

        return pl.BlockSpec((None, d, tm // d, width), index)

    tiles = D_MODEL // LANES
    wr = jnp.zeros((D_MODEL, LANES), _F32).at[:, :N_EXPERTS].set(w_router)
    wr_hi = wr.astype(_BF16)
    wr_lo = (wr - wr_hi.astype(_F32)).astype(_BF16)
    tile_out = lambda width: pl.BlockSpec((tm, width), lambda i: (i, 0))
    tile_shape = lambda width: jax.ShapeDtypeStruct((m + ms, width), _F32)
    return pl.pallas_call(
        functools.partial(_merge_proj_kernel, n_merge=n_merge, tm=tm),
        grid=((m + ms) // tm,),
        in_specs=[by_class(a) for a in outs] + [by_class(a) for a in lses] + [
            pl.BlockSpec((LANES, N_ATTN_GROUPS * D_MODEL), const),
            pl.BlockSpec((tm, D_MODEL), row),
            pl.BlockSpec((tm, D_MODEL), srow),
            pl.BlockSpec((tm, D_MODEL), srow),
            pl.BlockSpec((D_MODEL, D_MODEL), const),
            pl.BlockSpec((1, D_MODEL), const),
            pl.BlockSpec((D_MODEL, LANES), const),
            pl.BlockSpec((D_MODEL, LANES), const),
        ],
        out_specs=[tile_out(D_MODEL), tile_out(LANES), tile_out(LANES), tile_out(LANES)],
        out_shape=[tile_shape(D_MODEL), tile_shape(LANES), tile_shape(LANES), tile_shape(LANES)],
        scratch_shapes=[
            pltpu.VMEM((tiles, tm, LANES), _F32),
            pltpu.VMEM((tiles, tm, LANES), _F32),
            pltpu.VMEM((tm, LANES), _F32),
            pltpu.VMEM((tm, LANES), _F32),
            pltpu.VMEM((tm, D_MODEL), _BF16),
        ],
        compiler_params=_params(("parallel",)),
        name="merge_out_proj",
    )(*outs, *lses, _head_expand_matrix(), h, o_sample, h_sample, w_o, route_gain.reshape(1, D_MODEL), wr_hi, wr_lo)


def _cached_bias(t, n_ctx):
    rows = N_ATTN_GROUPS * N_HEADS * t
    cache = np.full((rows, n_ctx), MASK_VALUE, np.float32)
    new = np.full((rows, LANES), MASK_VALUE, np.float32)
    for g, (win, dil) in enumerate(ATTN_GROUPS):
        for j in range(t):
            idx = n_ctx + j - np.arange(win // dil + 1) * dil
            idx = idx[idx >= 0]
            for h in range(N_HEADS):
                r = (g * N_HEADS + h) * t + j
                cache[r, idx[idx < n_ctx]] = 0.0
                new[r, idx[idx >= n_ctx] - n_ctx] = 0.0
    return cache, new


def _cached_attn_kernel(q0_ref, q1_ref, q2_ref, kc_ref, vc_ref, kx_ref, vx_ref, kn_ref, vn_ref, knt_ref, vnt_ref,
                        bias_ref, biasn_ref, o_ref, kw_ref, vw_ref, qbd_ref, m_ref, l_ref, acc_ref,
                        *, t, first_chunk):
    c = pl.program_id(1)
    last = pl.num_programs(1) - 1
    gh = N_HEADS * t
    row = lax.broadcasted_iota(jnp.int32, (gh, D_MODEL), 0)
    lane = lax.broadcasted_iota(jnp.int32, (gh, D_MODEL), 1)
    own = jnp.right_shift(row, int(math.log2(t))) == jnp.right_shift(lane, int(math.log2(HEAD_DIM)))

    @pl.when(c == 0)
    def _():
        for g, q_ref in enumerate((q0_ref, q1_ref, q2_ref)):
            tiled = jnp.concatenate([q_ref[...].astype(_F32)] * N_HEADS, axis=0)
            qbd_ref[g * gh:(g + 1) * gh, :] = jnp.where(own, tiled, 0.0).astype(_BF16)
        m_ref[...] = jnp.full(m_ref.shape, M_INIT, _F32)
        l_ref[...] = jnp.zeros_like(l_ref)
        acc_ref[...] = jnp.zeros_like(acc_ref)

    def update(rows, s, weighted_values):
        m_old = m_ref[rows, :]
        m_new = jnp.maximum(m_old, jnp.max(s, axis=1, keepdims=True))
        alpha = jnp.exp(m_old - m_new)
        pr = jnp.exp(s - m_new)
        l_ref[rows, :] = alpha * l_ref[rows, :] + jnp.sum(pr, axis=1, keepdims=True)
        acc_ref[rows, :] = alpha * acc_ref[rows, :] + weighted_values(pr.astype(_BF16))
        m_ref[rows, :] = m_new

    kc, vc = kc_ref[...], vc_ref[...]
    kct, vct = kc.astype(_BF16), vc.astype(_BF16)
    for g in range(N_ATTN_GROUPS):
        rows = slice(g * gh, (g + 1) * gh)

        @pl.when(c >= first_chunk[g])
        def _():
            update(rows, _dot(qbd_ref[rows, :], kct) + bias_ref[rows, :],
                   lambda pr: lax.dot_general(pr, vct, _NT, preferred_element_type=_F32))

    chunk = kc.shape[1]
    for cur, nxt, new_t, w_ref in ((kc, kx_ref, knt_ref, kw_ref), (vc, vx_ref, vnt_ref, vw_ref)):
        tail = jnp.where(c == last, new_t[...], nxt[...])
        ext = jnp.concatenate([cur, tail], axis=1)
        w_ref[...] = pltpu.roll(ext, chunk + LANES - t, 1)[:, 0:chunk]

    @pl.when(c == last)
    def _():
        pad = jnp.zeros((LANES - t, D_MODEL), _F32)
        kn = jnp.concatenate([kn_ref[...].astype(_F32), pad], axis=0).astype(_BF16)
        vn = jnp.concatenate([vn_ref[...].astype(_F32), pad], axis=0).astype(_BF16)
        update(slice(0, N_ATTN_GROUPS * gh),
               lax.dot_general(qbd_ref[...], kn, _NT, preferred_element_type=_F32) + biasn_ref[...],
               lambda pr: _dot(pr, vn))
        ms = [m_ref[g * gh:(g + 1) * gh, :] for g in range(N_ATTN_GROUPS)]
        mx = jnp.maximum(jnp.maximum(ms[0], ms[1]), ms[2])
        num = jnp.zeros((gh, D_MODEL), _F32)
        den = jnp.zeros((gh, 1), _F32)
        for g in range(N_ATTN_GROUPS):
            w = jnp.exp(ms[g] - mx)
            num = num + w * acc_ref[g * gh:(g + 1) * gh, :]
            den = den + w * l_ref[g * gh:(g + 1) * gh, :]
        on = jnp.where(own, num * (1.0 / den), 0.0)
        o = on[0:t, :]
        for h in range(1, N_HEADS):
            o = o + on[h * t:(h + 1) * t, :]
        o_ref[...] = o


def _cached_attention(qs, k_new, v_new, k_new_f32, v_new_f32, cache_kt, cache_vt, chunk):
    n, t, _ = k_new.shape
    n_ctx = cache_kt.shape[2]
    assert n_ctx == MAX_WINDOW and n_ctx % chunk == 0 and chunk % LANES == 0 and t <= LANES
    bias_np, bias_new = _cached_bias(t, n_ctx)
    gh = N_HEADS * t
    rows = N_ATTN_GROUPS * gh
    n_chunks = n_ctx // chunk
    first_chunk = tuple(
        min(c for c in range(n_chunks) if (bias_np[g * gh:(g + 1) * gh, c * chunk:(c + 1) * chunk] == 0.0).any())
        for g in range(N_ATTN_GROUPS))
    new_t = lambda a: jnp.pad(a.transpose(0, 2, 1), ((0, 0), (0, 0), (0, LANES - t)))
    per_n = lambda i, c: (i, 0, 0)
    cur = pl.BlockSpec((None, D_MODEL, chunk), lambda i, c: (i, 0, c))
    ahead = pl.BlockSpec((None, D_MODEL, LANES),
                         lambda i, c: (i, 0, jnp.minimum((c + 1) * (chunk // LANES), n_ctx // LANES - 1)))
    small = pl.BlockSpec((None, t, D_MODEL), per_n)
    small_t = pl.BlockSpec((None, D_MODEL, LANES), per_n)
    window = jax.ShapeDtypeStruct((n, D_MODEL, n_ctx), _F32)
    return pl.pallas_call(
        functools.partial(_cached_attn_kernel, t=t, first_chunk=first_chunk),
        grid=(n, n_chunks),
        in_specs=[
            small, small, small,
            cur, cur, ahead, ahead,
            small, small, small_t, small_t,
            pl.BlockSpec((rows, chunk), lambda i, c: (0, c)),
            pl.BlockSpec((rows, LANES), lambda i, c: (0, 0)),
        ],
        out_specs=[small, cur, cur],
        out_shape=[jax.ShapeDtypeStruct((n, t, D_MODEL), _F32), window, window],
        scratch_shapes=[
            pltpu.VMEM((rows, D_MODEL), _BF16),
            pltpu.VMEM((rows, 1), _F32),
            pltpu.VMEM((rows, 1), _F32),
            pltpu.VMEM((rows, D_MODEL), _F32),
        ],
        compiler_params=_params(("parallel", "arbitrary")),
        name="cached_attn",
    )(*qs, cache_kt, cache_vt, cache_kt, cache_vt, k_new, v_new, new_t(k_new_f32), new_t(v_new_f32),
      jnp.asarray(bias_np), jnp.asarray(bias_new))


def _split_bf16(x):
    hi = x.astype(_BF16)
    return hi, (x - hi.astype(_F32)).astype(_BF16)


def _route(h, gain, wr_hi, wr_lo, idx_ref, g1_ref, g2_ref):
    u_hi, u_lo = _split_bf16(_rms_unit(h) * gain)
    logits = _dot(u_hi, wr_hi) + (_dot(u_hi, wr_lo) + _dot(u_lo, wr_hi))
    lane = lax.broadcasted_iota(jnp.int32, logits.shape, 1).astype(_F32)
    neg = -jnp.inf
    lg = jnp.where(lane < N_EXPERTS, logits, neg)
    v1 = jnp.max(lg, axis=1, keepdims=True)
    i1 = jnp.min(jnp.where(lg == v1, lane, float(LANES)), axis=1, keepdims=True)
    lg2 = jnp.where(lane == i1, neg, lg)
    v2 = jnp.max(lg2, axis=1, keepdims=True)
    i2 = jnp.min(jnp.where(lg2 == v2, lane, float(LANES)), axis=1, keepdims=True)
    e2 = jnp.exp(v2 - v1)
    den = 1.0 + e2
    idx_ref[...] = jnp.where(lane < LANES // 2, i1, i2)
    g1_ref[...] = jnp.broadcast_to(1.0 / den, g1_ref.shape)
    g2_ref[...] = jnp.broadcast_to(e2 / den, g2_ref.shape)


def _invert_kernel(pos_ref, src_ref):
    def clear(p, carry):
        src_ref[p] = 0
        return carry

    lax.fori_loop(0, src_ref.shape[0], clear, 0, unroll=16)

    group = 16
    assert pos_ref.shape[0] % group == 0 and group % TOP_K == 0

    def place(b, carry):
        a0 = b * group
        token0 = b * (group // TOP_K)
        targets = [pos_ref[a0 + u] for u in range(group)]
        for u in range(group):
            src_ref[targets[u]] = token0 + u // TOP_K
        return carry

    lax.fori_loop(0, pos_ref.shape[0] // group, place, 0)


def _invert_positions(pos, n_rows):
    smem = pl.BlockSpec(memory_space=pltpu.SMEM)
    return pl.pallas_call(
        _invert_kernel,
        in_specs=[smem],
        out_specs=smem,
        out_shape=jax.ShapeDtypeStruct((n_rows,), jnp.int32),
        name="invert_positions",
    )(pos)


def _routing_tables(idx_tile, tm, n_tiles):
    m = idx_tile.shape[0]
    e = jnp.stack([idx_tile[:, 0], idx_tile[:, LANES // 2]], axis=1).reshape(TOP_K * m).astype(jnp.int32)
    onehot = (e[:, None] == jnp.arange(N_EXPERTS, dtype=jnp.int32)[None, :]).astype(jnp.int32)
    running = jnp.cumsum(onehot, axis=0)
    counts = running[-1]
    rank = jnp.sum(onehot * running, axis=1) - 1
    padded = (counts + tm - 1) // tm * tm
    ends = jnp.cumsum(padded)
    starts = ends - padded
    pos = jnp.sum(onehot * starts[None, :], axis=1) + rank
    src = _invert_positions(pos, n_tiles * tm)
    n_act = ends[-1] // tm
    tile_id = jnp.arange(n_tiles, dtype=jnp.int32)
    tile_expert = jnp.sum((tile_id[:, None] * tm >= ends[None, :]).astype(jnp.int32), axis=1)
    tile_expert = jnp.where(tile_id < n_act, tile_expert, jnp.take(tile_expert, n_act - 1))
    return src.reshape(n_tiles, 1, tm), tile_expert, n_act.reshape(1), pos.reshape(m, TOP_K)


def _expert_kernel(te_ref, na_ref, rows_ref, rows_next_ref, h_hbm, g_ref, wg_ref, wu_ref, wd_ref, out_ref,
                   xbuf, sem, u_sc, acc_sc, *, tm, n_chunks):
    del te_ref
    i = pl.program_id(0)
    c = pl.program_id(1)
    n_act = na_ref[0]
    slot = lax.rem(i, 2)
    active = i < n_act

    def start_row_copy(rows, r, dst_slot):
        pltpu.make_async_copy(h_hbm.at[pl.ds(rows[0, 0, r], 1)], xbuf.at[dst_slot, pl.ds(r, 1)],
                              sem.at[dst_slot]).start(priority=GATHER_DMA_QUEUE)

    @pl.when((c == 0) & (i == 0))
    def _():
        def body(r, carry):
            start_row_copy(rows_ref, r, 0)
            return carry
        lax.fori_loop(0, tm, body, 0, unroll=8)

    @pl.when((c == 0) & (i <= n_act))
    def _():
        pltpu.make_async_copy(h_hbm.at[pl.ds(0, tm)], xbuf.at[slot], sem.at[slot]).wait()

    tf = wd_ref.shape[0]
    n_sub = tf // EXPERT_SUB_COLS
    per_sub = -(-tm // n_sub)
    for step in range(n_chunks):
        @pl.when(active & (c == step))
        def _():
            if step == 0:
                u_sc[...] = (_rms_unit(xbuf[slot]) * g_ref[...]).astype(_BF16)
            u = u_sc[...]
            acc = None
            for s in range(n_sub):
                cols = slice(s * EXPERT_SUB_COLS, (s + 1) * EXPERT_SUB_COLS)
                y = _swiglu_chunk(u, wg_ref[:, cols], wu_ref[:, cols], wd_ref[cols, :])
                acc = y if acc is None else acc + y
                if step == 0:
                    for r in range(s * per_sub, min((s + 1) * per_sub, tm)):
                        start_row_copy(rows_next_ref, r, 1 - slot)
            acc_sc[...] = acc if step == 0 else acc_sc[...] + acc

    @pl.when(c == n_chunks - 1)
    def _():
        out_ref[...] = jnp.where(active, acc_sc[...], 0.0)


def _expert_layer(h, src_rows, tile_expert, n_act, gain, w_gu, w_down, tm, tf):
    n_tiles = src_rows.shape[0]
    f = w_down.shape[1]
    nc = f // tf
    smem_rows = lambda index_map: pl.BlockSpec((1, 1, tm), index_map, memory_space=pltpu.SMEM)
    grid_spec = pltpu.PrefetchScalarGridSpec(
        num_scalar_prefetch=2,
        grid=(n_tiles, nc),
        in_specs=[
            smem_rows(lambda i, c, te, na: (i, 0, 0)),
            smem_rows(lambda i, c, te, na: (jnp.minimum(i + 1, n_tiles - 1), 0, 0)),
            pl.BlockSpec(memory_space=pl.ANY),
            pl.BlockSpec((1, D_MODEL), lambda i, c, te, na: (0, 0)),
            pl.BlockSpec((None, D_MODEL, tf), lambda i, c, te, na: (te[i], 0, c)),
            pl.BlockSpec((None, D_MODEL, tf), lambda i, c, te, na: (te[i], 0, nc + c)),
            pl.BlockSpec((None, tf, D_MODEL), lambda i, c, te, na: (te[i], c, 0)),
        ],
        out_specs=pl.BlockSpec((tm, D_MODEL), lambda i, c, te, na: (i, 0)),
        scratch_shapes=[
            pltpu.VMEM((2, tm, D_MODEL), _F32),
            pltpu.SemaphoreType.DMA((2,)),
            pltpu.VMEM((tm, D_MODEL), _BF16),
            pltpu.VMEM((tm, D_MODEL), _F32),
        ],
    )
    return pl.pallas_call(
        functools.partial(_expert_kernel, tm=tm, n_chunks=nc),
        grid_spec=grid_spec,
        out_shape=jax.ShapeDtypeStruct((n_tiles * tm, D_MODEL), _F32),
        compiler_params=_params(("arbitrary", "arbitrary")),
        name="swiglu_routed",
    )(tile_expert, n_act, src_rows, src_rows, h, gain.reshape(1, D_MODEL), w_gu, w_gu, w_down)


def _combine_kernel(pos_ref, pos_next_ref, ys_hbm, h_ref, g1_ref, g2_ref, gf_ref, out_ref, buf, sem, *, tq):
    j = pl.program_id(0)
    slot = lax.rem(j, 2)

    def start_gather(pos, dst_slot):
        for r in range(tq):
            for k in range(TOP_K):
                pltpu.make_async_copy(ys_hbm.at[pl.ds(pos[0, k, r], 1)], buf.at[dst_slot, k, pl.ds(r, 1)],
                                      sem.at[dst_slot]).start(priority=k)

    @pl.when(j == 0)
    def _():
        start_gather(pos_ref, 0)

    for s in range(2):
        @pl.when((j + 1 < pl.num_programs(0)) & (slot == s))
        def _():
            start_gather(pos_next_ref, 1 - s)

    for k in range(TOP_K):
        pltpu.make_async_copy(ys_hbm.at[pl.ds(0, tq)], buf.at[slot, k], sem.at[slot]).wait()
    rep = D_MODEL // LANES
    y = jnp.tile(g1_ref[...], (1, rep)) * buf[slot, 0] + jnp.tile(g2_ref[...], (1, rep)) * buf[slot, 1]
    out_ref[...] = _rms_unit(h_ref[...] + y) * gf_ref[...]


def _moe_combine(ys, h, g1, g2, pos, final_gain, tq, tile_off, n_tiles):
    pos3 = pos[tile_off * tq:(tile_off + n_tiles) * tq].reshape(n_tiles, tq, TOP_K).transpose(0, 2, 1)
    smem_pos = lambda index_map: pl.BlockSpec((1, TOP_K, tq), index_map, memory_space=pltpu.SMEM)
    tok = lambda width: pl.BlockSpec((tq, width), lambda j: (j + tile_off, 0))
    return pl.pallas_call(
        functools.partial(_combine_kernel, tq=tq),
        grid=(n_tiles,),
        in_specs=[
            smem_pos(lambda j: (j, 0, 0)),
            smem_pos(lambda j: (jnp.minimum(j + 1, n_tiles - 1), 0, 0)),
            pl.BlockSpec(memory_space=pl.ANY),
            tok(D_MODEL), tok(LANES), tok(LANES),
            pl.BlockSpec((1, D_MODEL), lambda j: (0, 0)),
        ],
        out_specs=pl.BlockSpec((tq, D_MODEL), lambda j: (j, 0)),
        out_shape=jax.ShapeDtypeStruct((n_tiles * tq, D_MODEL), _F32),
        scratch_shapes=[pltpu.VMEM((2, TOP_K, tq, D_MODEL), _F32), pltpu.SemaphoreType.DMA((2,))],
        compiler_params=_params(("arbitrary",)),
        name="moe_combine",
    )(pos3, pos3, ys, h, g1, g2, final_gain.reshape(1, D_MODEL))


def kernel(x_prompt, x_sample, state_pool, cache_k_win, cache_v_win, norm_mix, norm_ffn, pool_w, pool_scale,
           norm_kv, w_kv, w_q, w_o, w_ffn_gu, w_ffn_down, w_router, w_exp_gu, w_exp_down, norm_final):
    nb, seq, _ = x_prompt.shape
    ns, ts, _ = x_sample.shape
    n_ctx = cache_k_win.shape[1]
    mp, ms = nb * seq, ns * ts
    w_kv_b = w_kv.astype(_BF16)
    w_q_b = w_q[0].astype(_BF16)
    w_o_b = w_o[0].astype(_BF16)
    w_ffn_gu_b = w_ffn_gu[0].astype(_BF16)
    w_ffn_down_b = w_ffn_down[0].astype(_BF16)
    w_exp_gu_b = w_exp_gu[0].astype(_BF16)
    w_exp_down_b = w_exp_down[0].astype(_BF16)
    tf_ffn = w_ffn_down_b.shape[0] // 2
    tf_exp = w_exp_down_b.shape[1] // 2

    dils = tuple(d for _, d in ATTN_GROUPS)
    h, pool_prompt = _pool_layer(x_prompt, None, norm_mix[0], pool_w[0], pool_scale[0], tm=TM_POOL, bn=1)
    h = _ffn_layer(h.reshape(mp, D_MODEL), norm_ffn[0], w_ffn_gu_b, w_ffn_down_b, TM_FFN, tf_ffn)
    keep = min(MAX_WINDOW, seq)
    q, k, v, k_win_p, v_win_p = _qkv_layer(h.reshape(nb, seq, D_MODEL), jnp.arange(seq, dtype=jnp.int32),
                                           norm_kv, norm_mix[1], w_kv_b, w_q_b, tm=TM_QKV, keep=keep, q_dils=dils)
    outs, lses = [], []
    for g, d in enumerate(dils):
        o, lse = _band_attention(q[g], k[d], v[d], g, tb=ATTN_BANDS)
        outs.append(o)
        lses.append(lse)

    hs, pool_sample = _pool_layer(x_sample, state_pool[0], norm_mix[0], pool_w[0], pool_scale[0], tm=ts, bn=8)
    hs = _ffn_layer(hs.reshape(ms, D_MODEL), norm_ffn[0], w_ffn_gu_b, w_ffn_down_b, ms, tf_ffn)
    pos_s = jnp.tile(PAST_LEN + jnp.arange(ts, dtype=jnp.int32), ns)
    qs, ks, vs, ks_f, vs_f = _qkv_layer(hs.reshape(1, ms, D_MODEL), pos_s, norm_kv, norm_mix[1], w_kv_b, w_q_b,
                                        tm=ms, keep=ms, q_dils=(1,) * N_ATTN_GROUPS)
    ckt = cache_k_win.transpose(0, 2, 3, 1).reshape(ns, D_MODEL, n_ctx)
    cvt = cache_v_win.transpose(0, 2, 3, 1).reshape(ns, D_MODEL, n_ctx)
    per_sample = lambda a: a.reshape(ns, ts, D_MODEL)
    os_, kwt, vwt = _cached_attention([per_sample(a) for a in qs], per_sample(ks[1]), per_sample(vs[1]),
                                      per_sample(ks_f), per_sample(vs_f), ckt, cvt, chunk=CACHE_CHUNK)
    h_all, idx_tile, g1, g2 = _merge_proj(outs, lses, h, os_.reshape(ms, D_MODEL), hs, w_o_b, norm_ffn[1],
                                          w_router[0], tm=TM_TOKEN)

    m_all = mp + ms
    n_tiles = -(-TOP_K * m_all // TM_EXPERT) + N_EXPERTS + 1
    src_rows, tile_expert, n_act, pos = _routing_tables(idx_tile, TM_EXPERT, n_tiles)
    ys = _expert_layer(h_all, src_rows, tile_expert, n_act, norm_ffn[1], w_exp_gu_b, w_exp_down_b,
                       TM_EXPERT, tf_exp)
    y_prompt = _moe_combine(ys, h_all, g1, g2, pos, norm_final, TM_TOKEN, 0, mp // TM_TOKEN)
    y_sample = _moe_combine(ys, h_all, g1, g2, pos, norm_final, TM_TOKEN, mp // TM_TOKEN, ms // TM_TOKEN)

    heads = lambda a: a.reshape(a.shape[0], a.shape[1], N_HEADS, HEAD_DIM)
    heads_t = lambda a: a.reshape(ns, N_HEADS, HEAD_DIM, a.shape[2]).transpose(0, 3, 1, 2)
    return (y_prompt.reshape(nb, seq, D_MODEL), y_sample.reshape(ns, ts, D_MODEL), pool_prompt, pool_sample,
            heads(k_win_p), heads(v_win_p), heads_t(kwt), heads_t(vwt))
```

```python
import functools
import math

import numpy as np
import jax
import jax.numpy as jnp
from jax import lax
from jax.experimental import pallas as pl
from jax.experimental.pallas import tpu as pltpu

D_MODEL = 1024
PAST_LEN = 16384
POOL_WINDOWS = (2, 4, 8, 16)
POOL_GROUP_DIM = D_MODEL // len(POOL_WINDOWS)
POOL_CTX = max(POOL_WINDOWS) - 1
HEAD_DIM = 64
N_HEADS = D_MODEL // HEAD_DIM
ATTN_GROUPS = ((128, 1), (512, 4), (2048, 16))
N_ATTN_GROUPS = len(ATTN_GROUPS)
BAND = 128
MAX_WINDOW = max(w for w, _ in ATTN_GROUPS)
ROT_DIM = HEAD_DIM // 4
ROPE_THETA = 500000.0
ATTN_SCALE = HEAD_DIM ** -0.5
N_EXPERTS = 8
TOP_K = 2
RMS_EPS = 1e-5

LANES = 128
HALO = 16
MASK_VALUE = -1e30
M_INIT = -1e20
VMEM_LIMIT = 56 * 1024 * 1024

TM_POOL = 512
TM_FFN = 512
TM_QKV = 256
TM_TOKEN = 256
TM_EXPERT = 512
MXU_COLS = 256
EXPERT_SUB_COLS = MXU_COLS
GATHER_DMA_QUEUE = 1
ATTN_BANDS = 2
CACHE_CHUNK = 512

_F32 = jnp.float32
_BF16 = jnp.bfloat16
_NT = (((1,), (1,)), ((), ()))


def _params(semantics):
    return pltpu.CompilerParams(dimension_semantics=semantics, vmem_limit_bytes=VMEM_LIMIT)


def _rms_unit(x):
    return x * lax.rsqrt(jnp.mean(x * x, axis=-1, keepdims=True) + RMS_EPS)


def _dot(a, b):
    return jnp.dot(a, b, preferred_element_type=_F32)


def _swiglu_chunk(u, wg, wu, wd):
    g = _dot(u, wg)
    up = _dot(u, wu)
    return _dot((g * jax.nn.sigmoid(g) * up).astype(_BF16), wd)


def _pool_kernel(*refs, tm, n_ctx, bn):
    if n_ctx:
        x_ref, ctx_ref, g_ref, w_ref, sc_ref, h_ref, pool_ref, ext_ref = refs
    else:
        x_ref, g_ref, w_ref, sc_ref, h_ref, pool_ref, ext_ref = refs
        ctx_ref = None
    s = pl.program_id(1)
    pos = n_ctx + s * tm + lax.broadcasted_iota(jnp.int32, (tm, 1), 0)
    for b in range(bn):
        x = x_ref[b]
        u = _rms_unit(x) * g_ref[...]

        @pl.when(s == 0)
        def _():
            if n_ctx:
                ext_ref[b, 0:HALO - n_ctx, :] = jnp.zeros((HALO - n_ctx, D_MODEL), _F32)
                ext_ref[b, HALO - n_ctx:HALO, :] = ctx_ref[b]
            else:
                ext_ref[b, 0:HALO, :] = jnp.zeros((HALO, D_MODEL), _F32)

        ext_ref[b, HALO:HALO + tm, :] = u
        for g, w in enumerate(POOL_WINDOWS):
            sl = slice(g * POOL_GROUP_DIM, (g + 1) * POOL_GROUP_DIM)
            acc = u[:, sl]
            for k in range(1, w):
                acc = acc + ext_ref[b, HALO - k:HALO - k + tm, sl]
            cnt = jnp.minimum(pos + 1, w).astype(_F32)
            d = acc / cnt - u[:, sl]
            y = _dot(d.astype(_BF16), w_ref[g])
            h_ref[b, :, sl] = x[:, sl] + y * sc_ref[:, sl]

        @pl.when(s == pl.num_programs(1) - 1)
        def _():
            pool_ref[0, b] = ext_ref[b, HALO + tm - POOL_CTX:HALO + tm, :]

        ext_ref[b, 0:HALO, :] = ext_ref[b, tm:tm + HALO, :]


def _pool_layer(x, ctx, gain, w_pool, scale, tm, bn):
    n, t, _ = x.shape
    n_ctx = 0 if ctx is None else POOL_CTX
    grid = (n // bn, t // tm)
    row = lambda i, s: (i, s, 0)
    const2 = lambda i, s: (0, 0)
    in_specs = [pl.BlockSpec((bn, tm, D_MODEL), row)]
    args = [x]
    if ctx is not None:
        in_specs.append(pl.BlockSpec((bn, POOL_CTX, D_MODEL), lambda i, s: (i, 0, 0)))
        args.append(ctx)
    in_specs += [
        pl.BlockSpec((1, D_MODEL), const2),
        pl.BlockSpec((len(POOL_WINDOWS), POOL_GROUP_DIM, POOL_GROUP_DIM), lambda i, s: (0, 0, 0)),
        pl.BlockSpec((1, D_MODEL), const2),
    ]
    args += [gain.reshape(1, D_MODEL), w_pool.astype(_BF16), scale.reshape(1, D_MODEL)]
    return pl.pallas_call(
        functools.partial(_pool_kernel, tm=tm, n_ctx=n_ctx, bn=bn),
        grid=grid,
        in_specs=in_specs,
        out_specs=[
            pl.BlockSpec((bn, tm, D_MODEL), row),
            pl.BlockSpec((1, bn, POOL_CTX, D_MODEL), lambda i, s: (0, i, 0, 0)),
        ],
        out_shape=[
            jax.ShapeDtypeStruct((n, t, D_MODEL), _F32),
            jax.ShapeDtypeStruct((1, n, POOL_CTX, D_MODEL), _F32),
        ],
        scratch_shapes=[pltpu.VMEM((bn, HALO + tm, D_MODEL), _F32)],
        compiler_params=_params(("parallel", "arbitrary")),
        name="pool_layer",
    )(*args)


def _ffn_kernel(h_ref, g_ref, wg_ref, wu_ref, wd_ref, out_ref, u_sc, acc_sc):
    c = pl.program_id(1)

    @pl.when(c == 0)
    def _():
        u_sc[...] = (_rms_unit(h_ref[...]) * g_ref[...]).astype(_BF16)
        acc_sc[...] = jnp.zeros_like(acc_sc)

    acc_sc[...] += _swiglu_chunk(u_sc[...], wg_ref[...], wu_ref[...], wd_ref[...])

    @pl.when(c == pl.num_programs(1) - 1)
    def _():
        out_ref[...] = h_ref[...] + acc_sc[...]


def _ffn_layer(h, gain, w_gu, w_down, tm, tf):
    m = h.shape[0]
    f = w_down.shape[0]
    nc = f // tf
    row = lambda i, c: (i, 0)
    return pl.pallas_call(
        _ffn_kernel,
        grid=(m // tm, nc),
        in_specs=[
            pl.BlockSpec((tm, D_MODEL), row),
            pl.BlockSpec((1, D_MODEL), lambda i, c: (0, 0)),
            pl.BlockSpec((D_MODEL, tf), lambda i, c: (0, c)),
            pl.BlockSpec((D_MODEL, tf), lambda i, c: (0, nc + c)),
            pl.BlockSpec((tf, D_MODEL), lambda i, c: (c, 0)),
        ],
        out_specs=pl.BlockSpec((tm, D_MODEL), row),
        out_shape=jax.ShapeDtypeStruct((m, D_MODEL), _F32),
        scratch_shapes=[pltpu.VMEM((tm, D_MODEL), _BF16), pltpu.VMEM((tm, D_MODEL), _F32)],
        compiler_params=_params(("parallel", "arbitrary")),
        name="swiglu_dense",
    )(h, gain.reshape(1, D_MODEL), w_gu, w_gu, w_down)


def _rope_tables(pos):
    half = ROT_DIM // 2
    inv_freq = jnp.exp(-math.log(ROPE_THETA) * jnp.arange(half, dtype=_F32) / half)
    ang = pos.astype(_F32)[:, None] * inv_freq[None, :]
    cos, sin = jnp.cos(ang), jnp.sin(ang)
    t = pos.shape[0]
    rest = HEAD_DIM - ROT_DIM
    zero_h = jnp.zeros((t, half), _F32)
    zero_r = jnp.zeros((t, rest), _F32)
    c = jnp.concatenate([cos, cos, jnp.ones((t, rest), _F32)], axis=1)
    s1 = jnp.concatenate([zero_h, sin, zero_r], axis=1)
    s2 = jnp.concatenate([-sin, zero_h, zero_r], axis=1)
    rep = LANES // HEAD_DIM
    return tuple(jnp.tile(a, (1, rep)) for a in (c, s1, s2))


def _store_by_class(t, j, scr, slot, targets, tm):
    sl = slice(j * LANES, (j + 1) * LANES)
    if any(d > 1 for d, _ in targets):
        scr[slot] = t
    for d, ref in targets:
        if d == 1:
            ref[0, :, sl] = t.astype(_BF16)
        else:
            for r in range(d):
                ref[r, :, sl] = scr[slot, pl.ds(r, tm // d, stride=d), :].astype(_BF16)


def _qkv_kernel(*refs, q_dils, kv_dils, tm, n_cast):
    h_ref, gkv_ref, gq_ref, wkv_ref, wq_ref, c_ref, s1_ref, s2_ref = refs[:8]
    rest = list(refs[8:])
    cast_in = [rest.pop(0) for _ in range(n_cast)]
    q_refs = [rest.pop(0) for _ in q_dils]
    k_refs = [rest.pop(0) for _ in kv_dils]
    v_refs = [rest.pop(0) for _ in kv_dils]
    kf_ref, vf_ref = rest.pop(0), rest.pop(0)
    cast_out = [rest.pop(0) for _ in range(n_cast)]
    q_scr, k_scr, v_scr = rest
    hn = _rms_unit(h_ref[...])
    ukv = (hn * gkv_ref[...]).astype(_BF16)
    uq = (hn * gq_ref[...]).astype(_BF16)
    c, s1, s2 = c_ref[...], s1_ref[...], s2_ref[...]
    half = ROT_DIM // 2
    tiles = D_MODEL // LANES
    cols = MXU_COLS // LANES

    def rope(t):
        return t * c + pltpu.roll(t, half, 1) * s1 + pltpu.roll(t, LANES - half, 1) * s2

    for jb in range(2 * tiles // cols):
        blk = _dot(ukv, wkv_ref[:, jb * MXU_COLS:(jb + 1) * MXU_COLS])
        for jj in range(cols):
            j = jb * cols + jj
            t = blk[:, jj * LANES:(jj + 1) * LANES]
            if j < tiles:
                t = rope(t)
                kf_ref[:, j * LANES:(j + 1) * LANES] = t
                _store_by_class(t, j, k_scr, j, list(zip(kv_dils, k_refs)), tm)
            else:
                j -= tiles
                vf_ref[:, j * LANES:(j + 1) * LANES] = t
                _store_by_class(t, j, v_scr, j, list(zip(kv_dils, v_refs)), tm)
    for g, d in enumerate(q_dils):
        for jb in range(tiles // cols):
            c0 = g * D_MODEL + jb * MXU_COLS
            blk = _dot(uq, wq_ref[:, c0:c0 + MXU_COLS])
            for jj in range(cols):
                j = jb * cols + jj
                t = rope(blk[:, jj * LANES:(jj + 1) * LANES]) * ATTN_SCALE
                _store_by_class(t, j, q_scr, g * tiles + j, [(d, q_refs[g])], tm)
    for src, dst in zip(cast_in, cast_out):
        dst[...] = src[...].astype(_BF16)


def _qkv_layer(h, pos, norm_kv, norm_q, w_kv, w_q, tm, keep, q_dils, cast=()):
    n, t, _ = h.shape
    off = (t - keep) // tm
    kv_dils = tuple(sorted(set(q_dils)))
    tables = _rope_tables(pos)
    const2 = lambda i, s: (0, 0)
    tab = pl.BlockSpec((tm, LANES), lambda i, s: (s, 0))
    win = lambda i, s: (i, jnp.maximum(s - off, 0), 0)
    by_class = lambda d: pl.BlockSpec((None, d, tm // d, D_MODEL), lambda i, s: (i, 0, s, 0))
    class_shape = lambda d: jax.ShapeDtypeStruct((n, d, t // d, D_MODEL), _BF16)
    out_dils = tuple(q_dils) + kv_dils + kv_dils
    tiles = D_MODEL // LANES
    per_n = t // tm
    steps = n * per_n

    def slab(a):
        e, r, c = a.shape
        per_e = steps // e
        assert steps % e == 0 and r % per_e == 0 and (r // per_e) % 16 == 0
        return pl.BlockSpec((None, r // per_e, c),
                            lambda i, s: ((i * per_n + s) // per_e, (i * per_n + s) % per_e, 0))

    res = pl.pallas_call(
        functools.partial(_qkv_kernel, q_dils=tuple(q_dils), kv_dils=kv_dils, tm=tm, n_cast=len(cast)),
        grid=(n, per_n),
        in_specs=[
            pl.BlockSpec((None, tm, D_MODEL), lambda i, s: (i, s, 0)),
            pl.BlockSpec((1, D_MODEL), const2),
            pl.BlockSpec((1, D_MODEL), const2),
            pl.BlockSpec((D_MODEL, 2 * D_MODEL), const2, pipeline_mode=pl.Buffered(1)),
            pl.BlockSpec((D_MODEL, len(q_dils) * D_MODEL), const2, pipeline_mode=pl.Buffered(1)),
            tab, tab, tab,
        ] + [slab(a) for a in cast],
        out_specs=([by_class(d) for d in out_dils] + [pl.BlockSpec((None, tm, D_MODEL), win)] * 2
                   + [slab(a) for a in cast]),
        out_shape=([class_shape(d) for d in out_dils] + [jax.ShapeDtypeStruct((n, keep, D_MODEL), _F32)] * 2
                   + [jax.ShapeDtypeStruct(a.shape, _BF16) for a in cast]),
        scratch_shapes=[
            pltpu.VMEM((len(q_dils) * tiles, tm, LANES), _F32),
            pltpu.VMEM((tiles, tm, LANES), _F32),
            pltpu.VMEM((tiles, tm, LANES), _F32),
        ],
        compiler_params=_params(("parallel", "arbitrary")),
        name="qkv_rope",
    )(h, norm_kv.reshape(1, D_MODEL), norm_q.reshape(1, D_MODEL), w_kv, w_q, *tables, *cast)
    ng, nk = len(q_dils), len(kv_dils)
    qs = res[:ng]
    ks = dict(zip(kv_dils, res[ng:ng + nk]))
    vs = dict(zip(kv_dils, res[ng + nk:ng + 2 * nk]))
    kf, vf = res[ng + 2 * nk], res[ng + 2 * nk + 1]
    return qs, ks, vs, kf, vf, tuple(res[ng + 2 * nk + 2:])


def _band_attn_kernel(q_ref, kh_ref, k_ref, vh_ref, v_ref, o_ref, lse_ref, kx_ref, vx_ref, *, tb):
    m_id = pl.program_id(2)
    kx_ref[0:BAND, :] = kh_ref[...]
    kx_ref[BAND:, :] = k_ref[...]
    vx_ref[0:BAND, :] = vh_ref[...]
    vx_ref[BAND:, :] = v_ref[...]
    row = lax.broadcasted_iota(jnp.int32, (BAND, 2 * BAND), 0)
    col = lax.broadcasted_iota(jnp.int32, (BAND, 2 * BAND), 1)
    dist = row - col + BAND
    bias = jnp.where(dist >= 0, jnp.where(dist <= BAND, 0.0, MASK_VALUE), MASK_VALUE)
    no_prev = jnp.where(m_id > 0, 0.0, MASK_VALUE)
    bias_first = jnp.where(col < BAND, bias + no_prev, bias)
    lane = lax.broadcasted_iota(jnp.int32, (BAND, LANES), 1)
    low = lane < HEAD_DIM
    for b in range(tb):
        bb = bias_first if b == 0 else bias
        rows = slice(b * BAND, (b + 1) * BAND)
        lse_ref[rows, N_HEADS:] = jnp.zeros((BAND, LANES - N_HEADS), _F32)
        for p in range(D_MODEL // LANES):
            sl = slice(p * LANES, (p + 1) * LANES)
            q2 = q_ref[rows, sl].astype(_F32)
            k2 = kx_ref[b * BAND:(b + 2) * BAND, sl]
            v2 = vx_ref[b * BAND:(b + 2) * BAND, sl]
            outs = []
            for hh in range(2):
                qm = (jnp.where(low, q2, 0.0) if hh == 0 else jnp.where(low, 0.0, q2)).astype(_BF16)
                s = lax.dot_general(qm, k2, _NT, preferred_element_type=_F32) + bb
                mx = jnp.max(s, axis=1, keepdims=True)
                pr = jnp.exp(s - mx)
                den = jnp.sum(pr, axis=1, keepdims=True)
                o = _dot(pr.astype(_BF16), v2)
                outs.append(o * (1.0 / den))
                head = 2 * p + hh
                lse_ref[rows, head:head + 1] = mx + jnp.log(den)
            o_ref[rows, sl] = jnp.where(low, outs[0], outs[1]).astype(_BF16)


def _band_attention(q, k, v, group, tb):
    n, dil, length, _ = k.shape
    rows = tb * BAND
    cur = lambda i, r, m: (i, r, m, 0)
    halo = lambda i, r, m: (i, r, jnp.maximum(m * tb - 1, 0), 0)
    band = pl.BlockSpec((None, None, rows, D_MODEL), cur)
    prev = pl.BlockSpec((None, None, BAND, D_MODEL), halo)
    return pl.pallas_call(
        functools.partial(_band_attn_kernel, tb=tb),
        grid=(n, dil, length // rows),
        in_specs=[band, prev, band, prev, band],
        out_specs=[band, pl.BlockSpec((None, None, rows, LANES), cur)],
        out_shape=[
            jax.ShapeDtypeStruct((n, dil, length, D_MODEL), _BF16),
            jax.ShapeDtypeStruct((n, dil, length, LANES), _F32),
        ],
        scratch_shapes=[pltpu.VMEM((BAND + rows, D_MODEL), _BF16), pltpu.VMEM((BAND + rows, D_MODEL), _BF16)],
        compiler_params=_params(("parallel", "parallel", "arbitrary")),
        name="band_attn_g%d" % group,
    )(q, k, k, v, v)


def _head_expand_matrix():
    e = np.zeros((LANES, N_ATTN_GROUPS * D_MODEL), np.float32)
    for part in range(2):
        for g in range(N_ATTN_GROUPS):
            for h in range(N_HEADS):
                r = part * N_ATTN_GROUPS * N_HEADS + g * N_HEADS + h
                e[r, g * D_MODEL + h * HEAD_DIM:g * D_MODEL + (h + 1) * HEAD_DIM] = 1.0
    return jnp.asarray(e, _BF16)


def _merge_proj_kernel(o0_ref, o1_ref, o2_ref, l0_ref, l1_ref, l2_ref, e_ref, h_ref, os_ref, hs_ref, wo_ref,
                       gr_ref, wrh_ref, wrl_ref, out_ref, idx_ref, g1_ref, g2_ref,
                       o1_scr, o2_scr, l1_scr, l2_scr, o_scr, *, n_merge, tm):
    i = pl.program_id(0)
    tiles = D_MODEL // LANES

    @pl.when(i < n_merge)
    def _():
        for o_ref, l_ref, o_dst, l_dst in ((o1_ref, l1_ref, o1_scr, l1_scr), (o2_ref, l2_ref, o2_scr, l2_scr)):
            d = o_ref.shape[0]
            for r in range(d):
                l_dst[pl.ds(r, tm // d, stride=d), :] = l_ref[r]
                for j in range(tiles):
                    o_dst[j, pl.ds(r, tm // d, stride=d), :] = o_ref[r, :, j * LANES:(j + 1) * LANES].astype(_F32)
        l0, l1, l2 = l0_ref[0], l1_scr[...], l2_scr[...]
        mx = jnp.maximum(jnp.maximum(l0, l1), l2)
        e0, e1, e2 = jnp.exp(l0 - mx), jnp.exp(l1 - mx), jnp.exp(l2 - mx)
        inv = 1.0 / (e0 + e1 + e2)
        lane = lax.broadcasted_iota(jnp.int32, l0.shape, 1)
        head = lane < N_HEADS
        a = (jnp.where(head, e0 * inv, 0.0)
             + pltpu.roll(jnp.where(head, e1 * inv, 0.0), N_HEADS, 1)
             + pltpu.roll(jnp.where(head, e2 * inv, 0.0), 2 * N_HEADS, 1))
        hi = a.astype(_BF16).astype(_F32)
        a2 = hi + pltpu.roll(a - hi, N_ATTN_GROUPS * N_HEADS, 1)
        w = _dot(a2.astype(_BF16), e_ref[...])
        for j in range(tiles):
            sl = slice(j * LANES, (j + 1) * LANES)
            o = (w[:, sl] * o0_ref[0, :, sl].astype(_F32)
                 + w[:, D_MODEL + j * LANES:D_MODEL + (j + 1) * LANES] * o1_scr[j]
                 + w[:, 2 * D_MODEL + j * LANES:2 * D_MODEL + (j + 1) * LANES] * o2_scr[j])
            o_scr[:, sl] = o.astype(_BF16)
        out_ref[...] = h_ref[...] + _dot(o_scr[...], wo_ref[...])

    @pl.when(i >= n_merge)
    def _():
        out_ref[...] = hs_ref[...] + _dot(os_ref[...].astype(_BF16), wo_ref[...])

    _route(out_ref[...], gr_ref[...], wrh_ref[...], wrl_ref[...], idx_ref, g1_ref, g2_ref)


def _merge_proj(outs, lses, h, o_sample, h_sample, w_o, route_gain, w_router, tm):
    m, ms = h.shape[0], h_sample.shape[0]
    n, _, seq, _ = outs[0].shape
    n_merge = m // tm
    per_n = seq // tm
    row = lambda i: (jnp.minimum(i, n_merge - 1), 0)
    srow = lambda i: (jnp.maximum(i - n_merge, 0), 0)
    const = lambda i: (0, 0)

    def by_class(a):
        d, width = a.shape[1], a.shape[3]

        def index(i):
            ic = jnp.minimum(i, n_merge - 1)
            return (ic // per_n, 0, ic % per_n, 0)

        return pl.BlockSpec((None, d, tm // d, width), index)

    tiles = D_MODEL // LANES
    wr = jnp.zeros((D_MODEL, LANES), _F32).at[:, :N_EXPERTS].set(w_router)
    wr_hi = wr.astype(_BF16)
    wr_lo = (wr - wr_hi.astype(_F32)).astype(_BF16)
    tile_out = lambda width: pl.BlockSpec((tm, width), lambda i: (i, 0))
    tile_shape = lambda width: jax.ShapeDtypeStruct((m + ms, width), _F32)
    return pl.pallas_call(
        functools.partial(_merge_proj_kernel, n_merge=n_merge, tm=tm),
        grid=((m + ms) // tm,),
        in_specs=[by_class(a) for a in outs] + [by_class(a) for a in lses] + [
            pl.BlockSpec((LANES, N_ATTN_GROUPS * D_MODEL), const),
            pl.BlockSpec((tm, D_MODEL), row),
            pl.BlockSpec((tm, D_MODEL), srow),
            pl.BlockSpec((tm, D_MODEL), srow),
            pl.BlockSpec((D_MODEL, D_MODEL), const),
            pl.BlockSpec((1, D_MODEL), const),
            pl.BlockSpec((D_MODEL, LANES), const),
            pl.BlockSpec((D_MODEL, LANES), const),
        ],
        out_specs=[tile_out(D_MODEL), tile_out(LANES), tile_out(LANES), tile_out(LANES)],
        out_shape=[tile_shape(D_MODEL), tile_shape(LANES), tile_shape(LANES), tile_shape(LANES)],
        scratch_shapes=[
            pltpu.VMEM((tiles, tm, LANES), _F32),
            pltpu.VMEM((tiles, tm, LANES), _F32),
            pltpu.VMEM((tm, LANES), _F32),
            pltpu.VMEM((tm, LANES), _F32),
            pltpu.VMEM((tm, D_MODEL), _BF16),
        ],
        compiler_params=_params(("parallel",)),
        name="merge_out_proj",
    )(*outs, *lses, _head_expand_matrix(), h, o_sample, h_sample, w_o, route_gain.reshape(1, D_MODEL), wr_hi, wr_lo)


def _cached_bias(t, n_ctx):
    rows = N_ATTN_GROUPS * N_HEADS * t
    cache = np.full((rows, n_ctx), MASK_VALUE, np.float32)
    new = np.full((rows, LANES), MASK_VALUE, np.float32)
    for g, (win, dil) in enumerate(ATTN_GROUPS):
        for j in range(t):
            idx = n_ctx + j - np.arange(win // dil + 1) * dil
            idx = idx[idx >= 0]
            for h in range(N_HEADS):
                r = (g * N_HEADS + h) * t + j
                cache[r, idx[idx < n_ctx]] = 0.0
                new[r, idx[idx >= n_ctx] - n_ctx] = 0.0
    return cache, new


def _cached_attn_kernel(q0_ref, q1_ref, q2_ref, kc_ref, vc_ref, kx_ref, vx_ref, kn_ref, vn_ref, knt_ref, vnt_ref,
                        bias_ref, biasn_ref, o_ref, kw_ref, vw_ref, qbd_ref, m_ref, l_ref, acc_ref,
                        *, t, first_chunk):
    c = pl.program_id(1)
    last = pl.num_programs(1) - 1
    gh = N_HEADS * t
    row = lax.broadcasted_iota(jnp.int32, (gh, D_MODEL), 0)
    lane = lax.broadcasted_iota(jnp.int32, (gh, D_MODEL), 1)
    own = jnp.right_shift(row, int(math.log2(t))) == jnp.right_shift(lane, int(math.log2(HEAD_DIM)))

    @pl.when(c == 0)
    def _():
        for g, q_ref in enumerate((q0_ref, q1_ref, q2_ref)):
            tiled = jnp.concatenate([q_ref[...].astype(_F32)] * N_HEADS, axis=0)
            qbd_ref[g * gh:(g + 1) * gh, :] = jnp.where(own, tiled, 0.0).astype(_BF16)
        m_ref[...] = jnp.full(m_ref.shape, M_INIT, _F32)
        l_ref[...] = jnp.zeros_like(l_ref)
        acc_ref[...] = jnp.zeros_like(acc_ref)

    def update(rows, s, weighted_values):
        m_old = m_ref[rows, :]
        m_new = jnp.maximum(m_old, jnp.max(s, axis=1, keepdims=True))
        alpha = jnp.exp(m_old - m_new)
        pr = jnp.exp(s - m_new)
        l_ref[rows, :] = alpha * l_ref[rows, :] + jnp.sum(pr, axis=1, keepdims=True)
        acc_ref[rows, :] = alpha * acc_ref[rows, :] + weighted_values(pr.astype(_BF16))
        m_ref[rows, :] = m_new

    kc, vc = kc_ref[...], vc_ref[...]
    kct, vct = kc.astype(_BF16), vc.astype(_BF16)
    for g in range(N_ATTN_GROUPS):
        rows = slice(g * gh, (g + 1) * gh)

        @pl.when(c >= first_chunk[g])
        def _():
            update(rows, _dot(qbd_ref[rows, :], kct) + bias_ref[rows, :],
                   lambda pr: lax.dot_general(pr, vct, _NT, preferred_element_type=_F32))

    chunk = kc.shape[1]
    for cur, nxt, new_t, w_ref in ((kc, kx_ref, knt_ref, kw_ref), (vc, vx_ref, vnt_ref, vw_ref)):
        tail = jnp.where(c == last, new_t[...], nxt[...])
        ext = jnp.concatenate([cur, tail], axis=1)
        w_ref[...] = pltpu.roll(ext, chunk + LANES - t, 1)[:, 0:chunk]

    @pl.when(c == last)
    def _():
        pad = jnp.zeros((LANES - t, D_MODEL), _F32)
        kn = jnp.concatenate([kn_ref[...].astype(_F32), pad], axis=0).astype(_BF16)
        vn = jnp.concatenate([vn_ref[...].astype(_F32), pad], axis=0).astype(_BF16)
        update(slice(0, N_ATTN_GROUPS * gh),
               lax.dot_general(qbd_ref[...], kn, _NT, preferred_element_type=_F32) + biasn_ref[...],
               lambda pr: _dot(pr, vn))
        ms = [m_ref[g * gh:(g + 1) * gh, :] for g in range(N_ATTN_GROUPS)]
        mx = jnp.maximum(jnp.maximum(ms[0], ms[1]), ms[2])
        num = jnp.zeros((gh, D_MODEL), _F32)
        den = jnp.zeros((gh, 1), _F32)
        for g in range(N_ATTN_GROUPS):
            w = jnp.exp(ms[g] - mx)
            num = num + w * acc_ref[g * gh:(g + 1) * gh, :]
            den = den + w * l_ref[g * gh:(g + 1) * gh, :]
        on = jnp.where(own, num * (1.0 / den), 0.0)
        o = on[0:t, :]
        for h in range(1, N_HEADS):
            o = o + on[h * t:(h + 1) * t, :]
        o_ref[...] = o


def _cached_attention(qs, k_new, v_new, k_new_f32, v_new_f32, cache_kt, cache_vt, chunk):
    n, t, _ = k_new.shape
    n_ctx = cache_kt.shape[2]
    assert n_ctx == MAX_WINDOW and n_ctx % chunk == 0 and chunk % LANES == 0 and t <= LANES
    bias_np, bias_new = _cached_bias(t, n_ctx)
    gh = N_HEADS * t
    rows = N_ATTN_GROUPS * gh
    n_chunks = n_ctx // chunk
    first_chunk = tuple(
        min(c for c in range(n_chunks) if (bias_np[g * gh:(g + 1) * gh, c * chunk:(c + 1) * chunk] == 0.0).any())
        for g in range(N_ATTN_GROUPS))
    new_t = lambda a: jnp.pad(a.transpose(0, 2, 1), ((0, 0), (0, 0), (0, LANES - t)))
    per_n = lambda i, c: (i, 0, 0)
    cur = pl.BlockSpec((None, D_MODEL, chunk), lambda i, c: (i, 0, c))
    ahead = pl.BlockSpec((None, D_MODEL, LANES),
                         lambda i, c: (i, 0, jnp.minimum((c + 1) * (chunk // LANES), n_ctx // LANES - 1)))
    small = pl.BlockSpec((None, t, D_MODEL), per_n)
    small_t = pl.BlockSpec((None, D_MODEL, LANES), per_n)
    window = jax.ShapeDtypeStruct((n, D_MODEL, n_ctx), _F32)
    return pl.pallas_call(
        functools.partial(_cached_attn_kernel, t=t, first_chunk=first_chunk),
        grid=(n, n_chunks),
        in_specs=[
            small, small, small,
            cur, cur, ahead, ahead,
            small, small, small_t, small_t,
            pl.BlockSpec((rows, chunk), lambda i, c: (0, c)),
            pl.BlockSpec((rows, LANES), lambda i, c: (0, 0)),
        ],
        out_specs=[small, cur, cur],
        out_shape=[jax.ShapeDtypeStruct((n, t, D_MODEL), _F32), window, window],
        scratch_shapes=[
            pltpu.VMEM((rows, D_MODEL), _BF16),
            pltpu.VMEM((rows, 1), _F32),
            pltpu.VMEM((rows, 1), _F32),
            pltpu.VMEM((rows, D_MODEL), _F32),
        ],
        compiler_params=_params(("parallel", "arbitrary")),
        name="cached_attn",
    )(*qs, cache_kt, cache_vt, cache_kt, cache_vt, k_new, v_new, new_t(k_new_f32), new_t(v_new_f32),
      jnp.asarray(bias_np), jnp.asarray(bias_new))


def _split_bf16(x):
    hi = x.astype(_BF16)
    return hi, (x - hi.astype(_F32)).astype(_BF16)


def _route(h, gain, wr_hi, wr_lo, idx_ref, g1_ref, g2_ref):
    u_hi, u_lo = _split_bf16(_rms_unit(h) * gain)
    logits = _dot(u_hi, wr_hi) + (_dot(u_hi, wr_lo) + _dot(u_lo, wr_hi))
    lane = lax.broadcasted_iota(jnp.int32, logits.shape, 1).astype(_F32)
    neg = -jnp.inf
    lg = jnp.where(lane < N_EXPERTS, logits, neg)
    v1 = jnp.max(lg, axis=1, keepdims=True)
    i1 = jnp.min(jnp.where(lg == v1, lane, float(LANES)), axis=1, keepdims=True)
    lg2 = jnp.where(lane == i1, neg, lg)
    v2 = jnp.max(lg2, axis=1, keepdims=True)
    i2 = jnp.min(jnp.where(lg2 == v2, lane, float(LANES)), axis=1, keepdims=True)
    e2 = jnp.exp(v2 - v1)
    den = 1.0 + e2
    idx_ref[...] = jnp.where(lane < LANES // 2, i1, i2)
    g1_ref[...] = jnp.broadcast_to(1.0 / den, g1_ref.shape)
    g2_ref[...] = jnp.broadcast_to(e2 / den, g2_ref.shape)


def _invert_kernel(pos_ref, src_ref):
    def clear(p, carry):
        src_ref[p] = 0
        return carry

    lax.fori_loop(0, src_ref.shape[0], clear, 0, unroll=16)

    group = 16
    assert pos_ref.shape[0] % group == 0 and group % TOP_K == 0

    def place(b, carry):
        a0 = b * group
        token0 = b * (group // TOP_K)
        targets = [pos_ref[a0 + u] for u in range(group)]
        for u in range(group):
            src_ref[targets[u]] = token0 + u // TOP_K
        return carry

    lax.fori_loop(0, pos_ref.shape[0] // group, place, 0)


def _invert_positions(pos, n_rows):
    smem = pl.BlockSpec(memory_space=pltpu.SMEM)
    return pl.pallas_call(
        _invert_kernel,
        in_specs=[smem],
        out_specs=smem,
        out_shape=jax.ShapeDtypeStruct((n_rows,), jnp.int32),
        name="invert_positions",
    )(pos)


def _routing_tables(idx_tile, tm, n_tiles):
    m = idx_tile.shape[0]
    e = jnp.stack([idx_tile[:, 0], idx_tile[:, LANES // 2]], axis=1).reshape(TOP_K * m).astype(jnp.int32)
    onehot = (e[:, None] == jnp.arange(N_EXPERTS, dtype=jnp.int32)[None, :]).astype(jnp.int32)
    running = jnp.cumsum(onehot, axis=0)
    counts = running[-1]
    rank = jnp.sum(onehot * running, axis=1) - 1
    padded = (counts + tm - 1) // tm * tm
    ends = jnp.cumsum(padded)
    starts = ends - padded
    pos = jnp.sum(onehot * starts[None, :], axis=1) + rank
    src = _invert_positions(pos, n_tiles * tm)
    n_act = ends[-1] // tm
    tile_id = jnp.arange(n_tiles, dtype=jnp.int32)
    tile_expert = jnp.sum((tile_id[:, None] * tm >= ends[None, :]).astype(jnp.int32), axis=1)
    tile_expert = jnp.where(tile_id < n_act, tile_expert, jnp.take(tile_expert, n_act - 1))
    return src.reshape(n_tiles, 1, tm), tile_expert, n_act.reshape(1), pos.reshape(m, TOP_K)


def _expert_kernel(te_ref, na_ref, rows_ref, rows_next_ref, h_hbm, g_ref, wg_ref, wu_ref, wd_ref, out_ref,
                   xbuf, sem, u_sc, acc_sc, *, tm, n_chunks):
    del te_ref
    i = pl.program_id(0)
    c = pl.program_id(1)
    n_act = na_ref[0]
    slot = lax.rem(i, 2)
    active = i < n_act

    def start_row_copy(rows, r, dst_slot):
        pltpu.make_async_copy(h_hbm.at[pl.ds(rows[0, 0, r], 1)], xbuf.at[dst_slot, pl.ds(r, 1)],
                              sem.at[dst_slot]).start(priority=GATHER_DMA_QUEUE)

    @pl.when((c == 0) & (i == 0))
    def _():
        def body(r, carry):
            start_row_copy(rows_ref, r, 0)
            return carry
        lax.fori_loop(0, tm, body, 0, unroll=8)

    @pl.when((c == 0) & (i <= n_act))
    def _():
        pltpu.make_async_copy(h_hbm.at[pl.ds(0, tm)], xbuf.at[slot], sem.at[slot]).wait()

    tf = wd_ref.shape[0]
    n_sub = tf // EXPERT_SUB_COLS
    per_sub = -(-tm // n_sub)
    for step in range(n_chunks):
        @pl.when(active & (c == step))
        def _():
            if step == 0:
                u_sc[...] = (_rms_unit(xbuf[slot]) * g_ref[...]).astype(_BF16)
            u = u_sc[...]
            acc = None
            for s in range(n_sub):
                cols = slice(s * EXPERT_SUB_COLS, (s + 1) * EXPERT_SUB_COLS)
                y = _swiglu_chunk(u, wg_ref[:, cols], wu_ref[:, cols], wd_ref[cols, :])
                acc = y if acc is None else acc + y
                if step == 0:
                    for r in range(s * per_sub, min((s + 1) * per_sub, tm)):
                        start_row_copy(rows_next_ref, r, 1 - slot)
            acc_sc[...] = acc if step == 0 else acc_sc[...] + acc

    @pl.when(c == n_chunks - 1)
    def _():
        out_ref[...] = jnp.where(active, acc_sc[...], 0.0)


def _expert_layer(h, src_rows, tile_expert, n_act, gain, w_gu, w_down, tm, tf):
    n_tiles = src_rows.shape[0]
    f = w_down.shape[1]
    nc = f // tf
    smem_rows = lambda index_map: pl.BlockSpec((1, 1, tm), index_map, memory_space=pltpu.SMEM)
    grid_spec = pltpu.PrefetchScalarGridSpec(
        num_scalar_prefetch=2,
        grid=(n_tiles, nc),
        in_specs=[
            smem_rows(lambda i, c, te, na: (i, 0, 0)),
            smem_rows(lambda i, c, te, na: (jnp.minimum(i + 1, n_tiles - 1), 0, 0)),
            pl.BlockSpec(memory_space=pl.ANY),
            pl.BlockSpec((1, D_MODEL), lambda i, c, te, na: (0, 0)),
            pl.BlockSpec((None, D_MODEL, tf), lambda i, c, te, na: (te[i], 0, c)),
            pl.BlockSpec((None, D_MODEL, tf), lambda i, c, te, na: (te[i], 0, nc + c)),
            pl.BlockSpec((None, tf, D_MODEL), lambda i, c, te, na: (te[i], c, 0)),
        ],
        out_specs=pl.BlockSpec((tm, D_MODEL), lambda i, c, te, na: (i, 0)),
        scratch_shapes=[
            pltpu.VMEM((2, tm, D_MODEL), _F32),
            pltpu.SemaphoreType.DMA((2,)),
            pltpu.VMEM((tm, D_MODEL), _BF16),
            pltpu.VMEM((tm, D_MODEL), _F32),
        ],
    )
    return pl.pallas_call(
        functools.partial(_expert_kernel, tm=tm, n_chunks=nc),
        grid_spec=grid_spec,
        out_shape=jax.ShapeDtypeStruct((n_tiles * tm, D_MODEL), _F32),
        compiler_params=_params(("arbitrary", "arbitrary")),
        name="swiglu_routed",
    )(tile_expert, n_act, src_rows, src_rows, h, gain.reshape(1, D_MODEL), w_gu, w_gu, w_down)


def _combine_kernel(pos_ref, pos_next_ref, ys_hbm, h_ref, g1_ref, g2_ref, gf_ref, out_ref, buf, sem, *, tq):
    j = pl.program_id(0)
    slot = lax.rem(j, 2)

    def start_gather(pos, dst_slot):
        for r in range(tq):
            for k in range(TOP_K):
                pltpu.make_async_copy(ys_hbm.at[pl.ds(pos[0, k, r], 1)], buf.at[dst_slot, k, pl.ds(r, 1)],
                                      sem.at[dst_slot]).start(priority=k)

    @pl.when(j == 0)
    def _():
        start_gather(pos_ref, 0)

    for s in range(2):
        @pl.when((j + 1 < pl.num_programs(0)) & (slot == s))
        def _():
            start_gather(pos_next_ref, 1 - s)

    for k in range(TOP_K):
        pltpu.make_async_copy(ys_hbm.at[pl.ds(0, tq)], buf.at[slot, k], sem.at[slot]).wait()
    rep = D_MODEL // LANES
    y = jnp.tile(g1_ref[...], (1, rep)) * buf[slot, 0] + jnp.tile(g2_ref[...], (1, rep)) * buf[slot, 1]
    out_ref[...] = _rms_unit(h_ref[...] + y) * gf_ref[...]


def _moe_combine(ys, h, g1, g2, pos, final_gain, tq, tile_off, n_tiles):
    pos3 = pos[tile_off * tq:(tile_off + n_tiles) * tq].reshape(n_tiles, tq, TOP_K).transpose(0, 2, 1)
    smem_pos = lambda index_map: pl.BlockSpec((1, TOP_K, tq), index_map, memory_space=pltpu.SMEM)
    tok = lambda width: pl.BlockSpec((tq, width), lambda j: (j + tile_off, 0))
    return pl.pallas_call(
        functools.partial(_combine_kernel, tq=tq),
        grid=(n_tiles,),
        in_specs=[
            smem_pos(lambda j: (j, 0, 0)),
            smem_pos(lambda j: (jnp.minimum(j + 1, n_tiles - 1), 0, 0)),
            pl.BlockSpec(memory_space=pl.ANY),
            tok(D_MODEL), tok(LANES), tok(LANES),
            pl.BlockSpec((1, D_MODEL), lambda j: (0, 0)),
        ],
        out_specs=pl.BlockSpec((tq, D_MODEL), lambda j: (j, 0)),
        out_shape=jax.ShapeDtypeStruct((n_tiles * tq, D_MODEL), _F32),
        scratch_shapes=[pltpu.VMEM((2, TOP_K, tq, D_MODEL), _F32), pltpu.SemaphoreType.DMA((2,))],
        compiler_params=_params(("arbitrary",)),
        name="moe_combine",
    )(pos3, pos3, ys, h, g1, g2, final_gain.reshape(1, D_MODEL))


def kernel(x_prompt, x_sample, state_pool, cache_k_win, cache_v_win, norm_mix, norm_ffn, pool_w, pool_scale,
           norm_kv, w_kv, w_q, w_o, w_ffn_gu, w_ffn_down, w_router, w_exp_gu, w_exp_down, norm_final):
    nb, seq, _ = x_prompt.shape
    ns, ts, _ = x_sample.shape
    n_ctx = cache_k_win.shape[1]
    mp, ms = nb * seq, ns * ts
    w_kv_b = w_kv.astype(_BF16)
    w_q_b = w_q[0].astype(_BF16)
    w_o_b = w_o[0].astype(_BF16)
    w_ffn_gu_b = w_ffn_gu[0].astype(_BF16)
    w_ffn_down_b = w_ffn_down[0].astype(_BF16)
    tf_ffn = w_ffn_down_b.shape[0] // 2
    tf_exp = w_exp_down.shape[2] // 2

    dils = tuple(d for _, d in ATTN_GROUPS)
    h, pool_prompt = _pool_layer(x_prompt, None, norm_mix[0], pool_w[0], pool_scale[0], tm=TM_POOL, bn=1)
    h = _ffn_layer(h.reshape(mp, D_MODEL), norm_ffn[0], w_ffn_gu_b, w_ffn_down_b, TM_FFN, tf_ffn)
    keep = min(MAX_WINDOW, seq)
    q, k, v, k_win_p, v_win_p, (w_exp_gu_b, w_exp_down_b) = _qkv_layer(
        h.reshape(nb, seq, D_MODEL), jnp.arange(seq, dtype=jnp.int32), norm_kv, norm_mix[1], w_kv_b, w_q_b,
        tm=TM_QKV, keep=keep, q_dils=dils, cast=(w_exp_gu[0], w_exp_down[0]))
    outs, lses = [], []
    for g, d in enumerate(dils):
        o, lse = _band_attention(q[g], k[d], v[d], g, tb=ATTN_BANDS)
        outs.append(o)
        lses.append(lse)

    hs, pool_sample = _pool_layer(x_sample, state_pool[0], norm_mix[0], pool_w[0], pool_scale[0], tm=ts, bn=8)
    hs = _ffn_layer(hs.reshape(ms, D_MODEL), norm_ffn[0], w_ffn_gu_b, w_ffn_down_b, ms, tf_ffn)
    pos_s = jnp.tile(PAST_LEN + jnp.arange(ts, dtype=jnp.int32), ns)
    qs, ks, vs, ks_f, vs_f, _ = _qkv_layer(hs.reshape(1, ms, D_MODEL), pos_s, norm_kv, norm_mix[1], w_kv_b, w_q_b,
                                           tm=ms, keep=ms, q_dils=(1,) * N_ATTN_GROUPS)
    ckt = cache_k_win.transpose(0, 2, 3, 1).reshape(ns, D_MODEL, n_ctx)
    cvt = cache_v_win.transpose(0, 2, 3, 1).reshape(ns, D_MODEL, n_ctx)
    per_sample = lambda a: a.reshape(ns, ts, D_MODEL)
    os_, kwt, vwt = _cached_attention([per_sample(a) for a in qs], per_sample(ks[1]), per_sample(vs[1]),
                                      per_sample(ks_f), per_sample(vs_f), ckt, cvt, chunk=CACHE_CHUNK)
    h_all, idx_tile, g1, g2 = _merge_proj(outs, lses, h, os_.reshape(ms, D_MODEL), hs, w_o_b, norm_ffn[1],
                                          w_router[0], tm=TM_TOKEN)

    m_all = mp + ms
    n_tiles = -(-TOP_K * m_all // TM_EXPERT) + N_EXPERTS + 1
    src_rows, tile_expert, n_act, pos = _routing_tables(idx_tile, TM_EXPERT, n_tiles)
    ys = _expert_layer(h_all, src_rows, tile_expert, n_act, norm_ffn[1], w_exp_gu_b, w_exp_down_b,
                       TM_EXPERT, tf_exp)
    y_prompt = _moe_combine(ys, h_all, g1, g2, pos, norm_final, TM_TOKEN, 0, mp // TM_TOKEN)
    y_sample = _moe_combine(ys, h_all, g1, g2, pos, norm_final, TM_TOKEN, mp // TM_TOKEN, ms // TM_TOKEN)

    heads = lambda a: a.reshape(a.shape[0], a.shape[1], N_HEADS, HEAD_DIM)
    heads_t = lambda a: a.reshape(ns, N_HEADS, HEAD_DIM, a.shape[2]).transpose(0, 3, 1, 2)
    return (y_prompt.reshape(nb, seq, D_MODEL), y_sample.reshape(ns, ts, D_MODEL), pool_prompt, pool_sample,
            heads(k_win_p), heads(v_win_p), heads_t(kwt), heads_t(vwt))
```

```python
import functools
import math

import numpy as np
import jax
import jax.numpy as jnp
from jax import lax
from jax.experimental import pallas as pl
from jax.experimental.pallas import tpu as pltpu

D_MODEL = 1024
PAST_LEN = 16384
POOL_WINDOWS = (2, 4, 8, 16)
POOL_GROUP_DIM = D_MODEL // len(POOL_WINDOWS)
POOL_CTX = max(POOL_WINDOWS) - 1
HEAD_DIM = 64
N_HEADS = D_MODEL // HEAD_DIM
ATTN_GROUPS = ((128, 1), (512, 4), (2048, 16))
N_ATTN_GROUPS = len(ATTN_GROUPS)
BAND = 128
MAX_WINDOW = max(w for w, _ in ATTN_GROUPS)
ROT_DIM = HEAD_DIM // 4
ROPE_THETA = 500000.0
ATTN_SCALE = HEAD_DIM ** -0.5
N_EXPERTS = 8
TOP_K = 2
RMS_EPS = 1e-5

LANES = 128
HALO = 16
MASK_VALUE = -1e30
M_INIT = -1e20
VMEM_LIMIT = 56 * 1024 * 1024

TM_POOL = 512
TM_FFN = 512
TM_QKV = 256
TM_TOKEN = 256
TM_EXPERT = 512
MXU_COLS = 256
ROW_TILE = 8
ATTN_BANDS = 2
CACHE_CHUNK = 512

_F32 = jnp.float32
_BF16 = jnp.bfloat16
_NT = (((1,), (1,)), ((), ()))


def _params(semantics):
    return pltpu.CompilerParams(dimension_semantics=semantics, vmem_limit_bytes=VMEM_LIMIT)


def _rms_unit(x):
    return x * lax.rsqrt(jnp.mean(x * x, axis=-1, keepdims=True) + RMS_EPS)


def _dot(a, b):
    return jnp.dot(a, b, preferred_element_type=_F32)


def _swiglu_chunk(u, wg, wu, wd):
    g = _dot(u, wg)
    up = _dot(u, wu)
    return _dot((g * jax.nn.sigmoid(g) * up).astype(_BF16), wd)


def _pool_kernel(*refs, tm, n_ctx, bn):
    if n_ctx:
        x_ref, ctx_ref, g_ref, w_ref, sc_ref, h_ref, pool_ref, ext_ref = refs
    else:
        x_ref, g_ref, w_ref, sc_ref, h_ref, pool_ref, ext_ref = refs
        ctx_ref = None
    s = pl.program_id(1)
    pos = n_ctx + s * tm + lax.broadcasted_iota(jnp.int32, (tm, 1), 0)
    for b in range(bn):
        x = x_ref[b]
        u = _rms_unit(x) * g_ref[...]

        @pl.when(s == 0)
        def _():
            if n_ctx:
                ext_ref[b, 0:HALO - n_ctx, :] = jnp.zeros((HALO - n_ctx, D_MODEL), _F32)
                ext_ref[b, HALO - n_ctx:HALO, :] = ctx_ref[b]
            else:
                ext_ref[b, 0:HALO, :] = jnp.zeros((HALO, D_MODEL), _F32)

        ext_ref[b, HALO:HALO + tm, :] = u
        for g, w in enumerate(POOL_WINDOWS):
            sl = slice(g * POOL_GROUP_DIM, (g + 1) * POOL_GROUP_DIM)
            acc = u[:, sl]
            for k in range(1, w):
                acc = acc + ext_ref[b, HALO - k:HALO - k + tm, sl]
            cnt = jnp.minimum(pos + 1, w).astype(_F32)
            d = acc / cnt - u[:, sl]
            y = _dot(d.astype(_BF16), w_ref[g])
            h_ref[b, :, sl] = x[:, sl] + y * sc_ref[:, sl]

        @pl.when(s == pl.num_programs(1) - 1)
        def _():
            pool_ref[0, b] = ext_ref[b, HALO + tm - POOL_CTX:HALO + tm, :]

        ext_ref[b, 0:HALO, :] = ext_ref[b, tm:tm + HALO, :]


def _pool_layer(x, ctx, gain, w_pool, scale, tm, bn):
    n, t, _ = x.shape
    n_ctx = 0 if ctx is None else POOL_CTX
    grid = (n // bn, t // tm)
    row = lambda i, s: (i, s, 0)
    const2 = lambda i, s: (0, 0)
    in_specs = [pl.BlockSpec((bn, tm, D_MODEL), row)]
    args = [x]
    if ctx is not None:
        in_specs.append(pl.BlockSpec((bn, POOL_CTX, D_MODEL), lambda i, s: (i, 0, 0)))
        args.append(ctx)
    in_specs += [
        pl.BlockSpec((1, D_MODEL), const2),
        pl.BlockSpec((len(POOL_WINDOWS), POOL_GROUP_DIM, POOL_GROUP_DIM), lambda i, s: (0, 0, 0)),
        pl.BlockSpec((1, D_MODEL), const2),
    ]
    args += [gain.reshape(1, D_MODEL), w_pool.astype(_BF16), scale.reshape(1, D_MODEL)]
    return pl.pallas_call(
        functools.partial(_pool_kernel, tm=tm, n_ctx=n_ctx, bn=bn),
        grid=grid,
        in_specs=in_specs,
        out_specs=[
            pl.BlockSpec((bn, tm, D_MODEL), row),
            pl.BlockSpec((1, bn, POOL_CTX, D_MODEL), lambda i, s: (0, i, 0, 0)),
        ],
        out_shape=[
            jax.ShapeDtypeStruct((n, t, D_MODEL), _F32),
            jax.ShapeDtypeStruct((1, n, POOL_CTX, D_MODEL), _F32),
        ],
        scratch_shapes=[pltpu.VMEM((bn, HALO + tm, D_MODEL), _F32)],
        compiler_params=_params(("parallel", "arbitrary")),
        name="pool_layer",
    )(*args)


def _ffn_kernel(h_ref, g_ref, wg_ref, wu_ref, wd_ref, out_ref, u_sc, acc_sc):
    c = pl.program_id(1)

    @pl.when(c == 0)
    def _():
        u_sc[...] = (_rms_unit(h_ref[...]) * g_ref[...]).astype(_BF16)
        acc_sc[...] = jnp.zeros_like(acc_sc)

    acc_sc[...] += _swiglu_chunk(u_sc[...], wg_ref[...], wu_ref[...], wd_ref[...])

    @pl.when(c == pl.num_programs(1) - 1)
    def _():
        out_ref[...] = h_ref[...] + acc_sc[...]


def _ffn_layer(h, gain, w_gu, w_down, tm, tf):
    m = h.shape[0]
    f = w_down.shape[0]
    nc = f // tf
    row = lambda i, c: (i, 0)
    return pl.pallas_call(
        _ffn_kernel,
        grid=(m // tm, nc),
        in_specs=[
            pl.BlockSpec((tm, D_MODEL), row),
            pl.BlockSpec((1, D_MODEL), lambda i, c: (0, 0)),
            pl.BlockSpec((D_MODEL, tf), lambda i, c: (0, c)),
            pl.BlockSpec((D_MODEL, tf), lambda i, c: (0, nc + c)),
            pl.BlockSpec((tf, D_MODEL), lambda i, c: (c, 0)),
        ],
        out_specs=pl.BlockSpec((tm, D_MODEL), row),
        out_shape=jax.ShapeDtypeStruct((m, D_MODEL), _F32),
        scratch_shapes=[pltpu.VMEM((tm, D_MODEL), _BF16), pltpu.VMEM((tm, D_MODEL), _F32)],
        compiler_params=_params(("parallel", "arbitrary")),
        name="swiglu_dense",
    )(h, gain.reshape(1, D_MODEL), w_gu, w_gu, w_down)


def _rope_tables(pos):
    half = ROT_DIM // 2
    inv_freq = jnp.exp(-math.log(ROPE_THETA) * jnp.arange(half, dtype=_F32) / half)
    ang = pos.astype(_F32)[:, None] * inv_freq[None, :]
    cos, sin = jnp.cos(ang), jnp.sin(ang)
    t = pos.shape[0]
    rest = HEAD_DIM - ROT_DIM
    zero_h = jnp.zeros((t, half), _F32)
    zero_r = jnp.zeros((t, rest), _F32)
    c = jnp.concatenate([cos, cos, jnp.ones((t, rest), _F32)], axis=1)
    s1 = jnp.concatenate([zero_h, sin, zero_r], axis=1)
    s2 = jnp.concatenate([-sin, zero_h, zero_r], axis=1)
    rep = LANES // HEAD_DIM
    return tuple(jnp.tile(a, (1, rep)) for a in (c, s1, s2))


def _store_by_class(t, j, scr, slot, targets, tm):
    sl = slice(j * LANES, (j + 1) * LANES)
    if any(d > 1 for d, _ in targets):
        scr[slot] = t
    for d, ref in targets:
        if d == 1:
            ref[0, :, sl] = t.astype(_BF16)
        else:
            for r in range(d):
                ref[r, :, sl] = scr[slot, pl.ds(r, tm // d, stride=d), :].astype(_BF16)


def _qkv_kernel(*refs, q_dils, kv_dils, tm, n_cast):
    h_ref, gkv_ref, gq_ref, wkv_ref, wq_ref, c_ref, s1_ref, s2_ref = refs[:8]
    rest = list(refs[8:])
    cast_in = [rest.pop(0) for _ in range(n_cast)]
    q_refs = [rest.pop(0) for _ in q_dils]
    k_refs = [rest.pop(0) for _ in kv_dils]
    v_refs = [rest.pop(0) for _ in kv_dils]
    kf_ref, vf_ref = rest.pop(0), rest.pop(0)
    cast_out = [rest.pop(0) for _ in range(n_cast)]
    q_scr, k_scr, v_scr = rest
    hn = _rms_unit(h_ref[...])
    ukv = (hn * gkv_ref[...]).astype(_BF16)
    uq = (hn * gq_ref[...]).astype(_BF16)
    c, s1, s2 = c_ref[...], s1_ref[...], s2_ref[...]
    half = ROT_DIM // 2
    tiles = D_MODEL // LANES
    cols = MXU_COLS // LANES

    def rope(t):
        return t * c + pltpu.roll(t, half, 1) * s1 + pltpu.roll(t, LANES - half, 1) * s2

    for jb in range(2 * tiles // cols):
        blk = _dot(ukv, wkv_ref[:, jb * MXU_COLS:(jb + 1) * MXU_COLS])
        for jj in range(cols):
            j = jb * cols + jj
            t = blk[:, jj * LANES:(jj + 1) * LANES]
            if j < tiles:
                t = rope(t)
                kf_ref[:, j * LANES:(j + 1) * LANES] = t
                _store_by_class(t, j, k_scr, j, list(zip(kv_dils, k_refs)), tm)
            else:
                j -= tiles
                vf_ref[:, j * LANES:(j + 1) * LANES] = t
                _store_by_class(t, j, v_scr, j, list(zip(kv_dils, v_refs)), tm)
    for g, d in enumerate(q_dils):
        for jb in range(tiles // cols):
            c0 = g * D_MODEL + jb * MXU_COLS
            blk = _dot(uq, wq_ref[:, c0:c0 + MXU_COLS])
            for jj in range(cols):
                j = jb * cols + jj
                t = rope(blk[:, jj * LANES:(jj + 1) * LANES]) * ATTN_SCALE
                _store_by_class(t, j, q_scr, g * tiles + j, [(d, q_refs[g])], tm)
    for src, dst in zip(cast_in, cast_out):
        dst[...] = src[...].astype(_BF16)


def _qkv_layer(h, pos, norm_kv, norm_q, w_kv, w_q, tm, keep, q_dils, cast=()):
    n, t, _ = h.shape
    off = (t - keep) // tm
    kv_dils = tuple(sorted(set(q_dils)))
    tables = _rope_tables(pos)
    const2 = lambda i, s: (0, 0)
    tab = pl.BlockSpec((tm, LANES), lambda i, s: (s, 0))
    win = lambda i, s: (i, jnp.maximum(s - off, 0), 0)
    by_class = lambda d: pl.BlockSpec((None, d, tm // d, D_MODEL), lambda i, s: (i, 0, s, 0))
    class_shape = lambda d: jax.ShapeDtypeStruct((n, d, t // d, D_MODEL), _BF16)
    out_dils = tuple(q_dils) + kv_dils + kv_dils
    tiles = D_MODEL // LANES
    per_n = t // tm
    steps = n * per_n

    def slab(a):
        e, r, c = a.shape
        per_e = steps // e
        assert steps % e == 0 and r % per_e == 0 and (r // per_e) % 16 == 0
        return pl.BlockSpec((None, r // per_e, c),
                            lambda i, s: ((i * per_n + s) // per_e, (i * per_n + s) % per_e, 0))

    res = pl.pallas_call(
        functools.partial(_qkv_kernel, q_dils=tuple(q_dils), kv_dils=kv_dils, tm=tm, n_cast=len(cast)),
        grid=(n, per_n),
        in_specs=[
            pl.BlockSpec((None, tm, D_MODEL), lambda i, s: (i, s, 0)),
            pl.BlockSpec((1, D_MODEL), const2),
            pl.BlockSpec((1, D_MODEL), const2),
            pl.BlockSpec((D_MODEL, 2 * D_MODEL), const2, pipeline_mode=pl.Buffered(1)),
            pl.BlockSpec((D_MODEL, len(q_dils) * D_MODEL), const2, pipeline_mode=pl.Buffered(1)),
            tab, tab, tab,
        ] + [slab(a) for a in cast],
        out_specs=([by_class(d) for d in out_dils] + [pl.BlockSpec((None, tm, D_MODEL), win)] * 2
                   + [slab(a) for a in cast]),
        out_shape=([class_shape(d) for d in out_dils] + [jax.ShapeDtypeStruct((n, keep, D_MODEL), _F32)] * 2
                   + [jax.ShapeDtypeStruct(a.shape, _BF16) for a in cast]),
        scratch_shapes=[
            pltpu.VMEM((len(q_dils) * tiles, tm, LANES), _F32),
            pltpu.VMEM((tiles, tm, LANES), _F32),
            pltpu.VMEM((tiles, tm, LANES), _F32),
        ],
        compiler_params=_params(("parallel", "arbitrary")),
        name="qkv_rope",
    )(h, norm_kv.reshape(1, D_MODEL), norm_q.reshape(1, D_MODEL), w_kv, w_q, *tables, *cast)
    ng, nk = len(q_dils), len(kv_dils)
    qs = res[:ng]
    ks = dict(zip(kv_dils, res[ng:ng + nk]))
    vs = dict(zip(kv_dils, res[ng + nk:ng + 2 * nk]))
    kf, vf = res[ng + 2 * nk], res[ng + 2 * nk + 1]
    return qs, ks, vs, kf, vf, tuple(res[ng + 2 * nk + 2:])


def _band_attn_kernel(q_ref, kh_ref, k_ref, vh_ref, v_ref, o_ref, lse_ref, kx_ref, vx_ref, *, tb):
    m_id = pl.program_id(2)
    kx_ref[0:BAND, :] = kh_ref[...]
    kx_ref[BAND:, :] = k_ref[...]
    vx_ref[0:BAND, :] = vh_ref[...]
    vx_ref[BAND:, :] = v_ref[...]
    row = lax.broadcasted_iota(jnp.int32, (BAND, 2 * BAND), 0)
    col = lax.broadcasted_iota(jnp.int32, (BAND, 2 * BAND), 1)
    dist = row - col + BAND
    bias = jnp.where(dist >= 0, jnp.where(dist <= BAND, 0.0, MASK_VALUE), MASK_VALUE)
    no_prev = jnp.where(m_id > 0, 0.0, MASK_VALUE)
    bias_first = jnp.where(col < BAND, bias + no_prev, bias)
    lane = lax.broadcasted_iota(jnp.int32, (BAND, LANES), 1)
    low = lane < HEAD_DIM
    for b in range(tb):
        bb = bias_first if b == 0 else bias
        rows = slice(b * BAND, (b + 1) * BAND)
        lse_ref[rows, N_HEADS:] = jnp.zeros((BAND, LANES - N_HEADS), _F32)
        for p in range(D_MODEL // LANES):
            sl = slice(p * LANES, (p + 1) * LANES)
            q2 = q_ref[rows, sl].astype(_F32)
            k2 = kx_ref[b * BAND:(b + 2) * BAND, sl]
            v2 = vx_ref[b * BAND:(b + 2) * BAND, sl]
            outs = []
            for hh in range(2):
                qm = (jnp.where(low, q2, 0.0) if hh == 0 else jnp.where(low, 0.0, q2)).astype(_BF16)
                s = lax.dot_general(qm, k2, _NT, preferred_element_type=_F32) + bb
                mx = jnp.max(s, axis=1, keepdims=True)
                pr = jnp.exp(s - mx)
                den = jnp.sum(pr, axis=1, keepdims=True)
                o = _dot(pr.astype(_BF16), v2)
                outs.append(o * (1.0 / den))
                head = 2 * p + hh
                lse_ref[rows, head:head + 1] = mx + jnp.log(den)
            o_ref[rows, sl] = jnp.where(low, outs[0], outs[1]).astype(_BF16)


def _band_attention(q, k, v, group, tb):
    n, dil, length, _ = k.shape
    rows = tb * BAND
    cur = lambda i, r, m: (i, r, m, 0)
    halo = lambda i, r, m: (i, r, jnp.maximum(m * tb - 1, 0), 0)
    band = pl.BlockSpec((None, None, rows, D_MODEL), cur)
    prev = pl.BlockSpec((None, None, BAND, D_MODEL), halo)
    return pl.pallas_call(
        functools.partial(_band_attn_kernel, tb=tb),
        grid=(n, dil, length // rows),
        in_specs=[band, prev, band, prev, band],
        out_specs=[band, pl.BlockSpec((None, None, rows, LANES), cur)],
        out_shape=[
            jax.ShapeDtypeStruct((n, dil, length, D_MODEL), _BF16),
            jax.ShapeDtypeStruct((n, dil, length, LANES), _F32),
        ],
        scratch_shapes=[pltpu.VMEM((BAND + rows, D_MODEL), _BF16), pltpu.VMEM((BAND + rows, D_MODEL), _BF16)],
        compiler_params=_params(("parallel", "parallel", "arbitrary")),
        name="band_attn_g%d" % group,
    )(q, k, k, v, v)


def _head_expand_matrix():
    e = np.zeros((LANES, N_ATTN_GROUPS * D_MODEL), np.float32)
    for part in range(2):
        for g in range(N_ATTN_GROUPS):
            for h in range(N_HEADS):
                r = part * N_ATTN_GROUPS * N_HEADS + g * N_HEADS + h
                e[r, g * D_MODEL + h * HEAD_DIM:g * D_MODEL + (h + 1) * HEAD_DIM] = 1.0
    return jnp.asarray(e, _BF16)


def _merge_proj_kernel(o0_ref, o1_ref, o2_ref, l0_ref, l1_ref, l2_ref, e_ref, h_ref, os_ref, hs_ref, wo_ref,
                       gr_ref, wrh_ref, wrl_ref, out_ref, ht_ref, idx_ref, g1_ref, g2_ref,
                       o1_scr, o2_scr, l1_scr, l2_scr, o_scr, *, n_merge, tm):
    i = pl.program_id(0)
    tiles = D_MODEL // LANES

    @pl.when(i < n_merge)
    def _():
        for o_ref, l_ref, o_dst, l_dst in ((o1_ref, l1_ref, o1_scr, l1_scr), (o2_ref, l2_ref, o2_scr, l2_scr)):
            d = o_ref.shape[0]
            for r in range(d):
                l_dst[pl.ds(r, tm // d, stride=d), :] = l_ref[r]
                for j in range(tiles):
                    o_dst[j, pl.ds(r, tm // d, stride=d), :] = o_ref[r, :, j * LANES:(j + 1) * LANES].astype(_F32)
        l0, l1, l2 = l0_ref[0], l1_scr[...], l2_scr[...]
        mx = jnp.maximum(jnp.maximum(l0, l1), l2)
        e0, e1, e2 = jnp.exp(l0 - mx), jnp.exp(l1 - mx), jnp.exp(l2 - mx)
        inv = 1.0 / (e0 + e1 + e2)
        lane = lax.broadcasted_iota(jnp.int32, l0.shape, 1)
        head = lane < N_HEADS
        a = (jnp.where(head, e0 * inv, 0.0)
             + pltpu.roll(jnp.where(head, e1 * inv, 0.0), N_HEADS, 1)
             + pltpu.roll(jnp.where(head, e2 * inv, 0.0), 2 * N_HEADS, 1))
        hi = a.astype(_BF16).astype(_F32)
        a2 = hi + pltpu.roll(a - hi, N_ATTN_GROUPS * N_HEADS, 1)
        w = _dot(a2.astype(_BF16), e_ref[...])
        for j in range(tiles):
            sl = slice(j * LANES, (j + 1) * LANES)
            o = (w[:, sl] * o0_ref[0, :, sl].astype(_F32)
                 + w[:, D_MODEL + j * LANES:D_MODEL + (j + 1) * LANES] * o1_scr[j]
                 + w[:, 2 * D_MODEL + j * LANES:2 * D_MODEL + (j + 1) * LANES] * o2_scr[j])
            o_scr[:, sl] = o.astype(_BF16)
        out_ref[...] = h_ref[...] + _dot(o_scr[...], wo_ref[...])

    @pl.when(i >= n_merge)
    def _():
        out_ref[...] = hs_ref[...] + _dot(os_ref[...].astype(_BF16), wo_ref[...])

    hn = out_ref[...]
    _to_row_tiles(hn, ht_ref)
    _route(hn, gr_ref[...], wrh_ref[...], wrl_ref[...], idx_ref, g1_ref, g2_ref)


def _merge_proj(outs, lses, h, o_sample, h_sample, w_o, route_gain, w_router, tm):
    m, ms = h.shape[0], h_sample.shape[0]
    n, _, seq, _ = outs[0].shape
    n_merge = m // tm
    per_n = seq // tm
    row = lambda i: (jnp.minimum(i, n_merge - 1), 0)
    srow = lambda i: (jnp.maximum(i - n_merge, 0), 0)
    const = lambda i: (0, 0)

    def by_class(a):
        d, width = a.shape[1], a.shape[3]

        def index(i):
            ic = jnp.minimum(i, n_merge - 1)
            return (ic // per_n, 0, ic % per_n, 0)

        return pl.BlockSpec((None, d, tm // d, width), index)

    tiles = D_MODEL // LANES
    wr = jnp.zeros((D_MODEL, LANES), _F32).at[:, :N_EXPERTS].set(w_router)
    wr_hi = wr.astype(_BF16)
    wr_lo = (wr - wr_hi.astype(_F32)).astype(_BF16)
    tile_out = lambda width: pl.BlockSpec((tm, width), lambda i: (i, 0))
    tile_shape = lambda width: jax.ShapeDtypeStruct((m + ms, width), _F32)
    return pl.pallas_call(
        functools.partial(_merge_proj_kernel, n_merge=n_merge, tm=tm),
        grid=((m + ms) // tm,),
        in_specs=[by_class(a) for a in outs] + [by_class(a) for a in lses] + [
            pl.BlockSpec((LANES, N_ATTN_GROUPS * D_MODEL), const),
            pl.BlockSpec((tm, D_MODEL), row),
            pl.BlockSpec((tm, D_MODEL), srow),
            pl.BlockSpec((tm, D_MODEL), srow),
            pl.BlockSpec((D_MODEL, D_MODEL), const),
            pl.BlockSpec((1, D_MODEL), const),
            pl.BlockSpec((D_MODEL, LANES), const),
            pl.BlockSpec((D_MODEL, LANES), const),
        ],
        out_specs=[tile_out(D_MODEL), pl.BlockSpec((tm * ROW_TILE, LANES), lambda i: (i, 0)),
                   tile_out(LANES), tile_out(LANES), tile_out(LANES)],
        out_shape=[tile_shape(D_MODEL), jax.ShapeDtypeStruct(((m + ms) * ROW_TILE, LANES), _F32),
                   tile_shape(LANES), tile_shape(LANES), tile_shape(LANES)],
        scratch_shapes=[
            pltpu.VMEM((tiles, tm, LANES), _F32),
            pltpu.VMEM((tiles, tm, LANES), _F32),
            pltpu.VMEM((tm, LANES), _F32),
            pltpu.VMEM((tm, LANES), _F32),
            pltpu.VMEM((tm, D_MODEL), _BF16),
        ],
        compiler_params=_params(("parallel",)),
        name="merge_out_proj",
    )(*outs, *lses, _head_expand_matrix(), h, o_sample, h_sample, w_o, route_gain.reshape(1, D_MODEL), wr_hi, wr_lo)


def _cached_bias(t, n_ctx):
    rows = N_ATTN_GROUPS * N_HEADS * t
    cache = np.full((rows, n_ctx), MASK_VALUE, np.float32)
    new = np.full((rows, LANES), MASK_VALUE, np.float32)
    for g, (win, dil) in enumerate(ATTN_GROUPS):
        for j in range(t):
            idx = n_ctx + j - np.arange(win // dil + 1) * dil
            idx = idx[idx >= 0]
            for h in range(N_HEADS):
                r = (g * N_HEADS + h) * t + j
                cache[r, idx[idx < n_ctx]] = 0.0
                new[r, idx[idx >= n_ctx] - n_ctx] = 0.0
    return cache, new


def _cached_attn_kernel(q0_ref, q1_ref, q2_ref, kc_ref, vc_ref, kx_ref, vx_ref, kn_ref, vn_ref, knt_ref, vnt_ref,
                        bias_ref, biasn_ref, o_ref, kw_ref, vw_ref, qbd_ref, m_ref, l_ref, acc_ref,
                        *, t, first_chunk):
    c = pl.program_id(1)
    last = pl.num_programs(1) - 1
    gh = N_HEADS * t
    row = lax.broadcasted_iota(jnp.int32, (gh, D_MODEL), 0)
    lane = lax.broadcasted_iota(jnp.int32, (gh, D_MODEL), 1)
    own = jnp.right_shift(row, int(math.log2(t))) == jnp.right_shift(lane, int(math.log2(HEAD_DIM)))

    @pl.when(c == 0)
    def _():
        for g, q_ref in enumerate((q0_ref, q1_ref, q2_ref)):
            tiled = jnp.concatenate([q_ref[...].astype(_F32)] * N_HEADS, axis=0)
            qbd_ref[g * gh:(g + 1) * gh, :] = jnp.where(own, tiled, 0.0).astype(_BF16)
        m_ref[...] = jnp.full(m_ref.shape, M_INIT, _F32)
        l_ref[...] = jnp.zeros_like(l_ref)
        acc_ref[...] = jnp.zeros_like(acc_ref)

    def update(rows, s, weighted_values):
        m_old = m_ref[rows, :]
        m_new = jnp.maximum(m_old, jnp.max(s, axis=1, keepdims=True))
        alpha = jnp.exp(m_old - m_new)
        pr = jnp.exp(s - m_new)
        l_ref[rows, :] = alpha * l_ref[rows, :] + jnp.sum(pr, axis=1, keepdims=True)
        acc_ref[rows, :] = alpha * acc_ref[rows, :] + weighted_values(pr.astype(_BF16))
        m_ref[rows, :] = m_new

    kc, vc = kc_ref[...], vc_ref[...]
    kct, vct = kc.astype(_BF16), vc.astype(_BF16)
    for g in range(N_ATTN_GROUPS):
        rows = slice(g * gh, (g + 1) * gh)

        @pl.when(c >= first_chunk[g])
        def _():
            update(rows, _dot(qbd_ref[rows, :], kct) + bias_ref[rows, :],
                   lambda pr: lax.dot_general(pr, vct, _NT, preferred_element_type=_F32))

    chunk = kc.shape[1]
    for cur, nxt, new_t, w_ref in ((kc, kx_ref, knt_ref, kw_ref), (vc, vx_ref, vnt_ref, vw_ref)):
        tail = jnp.where(c == last, new_t[...], nxt[...])
        ext = jnp.concatenate([cur, tail], axis=1)
        w_ref[...] = pltpu.roll(ext, chunk + LANES - t, 1)[:, 0:chunk]

    @pl.when(c == last)
    def _():
        pad = jnp.zeros((LANES - t, D_MODEL), _F32)
        kn = jnp.concatenate([kn_ref[...].astype(_F32), pad], axis=0).astype(_BF16)
        vn = jnp.concatenate([vn_ref[...].astype(_F32), pad], axis=0).astype(_BF16)
        update(slice(0, N_ATTN_GROUPS * gh),
               lax.dot_general(qbd_ref[...], kn, _NT, preferred_element_type=_F32) + biasn_ref[...],
               lambda pr: _dot(pr, vn))
        ms = [m_ref[g * gh:(g + 1) * gh, :] for g in range(N_ATTN_GROUPS)]
        mx = jnp.maximum(jnp.maximum(ms[0], ms[1]), ms[2])
        num = jnp.zeros((gh, D_MODEL), _F32)
        den = jnp.zeros((gh, 1), _F32)
        for g in range(N_ATTN_GROUPS):
            w = jnp.exp(ms[g] - mx)
            num = num + w * acc_ref[g * gh:(g + 1) * gh, :]
            den = den + w * l_ref[g * gh:(g + 1) * gh, :]
        on = jnp.where(own, num * (1.0 / den), 0.0)
        o = on[0:t, :]
        for h in range(1, N_HEADS):
            o = o + on[h * t:(h + 1) * t, :]
        o_ref[...] = o


def _cached_attention(qs, k_new, v_new, k_new_f32, v_new_f32, cache_kt, cache_vt, chunk):
    n, t, _ = k_new.shape
    n_ctx = cache_kt.shape[2]
    assert n_ctx == MAX_WINDOW and n_ctx % chunk == 0 and chunk % LANES == 0 and t <= LANES
    bias_np, bias_new = _cached_bias(t, n_ctx)
    gh = N_HEADS * t
    rows = N_ATTN_GROUPS * gh
    n_chunks = n_ctx // chunk
    first_chunk = tuple(
        min(c for c in range(n_chunks) if (bias_np[g * gh:(g + 1) * gh, c * chunk:(c + 1) * chunk] == 0.0).any())
        for g in range(N_ATTN_GROUPS))
    new_t = lambda a: jnp.pad(a.transpose(0, 2, 1), ((0, 0), (0, 0), (0, LANES - t)))
    per_n = lambda i, c: (i, 0, 0)
    cur = pl.BlockSpec((None, D_MODEL, chunk), lambda i, c: (i, 0, c))
    ahead = pl.BlockSpec((None, D_MODEL, LANES),
                         lambda i, c: (i, 0, jnp.minimum((c + 1) * (chunk // LANES), n_ctx // LANES - 1)))
    small = pl.BlockSpec((None, t, D_MODEL), per_n)
    small_t = pl.BlockSpec((None, D_MODEL, LANES), per_n)
    window = jax.ShapeDtypeStruct((n, D_MODEL, n_ctx), _F32)
    return pl.pallas_call(
        functools.partial(_cached_attn_kernel, t=t, first_chunk=first_chunk),
        grid=(n, n_chunks),
        in_specs=[
            small, small, small,
            cur, cur, ahead, ahead,
            small, small, small_t, small_t,
            pl.BlockSpec((rows, chunk), lambda i, c: (0, c)),
            pl.BlockSpec((rows, LANES), lambda i, c: (0, 0)),
        ],
        out_specs=[small, cur, cur],
        out_shape=[jax.ShapeDtypeStruct((n, t, D_MODEL), _F32), window, window],
        scratch_shapes=[
            pltpu.VMEM((rows, D_MODEL), _BF16),
            pltpu.VMEM((rows, 1), _F32),
            pltpu.VMEM((rows, 1), _F32),
            pltpu.VMEM((rows, D_MODEL), _F32),
        ],
        compiler_params=_params(("parallel", "arbitrary")),
        name="cached_attn",
    )(*qs, cache_kt, cache_vt, cache_kt, cache_vt, k_new, v_new, new_t(k_new_f32), new_t(v_new_f32),
      jnp.asarray(bias_np), jnp.asarray(bias_new))


def _split_bf16(x):
    hi = x.astype(_BF16)
    return hi, (x - hi.astype(_F32)).astype(_BF16)


def _route(h, gain, wr_hi, wr_lo, idx_ref, g1_ref, g2_ref):
    u_hi, u_lo = _split_bf16(_rms_unit(h) * gain)
    logits = _dot(u_hi, wr_hi) + (_dot(u_hi, wr_lo) + _dot(u_lo, wr_hi))
    lane = lax.broadcasted_iota(jnp.int32, logits.shape, 1).astype(_F32)
    neg = -jnp.inf
    lg = jnp.where(lane < N_EXPERTS, logits, neg)
    v1 = jnp.max(lg, axis=1, keepdims=True)
    i1 = jnp.min(jnp.where(lg == v1, lane, float(LANES)), axis=1, keepdims=True)
    lg2 = jnp.where(lane == i1, neg, lg)
    v2 = jnp.max(lg2, axis=1, keepdims=True)
    i2 = jnp.min(jnp.where(lg2 == v2, lane, float(LANES)), axis=1, keepdims=True)
    e2 = jnp.exp(v2 - v1)
    den = 1.0 + e2
    idx_ref[...] = jnp.where(lane < LANES // 2, i1, i2)
    g1_ref[...] = jnp.broadcast_to(1.0 / den, g1_ref.shape)
    g2_ref[...] = jnp.broadcast_to(e2 / den, g2_ref.shape)


def _invert_kernel(pos_ref, src_ref):
    def clear(p, carry):
        src_ref[p] = 0
        return carry

    lax.fori_loop(0, src_ref.shape[0], clear, 0, unroll=16)

    group = 16
    assert pos_ref.shape[0] % group == 0 and group % TOP_K == 0

    def place(b, carry):
        a0 = b * group
        token0 = b * (group // TOP_K)
        targets = [pos_ref[a0 + u] for u in range(group)]
        for u in range(group):
            src_ref[targets[u]] = token0 + u // TOP_K
        return carry

    lax.fori_loop(0, pos_ref.shape[0] // group, place, 0)


def _invert_positions(pos, n_rows):
    smem = pl.BlockSpec(memory_space=pltpu.SMEM)
    return pl.pallas_call(
        _invert_kernel,
        in_specs=[smem],
        out_specs=smem,
        out_shape=jax.ShapeDtypeStruct((n_rows,), jnp.int32),
        name="invert_positions",
    )(pos)


def _routing_tables(idx_tile, tm, n_tiles):
    m = idx_tile.shape[0]
    e = jnp.stack([idx_tile[:, 0], idx_tile[:, LANES // 2]], axis=1).reshape(TOP_K * m).astype(jnp.int32)
    onehot = (e[:, None] == jnp.arange(N_EXPERTS, dtype=jnp.int32)[None, :]).astype(jnp.int32)
    running = jnp.cumsum(onehot, axis=0)
    counts = running[-1]
    rank = jnp.sum(onehot * running, axis=1) - 1
    padded = (counts + tm - 1) // tm * tm
    ends = jnp.cumsum(padded)
    starts = ends - padded
    pos = jnp.sum(onehot * starts[None, :], axis=1) + rank
    src = _invert_positions(pos, n_tiles * tm)
    n_act = ends[-1] // tm
    tile_id = jnp.arange(n_tiles, dtype=jnp.int32)
    tile_expert = jnp.sum((tile_id[:, None] * tm >= ends[None, :]).astype(jnp.int32), axis=1)
    tile_expert = jnp.where(tile_id < n_act, tile_expert, jnp.take(tile_expert, n_act - 1))
    return src.reshape(n_tiles, 1, tm), tile_expert, n_act.reshape(1), pos.reshape(m, TOP_K)


def _to_row_tiles(x, dst_ref):
    rows = x.shape[0]
    for j in range(D_MODEL // LANES):
        dst_ref[pl.ds(j, rows, stride=ROW_TILE), :] = x[:, j * LANES:(j + 1) * LANES]


def _from_row_tiles(src_ref, base, rows):
    return jnp.concatenate(
        [src_ref[pl.ds(base * ROW_TILE + j, rows, stride=ROW_TILE), :] for j in range(D_MODEL // LANES)], axis=1)


def _start_row_gather(src_hbm, index, buf, slot, r, rows, sem, queue):
    dst = (slot * rows + r) * ROW_TILE
    pltpu.make_async_copy(src_hbm.at[pl.ds(index * ROW_TILE, ROW_TILE)], buf.at[pl.ds(dst, ROW_TILE)],
                          sem.at[slot]).start(priority=queue)


def _wait_row_gather(src_hbm, buf, slot, rows, sem):
    n = rows * ROW_TILE
    pltpu.make_async_copy(src_hbm.at[pl.ds(0, n)], buf.at[pl.ds(slot * n, n)], sem.at[slot]).wait()


def _sort_rows_kernel(na_ref, rows_ref, rows_next_ref, h_hbm, g_ref, out_ref, buf, sem, *, tm):
    i = pl.program_id(0)
    n_act = na_ref[0]
    slot = lax.rem(i, 2)

    def start_gather(rows, dst_slot):
        for r in range(tm):
            _start_row_gather(h_hbm, rows[0, 0, r], buf, dst_slot, r, tm, sem, r % 2)

    @pl.when(i == 0)
    def _():
        start_gather(rows_ref, 0)

    for s in range(2):
        @pl.when((i + 1 < n_act) & (slot == s))
        def _():
            start_gather(rows_next_ref, 1 - s)

    @pl.when(i < n_act)
    def _():
        for s in range(2):
            @pl.when(slot == s)
            def _():
                _wait_row_gather(h_hbm, buf, s, tm, sem)
                out_ref[...] = (_rms_unit(_from_row_tiles(buf, s * tm, tm)) * g_ref[...]).astype(_BF16)

    @pl.when(i >= n_act)
    def _():
        out_ref[...] = jnp.zeros_like(out_ref)


def _sort_rows(h_tiles, src_rows, n_act, gain, tm):
    n_tiles = src_rows.shape[0]
    smem_rows = lambda index_map: pl.BlockSpec((1, 1, tm), index_map, memory_space=pltpu.SMEM)
    grid_spec = pltpu.PrefetchScalarGridSpec(
        num_scalar_prefetch=1,
        grid=(n_tiles,),
        in_specs=[
            smem_rows(lambda i, na: (i, 0, 0)),
            smem_rows(lambda i, na: (jnp.minimum(i + 1, n_tiles - 1), 0, 0)),
            pl.BlockSpec(memory_space=pl.ANY),
            pl.BlockSpec((1, D_MODEL), lambda i, na: (0, 0)),
        ],
        out_specs=pl.BlockSpec((tm, D_MODEL), lambda i, na: (i, 0)),
        scratch_shapes=[pltpu.VMEM((2 * tm * ROW_TILE, LANES), _F32), pltpu.SemaphoreType.DMA((2,))],
    )
    return pl.pallas_call(
        functools.partial(_sort_rows_kernel, tm=tm),
        grid_spec=grid_spec,
        out_shape=jax.ShapeDtypeStruct((n_tiles * tm, D_MODEL), _BF16),
        compiler_params=_params(("arbitrary",)),
        name="sort_rows",
    )(n_act, src_rows, src_rows, h_tiles, gain.reshape(1, D_MODEL))


def _expert_kernel(te_ref, na_ref, x_ref, wg_ref, wu_ref, wd_ref, out_ref, acc_sc):
    del te_ref
    i = pl.program_id(0)
    c = pl.program_id(1)
    active = i < na_ref[0]

    @pl.when(active)
    def _():
        y = _swiglu_chunk(x_ref[...], wg_ref[...], wu_ref[...], wd_ref[...])

        @pl.when(c == 0)
        def _():
            acc_sc[...] = y

        @pl.when(c > 0)
        def _():
            acc_sc[...] += y

    @pl.when(c == pl.num_programs(1) - 1)
    def _():
        _to_row_tiles(jnp.where(active, acc_sc[...], 0.0), out_ref)


def _expert_layer(x_sorted, tile_expert, n_act, w_gu, w_down, tm, tf):
    n_tiles = x_sorted.shape[0] // tm
    f = w_down.shape[1]
    nc = f // tf
    grid_spec = pltpu.PrefetchScalarGridSpec(
        num_scalar_prefetch=2,
        grid=(n_tiles, nc),
        in_specs=[
            pl.BlockSpec((tm, D_MODEL), lambda i, c, te, na: (i, 0)),
            pl.BlockSpec((None, D_MODEL, tf), lambda i, c, te, na: (te[i], 0, c)),
            pl.BlockSpec((None, D_MODEL, tf), lambda i, c, te, na: (te[i], 0, nc + c)),
            pl.BlockSpec((None, tf, D_MODEL), lambda i, c, te, na: (te[i], c, 0)),
        ],
        out_specs=pl.BlockSpec((tm * ROW_TILE, LANES), lambda i, c, te, na: (i, 0)),
        scratch_shapes=[pltpu.VMEM((tm, D_MODEL), _F32)],
    )
    return pl.pallas_call(
        _expert_kernel,
        grid_spec=grid_spec,
        out_shape=jax.ShapeDtypeStruct((n_tiles * tm * ROW_TILE, LANES), _F32),
        compiler_params=_params(("parallel", "arbitrary")),
        name="swiglu_routed",
    )(tile_expert, n_act, x_sorted, w_gu, w_gu, w_down)


def _combine_kernel(pos_ref, pos_next_ref, ys_hbm, h_ref, g1_ref, g2_ref, gf_ref, out_ref, buf, sem, *, tq):
    j = pl.program_id(0)
    slot = lax.rem(j, 2)
    rows = TOP_K * tq

    def start_gather(pos, dst_slot):
        for r in range(tq):
            for k in range(TOP_K):
                _start_row_gather(ys_hbm, pos[0, k, r], buf, dst_slot, k * tq + r, rows, sem, k)

    @pl.when(j == 0)
    def _():
        start_gather(pos_ref, 0)

    for s in range(2):
        @pl.when((j + 1 < pl.num_programs(0)) & (slot == s))
        def _():
            start_gather(pos_next_ref, 1 - s)

    rep = D_MODEL // LANES
    for s in range(2):
        @pl.when(slot == s)
        def _():
            _wait_row_gather(ys_hbm, buf, s, rows, sem)
            y = (jnp.tile(g1_ref[...], (1, rep)) * _from_row_tiles(buf, s * rows, tq)
                 + jnp.tile(g2_ref[...], (1, rep)) * _from_row_tiles(buf, s * rows + tq, tq))
            out_ref[...] = _rms_unit(h_ref[...] + y) * gf_ref[...]


def _moe_combine(ys, h, g1, g2, pos, final_gain, tq, tile_off, n_tiles):
    pos3 = pos[tile_off * tq:(tile_off + n_tiles) * tq].reshape(n_tiles, tq, TOP_K).transpose(0, 2, 1)
    smem_pos = lambda index_map: pl.BlockSpec((1, TOP_K, tq), index_map, memory_space=pltpu.SMEM)
    tok = lambda width: pl.BlockSpec((tq, width), lambda j: (j + tile_off, 0))
    return pl.pallas_call(
        functools.partial(_combine_kernel, tq=tq),
        grid=(n_tiles,),
        in_specs=[
            smem_pos(lambda j: (j, 0, 0)),
            smem_pos(lambda j: (jnp.minimum(j + 1, n_tiles - 1), 0, 0)),
            pl.BlockSpec(memory_space=pl.ANY),
            tok(D_MODEL), tok(LANES), tok(LANES),
            pl.BlockSpec((1, D_MODEL), lambda j: (0, 0)),
        ],
        out_specs=pl.BlockSpec((tq, D_MODEL), lambda j: (j, 0)),
        out_shape=jax.ShapeDtypeStruct((n_tiles * tq, D_MODEL), _F32),
        scratch_shapes=[pltpu.VMEM((2 * TOP_K * tq * ROW_TILE, LANES), _F32), pltpu.SemaphoreType.DMA((2,))],
        compiler_params=_params(("arbitrary",)),
        name="moe_combine",
    )(pos3, pos3, ys, h, g1, g2, final_gain.reshape(1, D_MODEL))


def kernel(x_prompt, x_sample, state_pool, cache_k_win, cache_v_win, norm_mix, norm_ffn, pool_w, pool_scale,
           norm_kv, w_kv, w_q, w_o, w_ffn_gu, w_ffn_down, w_router, w_exp_gu, w_exp_down, norm_final):
    nb, seq, _ = x_prompt.shape
    ns, ts, _ = x_sample.shape
    n_ctx = cache_k_win.shape[1]
    mp, ms = nb * seq, ns * ts
    w_kv_b = w_kv.astype(_BF16)
    w_q_b = w_q[0].astype(_BF16)
    w_o_b = w_o[0].astype(_BF16)
    w_ffn_gu_b = w_ffn_gu[0].astype(_BF16)
    w_ffn_down_b = w_ffn_down[0].astype(_BF16)
    tf_ffn = w_ffn_down_b.shape[0] // 2
    tf_exp = w_exp_down.shape[2] // 2

    dils = tuple(d for _, d in ATTN_GROUPS)
    h, pool_prompt = _pool_layer(x_prompt, None, norm_mix[0], pool_w[0], pool_scale[0], tm=TM_POOL, bn=1)
    h = _ffn_layer(h.reshape(mp, D_MODEL), norm_ffn[0], w_ffn_gu_b, w_ffn_down_b, TM_FFN, tf_ffn)
    keep = min(MAX_WINDOW, seq)
    q, k, v, k_win_p, v_win_p, (w_exp_gu_b, w_exp_down_b) = _qkv_layer(
        h.reshape(nb, seq, D_MODEL), jnp.arange(seq, dtype=jnp.int32), norm_kv, norm_mix[1], w_kv_b, w_q_b,
        tm=TM_QKV, keep=keep, q_dils=dils, cast=(w_exp_gu[0], w_exp_down[0]))
    outs, lses = [], []
    for g, d in enumerate(dils):
        o, lse = _band_attention(q[g], k[d], v[d], g, tb=ATTN_BANDS)
        outs.append(o)
        lses.append(lse)

    hs, pool_sample = _pool_layer(x_sample, state_pool[0], norm_mix[0], pool_w[0], pool_scale[0], tm=ts, bn=8)
    hs = _ffn_layer(hs.reshape(ms, D_MODEL), norm_ffn[0], w_ffn_gu_b, w_ffn_down_b, ms, tf_ffn)
    pos_s = jnp.tile(PAST_LEN + jnp.arange(ts, dtype=jnp.int32), ns)
    qs, ks, vs, ks_f, vs_f, _ = _qkv_layer(hs.reshape(1, ms, D_MODEL), pos_s, norm_kv, norm_mix[1], w_kv_b, w_q_b,
                                           tm=ms, keep=ms, q_dils=(1,) * N_ATTN_GROUPS)
    ckt = cache_k_win.transpose(0, 2, 3, 1).reshape(ns, D_MODEL, n_ctx)
    cvt = cache_v_win.transpose(0, 2, 3, 1).reshape(ns, D_MODEL, n_ctx)
    per_sample = lambda a: a.reshape(ns, ts, D_MODEL)
    os_, kwt, vwt = _cached_attention([per_sample(a) for a in qs], per_sample(ks[1]), per_sample(vs[1]),
                                      per_sample(ks_f), per_sample(vs_f), ckt, cvt, chunk=CACHE_CHUNK)
    h_all, h_tiles, idx_tile, g1, g2 = _merge_proj(outs, lses, h, os_.reshape(ms, D_MODEL), hs, w_o_b, norm_ffn[1],
                                                   w_router[0], tm=TM_TOKEN)

    m_all = mp + ms
    n_tiles = -(-TOP_K * m_all // TM_EXPERT) + N_EXPERTS
    src_rows, tile_expert, n_act, pos = _routing_tables(idx_tile, TM_EXPERT, n_tiles)
    x_sorted = _sort_rows(h_tiles, src_rows, n_act, norm_ffn[1], TM_EXPERT)
    ys = _expert_layer(x_sorted, tile_expert, n_act, w_exp_gu_b, w_exp_down_b, TM_EXPERT, tf_exp)
    y_prompt = _moe_combine(ys, h_all, g1, g2, pos, norm_final, TM_TOKEN, 0, mp // TM_TOKEN)
    y_sample = _moe_combine(ys, h_all, g1, g2, pos, norm_final, TM_TOKEN, mp // TM_TOKEN, ms // TM_TOKEN)

    heads = lambda a: a.reshape(a.shape[0], a.shape[1], N_HEADS, HEAD_DIM)
    heads_t = lambda a: a.reshape(ns, N_HEADS, HEAD_DIM, a.shape[2]).transpose(0, 3, 1, 2)
    return (y_prompt.reshape(nb, seq, D_MODEL), y_sample.reshape(ns, ts, D_MODEL), pool_prompt, pool_sample,
            heads(k_win_p), heads(v_win_p), heads_t(kwt), heads_t(vwt))
```

```python
import functools
import math

import numpy as np
import jax
import jax.numpy as jnp
from jax import lax
from jax.experimental import pallas as pl
from jax.experimental.pallas import tpu as pltpu

D_MODEL = 1024
PAST_LEN = 16384
POOL_WINDOWS = (2, 4, 8, 16)
POOL_GROUP_DIM = D_MODEL // len(POOL_WINDOWS)
POOL_CTX = max(POOL_WINDOWS) - 1
HEAD_DIM = 64
N_HEADS = D_MODEL // HEAD_DIM
ATTN_GROUPS = ((128, 1), (512, 4), (2048, 16))
N_ATTN_GROUPS = len(ATTN_GROUPS)
BAND = 128
MAX_WINDOW = max(w for w, _ in ATTN_GROUPS)
ROT_DIM = HEAD_DIM // 4
ROPE_THETA = 500000.0
ATTN_SCALE = HEAD_DIM ** -0.5
N_EXPERTS = 8
TOP_K = 2
RMS_EPS = 1e-5

LANES = 128
HALO = 16
MASK_VALUE = -1e30
M_INIT = -1e20
VMEM_LIMIT = 56 * 1024 * 1024

TM_POOL = 512
TM_FFN = 512
TM_QKV = 256
TM_TOKEN = 256
TM_EXPERT = 512
MXU_COLS = 256
ROW_TILE = 8
ATTN_BANDS = 2
CACHE_CHUNK = 512

_F32 = jnp.float32
_BF16 = jnp.bfloat16
_NT = (((1,), (1,)), ((), ()))


def _params(semantics):
    return pltpu.CompilerParams(dimension_semantics=semantics, vmem_limit_bytes=VMEM_LIMIT)


def _rms_unit(x):
    return x * lax.rsqrt(jnp.mean(x * x, axis=-1, keepdims=True) + RMS_EPS)


def _dot(a, b):
    return jnp.dot(a, b, preferred_element_type=_F32)


def _swiglu_chunk(u, wg, wu, wd):
    g = _dot(u, wg)
    up = _dot(u, wu)
    return _dot((g * jax.nn.sigmoid(g) * up).astype(_BF16), wd)


def _pool_kernel(*refs, tm, n_ctx, bn):
    if n_ctx:
        x_ref, ctx_ref, g_ref, w_ref, sc_ref, h_ref, pool_ref, ext_ref = refs
    else:
        x_ref, g_ref, w_ref, sc_ref, h_ref, pool_ref, ext_ref = refs
        ctx_ref = None
    s = pl.program_id(1)
    pos = n_ctx + s * tm + lax.broadcasted_iota(jnp.int32, (tm, 1), 0)
    for b in range(bn):
        x = x_ref[b]
        u = _rms_unit(x) * g_ref[...]

        @pl.when(s == 0)
        def _():
            if n_ctx:
                ext_ref[b, 0:HALO - n_ctx, :] = jnp.zeros((HALO - n_ctx, D_MODEL), _F32)
                ext_ref[b, HALO - n_ctx:HALO, :] = ctx_ref[b]
            else:
                ext_ref[b, 0:HALO, :] = jnp.zeros((HALO, D_MODEL), _F32)

        ext_ref[b, HALO:HALO + tm, :] = u
        for g, w in enumerate(POOL_WINDOWS):
            sl = slice(g * POOL_GROUP_DIM, (g + 1) * POOL_GROUP_DIM)
            acc = u[:, sl]
            for k in range(1, w):
                acc = acc + ext_ref[b, HALO - k:HALO - k + tm, sl]
            cnt = jnp.minimum(pos + 1, w).astype(_F32)
            d = acc / cnt - u[:, sl]
            y = _dot(d.astype(_BF16), w_ref[g])
            h_ref[b, :, sl] = x[:, sl] + y * sc_ref[:, sl]

        @pl.when(s == pl.num_programs(1) - 1)
        def _():
            pool_ref[0, b] = ext_ref[b, HALO + tm - POOL_CTX:HALO + tm, :]

        ext_ref[b, 0:HALO, :] = ext_ref[b, tm:tm + HALO, :]


def _pool_layer(x, ctx, gain, w_pool, scale, tm, bn):
    n, t, _ = x.shape
    n_ctx = 0 if ctx is None else POOL_CTX
    grid = (n // bn, t // tm)
    row = lambda i, s: (i, s, 0)
    const2 = lambda i, s: (0, 0)
    in_specs = [pl.BlockSpec((bn, tm, D_MODEL), row)]
    args = [x]
    if ctx is not None:
        in_specs.append(pl.BlockSpec((bn, POOL_CTX, D_MODEL), lambda i, s: (i, 0, 0)))
        args.append(ctx)
    in_specs += [
        pl.BlockSpec((1, D_MODEL), const2),
        pl.BlockSpec((len(POOL_WINDOWS), POOL_GROUP_DIM, POOL_GROUP_DIM), lambda i, s: (0, 0, 0)),
        pl.BlockSpec((1, D_MODEL), const2),
    ]
    args += [gain.reshape(1, D_MODEL), w_pool.astype(_BF16), scale.reshape(1, D_MODEL)]
    return pl.pallas_call(
        functools.partial(_pool_kernel, tm=tm, n_ctx=n_ctx, bn=bn),
        grid=grid,
        in_specs=in_specs,
        out_specs=[
            pl.BlockSpec((bn, tm, D_MODEL), row),
            pl.BlockSpec((1, bn, POOL_CTX, D_MODEL), lambda i, s: (0, i, 0, 0)),
        ],
        out_shape=[
            jax.ShapeDtypeStruct((n, t, D_MODEL), _F32),
            jax.ShapeDtypeStruct((1, n, POOL_CTX, D_MODEL), _F32),
        ],
        scratch_shapes=[pltpu.VMEM((bn, HALO + tm, D_MODEL), _F32)],
        compiler_params=_params(("parallel", "arbitrary")),
        name="pool_layer",
    )(*args)


def _ffn_kernel(h_ref, g_ref, wg_ref, wu_ref, wd_ref, out_ref, u_sc, acc_sc):
    c = pl.program_id(1)

    @pl.when(c == 0)
    def _():
        u_sc[...] = (_rms_unit(h_ref[...]) * g_ref[...]).astype(_BF16)
        acc_sc[...] = jnp.zeros_like(acc_sc)

    acc_sc[...] += _swiglu_chunk(u_sc[...], wg_ref[...], wu_ref[...], wd_ref[...])

    @pl.when(c == pl.num_programs(1) - 1)
    def _():
        out_ref[...] = h_ref[...] + acc_sc[...]


def _ffn_layer(h, gain, w_gu, w_down, tm, tf):
    m = h.shape[0]
    f = w_down.shape[0]
    nc = f // tf
    row = lambda i, c: (i, 0)
    return pl.pallas_call(
        _ffn_kernel,
        grid=(m // tm, nc),
        in_specs=[
            pl.BlockSpec((tm, D_MODEL), row),
            pl.BlockSpec((1, D_MODEL), lambda i, c: (0, 0)),
            pl.BlockSpec((D_MODEL, tf), lambda i, c: (0, c)),
            pl.BlockSpec((D_MODEL, tf), lambda i, c: (0, nc + c)),
            pl.BlockSpec((tf, D_MODEL), lambda i, c: (c, 0)),
        ],
        out_specs=pl.BlockSpec((tm, D_MODEL), row),
        out_shape=jax.ShapeDtypeStruct((m, D_MODEL), _F32),
        scratch_shapes=[pltpu.VMEM((tm, D_MODEL), _BF16), pltpu.VMEM((tm, D_MODEL), _F32)],
        compiler_params=_params(("parallel", "arbitrary")),
        name="swiglu_dense",
    )(h, gain.reshape(1, D_MODEL), w_gu, w_gu, w_down)


def _rope_tables(pos):
    half = ROT_DIM // 2
    inv_freq = jnp.exp(-math.log(ROPE_THETA) * jnp.arange(half, dtype=_F32) / half)
    ang = pos.astype(_F32)[:, None] * inv_freq[None, :]
    cos, sin = jnp.cos(ang), jnp.sin(ang)
    t = pos.shape[0]
    rest = HEAD_DIM - ROT_DIM
    zero_h = jnp.zeros((t, half), _F32)
    zero_r = jnp.zeros((t, rest), _F32)
    c = jnp.concatenate([cos, cos, jnp.ones((t, rest), _F32)], axis=1)
    s1 = jnp.concatenate([zero_h, sin, zero_r], axis=1)
    s2 = jnp.concatenate([-sin, zero_h, zero_r], axis=1)
    rep = LANES // HEAD_DIM
    return tuple(jnp.tile(a, (1, rep)) for a in (c, s1, s2))


def _store_by_class(t, j, scr, slot, targets, tm):
    sl = slice(j * LANES, (j + 1) * LANES)
    if any(d > 1 for d, _ in targets):
        scr[slot] = t
    for d, ref in targets:
        if d == 1:
            ref[0, :, sl] = t.astype(_BF16)
        else:
            for r in range(d):
                ref[r, :, sl] = scr[slot, pl.ds(r, tm // d, stride=d), :].astype(_BF16)


def _qkv_kernel(*refs, q_dils, kv_dils, tm, n_cast):
    h_ref, gkv_ref, gq_ref, wkv_ref, wq_ref, c_ref, s1_ref, s2_ref = refs[:8]
    rest = list(refs[8:])
    cast_in = [rest.pop(0) for _ in range(n_cast)]
    q_refs = [rest.pop(0) for _ in q_dils]
    k_refs = [rest.pop(0) for _ in kv_dils]
    v_refs = [rest.pop(0) for _ in kv_dils]
    kf_ref, vf_ref = rest.pop(0), rest.pop(0)
    cast_out = [rest.pop(0) for _ in range(n_cast)]
    q_scr, k_scr, v_scr = rest
    hn = _rms_unit(h_ref[...])
    ukv = (hn * gkv_ref[...]).astype(_BF16)
    uq = (hn * gq_ref[...]).astype(_BF16)
    c, s1, s2 = c_ref[...], s1_ref[...], s2_ref[...]
    half = ROT_DIM // 2
    tiles = D_MODEL // LANES
    cols = MXU_COLS // LANES

    def rope(t):
        return t * c + pltpu.roll(t, half, 1) * s1 + pltpu.roll(t, LANES - half, 1) * s2

    for jb in range(2 * tiles // cols):
        blk = _dot(ukv, wkv_ref[:, jb * MXU_COLS:(jb + 1) * MXU_COLS])
        for jj in range(cols):
            j = jb * cols + jj
            t = blk[:, jj * LANES:(jj + 1) * LANES]
            if j < tiles:
                t = rope(t)
                kf_ref[:, j * LANES:(j + 1) * LANES] = t
                _store_by_class(t, j, k_scr, j, list(zip(kv_dils, k_refs)), tm)
            else:
                j -= tiles
                vf_ref[:, j * LANES:(j + 1) * LANES] = t
                _store_by_class(t, j, v_scr, j, list(zip(kv_dils, v_refs)), tm)
    for g, d in enumerate(q_dils):
        for jb in range(tiles // cols):
            c0 = g * D_MODEL + jb * MXU_COLS
            blk = _dot(uq, wq_ref[:, c0:c0 + MXU_COLS])
            for jj in range(cols):
                j = jb * cols + jj
                t = rope(blk[:, jj * LANES:(jj + 1) * LANES]) * ATTN_SCALE
                _store_by_class(t, j, q_scr, g * tiles + j, [(d, q_refs[g])], tm)
    for src, dst in zip(cast_in, cast_out):
        dst[...] = src[...].astype(_BF16)


def _qkv_layer(h, pos, norm_kv, norm_q, w_kv, w_q, tm, keep, q_dils, cast=()):
    n, t, _ = h.shape
    off = (t - keep) // tm
    kv_dils = tuple(sorted(set(q_dils)))
    tables = _rope_tables(pos)
    const2 = lambda i, s: (0, 0)
    tab = pl.BlockSpec((tm, LANES), lambda i, s: (s, 0))
    win = lambda i, s: (i, jnp.maximum(s - off, 0), 0)
    by_class = lambda d: pl.BlockSpec((None, d, tm // d, D_MODEL), lambda i, s: (i, 0, s, 0))
    class_shape = lambda d: jax.ShapeDtypeStruct((n, d, t // d, D_MODEL), _BF16)
    out_dils = tuple(q_dils) + kv_dils + kv_dils
    tiles = D_MODEL // LANES
    per_n = t // tm
    steps = n * per_n

    def slab(a):
        e, r, c = a.shape
        per_e = steps // e
        assert steps % e == 0 and r % per_e == 0 and (r // per_e) % 16 == 0
        return pl.BlockSpec((None, r // per_e, c),
                            lambda i, s: ((i * per_n + s) // per_e, (i * per_n + s) % per_e, 0))

    res = pl.pallas_call(
        functools.partial(_qkv_kernel, q_dils=tuple(q_dils), kv_dils=kv_dils, tm=tm, n_cast=len(cast)),
        grid=(n, per_n),
        in_specs=[
            pl.BlockSpec((None, tm, D_MODEL), lambda i, s: (i, s, 0)),
            pl.BlockSpec((1, D_MODEL), const2),
            pl.BlockSpec((1, D_MODEL), const2),
            pl.BlockSpec((D_MODEL, 2 * D_MODEL), const2, pipeline_mode=pl.Buffered(1)),
            pl.BlockSpec((D_MODEL, len(q_dils) * D_MODEL), const2, pipeline_mode=pl.Buffered(1)),
            tab, tab, tab,
        ] + [slab(a) for a in cast],
        out_specs=([by_class(d) for d in out_dils] + [pl.BlockSpec((None, tm, D_MODEL), win)] * 2
                   + [slab(a) for a in cast]),
        out_shape=([class_shape(d) for d in out_dils] + [jax.ShapeDtypeStruct((n, keep, D_MODEL), _F32)] * 2
                   + [jax.ShapeDtypeStruct(a.shape, _BF16) for a in cast]),
        scratch_shapes=[
            pltpu.VMEM((len(q_dils) * tiles, tm, LANES), _F32),
            pltpu.VMEM((tiles, tm, LANES), _F32),
            pltpu.VMEM((tiles, tm, LANES), _F32),
        ],
        compiler_params=_params(("parallel", "arbitrary")),
        name="qkv_rope",
    )(h, norm_kv.reshape(1, D_MODEL), norm_q.reshape(1, D_MODEL), w_kv, w_q, *tables, *cast)
    ng, nk = len(q_dils), len(kv_dils)
    qs = res[:ng]
    ks = dict(zip(kv_dils, res[ng:ng + nk]))
    vs = dict(zip(kv_dils, res[ng + nk:ng + 2 * nk]))
    kf, vf = res[ng + 2 * nk], res[ng + 2 * nk + 1]
    return qs, ks, vs, kf, vf, tuple(res[ng + 2 * nk + 2:])


def _band_attn_kernel(q_ref, kh_ref, k_ref, vh_ref, v_ref, o_ref, lse_ref, kx_ref, vx_ref, *, tb):
    m_id = pl.program_id(2)
    kx_ref[0:BAND, :] = kh_ref[...]
    kx_ref[BAND:, :] = k_ref[...]
    vx_ref[0:BAND, :] = vh_ref[...]
    vx_ref[BAND:, :] = v_ref[...]
    row = lax.broadcasted_iota(jnp.int32, (BAND, 2 * BAND), 0)
    col = lax.broadcasted_iota(jnp.int32, (BAND, 2 * BAND), 1)
    dist = row - col + BAND
    bias = jnp.where(dist >= 0, jnp.where(dist <= BAND, 0.0, MASK_VALUE), MASK_VALUE)
    no_prev = jnp.where(m_id > 0, 0.0, MASK_VALUE)
    bias_first = jnp.where(col < BAND, bias + no_prev, bias)
    lane = lax.broadcasted_iota(jnp.int32, (BAND, LANES), 1)
    low = lane < HEAD_DIM
    for b in range(tb):
        bb = bias_first if b == 0 else bias
        rows = slice(b * BAND, (b + 1) * BAND)
        lse_ref[rows, N_HEADS:] = jnp.zeros((BAND, LANES - N_HEADS), _F32)
        for p in range(D_MODEL // LANES):
            sl = slice(p * LANES, (p + 1) * LANES)
            q2 = q_ref[rows, sl].astype(_F32)
            k2 = kx_ref[b * BAND:(b + 2) * BAND, sl]
            v2 = vx_ref[b * BAND:(b + 2) * BAND, sl]
            outs = []
            for hh in range(2):
                qm = (jnp.where(low, q2, 0.0) if hh == 0 else jnp.where(low, 0.0, q2)).astype(_BF16)
                s = lax.dot_general(qm, k2, _NT, preferred_element_type=_F32) + bb
                mx = jnp.max(s, axis=1, keepdims=True)
                pr = jnp.exp(s - mx)
                den = jnp.sum(pr, axis=1, keepdims=True)
                o = _dot(pr.astype(_BF16), v2)
                outs.append(o * (1.0 / den))
                head = 2 * p + hh
                lse_ref[rows, head:head + 1] = mx + jnp.log(den)
            o_ref[rows, sl] = jnp.where(low, outs[0], outs[1]).astype(_BF16)


def _band_attention(q, k, v, group, tb):
    n, dil, length, _ = k.shape
    rows = tb * BAND
    cur = lambda i, r, m: (i, r, m, 0)
    halo = lambda i, r, m: (i, r, jnp.maximum(m * tb - 1, 0), 0)
    band = pl.BlockSpec((None, None, rows, D_MODEL), cur)
    prev = pl.BlockSpec((None, None, BAND, D_MODEL), halo)
    return pl.pallas_call(
        functools.partial(_band_attn_kernel, tb=tb),
        grid=(n, dil, length // rows),
        in_specs=[band, prev, band, prev, band],
        out_specs=[band, pl.BlockSpec((None, None, rows, LANES), cur)],
        out_shape=[
            jax.ShapeDtypeStruct((n, dil, length, D_MODEL), _BF16),
            jax.ShapeDtypeStruct((n, dil, length, LANES), _F32),
        ],
        scratch_shapes=[pltpu.VMEM((BAND + rows, D_MODEL), _BF16), pltpu.VMEM((BAND + rows, D_MODEL), _BF16)],
        compiler_params=_params(("parallel", "parallel", "arbitrary")),
        name="band_attn_g%d" % group,
    )(q, k, k, v, v)


def _head_expand_matrix():
    e = np.zeros((LANES, N_ATTN_GROUPS * D_MODEL), np.float32)
    for part in range(2):
        for g in range(N_ATTN_GROUPS):
            for h in range(N_HEADS):
                r = part * N_ATTN_GROUPS * N_HEADS + g * N_HEADS + h
                e[r, g * D_MODEL + h * HEAD_DIM:g * D_MODEL + (h + 1) * HEAD_DIM] = 1.0
    return jnp.asarray(e, _BF16)


def _merge_proj_kernel(o0_ref, o1_ref, o2_ref, l0_ref, l1_ref, l2_ref, e_ref, h_ref, os_ref, hs_ref, wo_ref,
                       gr_ref, wrh_ref, wrl_ref, out_ref, ht_ref, idx_ref, g1_ref, g2_ref,
                       o1_scr, o2_scr, l1_scr, l2_scr, o_scr, *, n_merge, tm):
    i = pl.program_id(0)
    tiles = D_MODEL // LANES

    @pl.when(i < n_merge)
    def _():
        for o_ref, l_ref, o_dst, l_dst in ((o1_ref, l1_ref, o1_scr, l1_scr), (o2_ref, l2_ref, o2_scr, l2_scr)):
            d = o_ref.shape[0]
            for r in range(d):
                l_dst[pl.ds(r, tm // d, stride=d), :] = l_ref[r]
                for j in range(tiles):
                    o_dst[j, pl.ds(r, tm // d, stride=d), :] = o_ref[r, :, j * LANES:(j + 1) * LANES].astype(_F32)
        l0, l1, l2 = l0_ref[0], l1_scr[...], l2_scr[...]
        mx = jnp.maximum(jnp.maximum(l0, l1), l2)
        e0, e1, e2 = jnp.exp(l0 - mx), jnp.exp(l1 - mx), jnp.exp(l2 - mx)
        inv = 1.0 / (e0 + e1 + e2)
        lane = lax.broadcasted_iota(jnp.int32, l0.shape, 1)
        head = lane < N_HEADS
        a = (jnp.where(head, e0 * inv, 0.0)
             + pltpu.roll(jnp.where(head, e1 * inv, 0.0), N_HEADS, 1)
             + pltpu.roll(jnp.where(head, e2 * inv, 0.0), 2 * N_HEADS, 1))
        hi = a.astype(_BF16).astype(_F32)
        a2 = hi + pltpu.roll(a - hi, N_ATTN_GROUPS * N_HEADS, 1)
        w = _dot(a2.astype(_BF16), e_ref[...])
        for j in range(tiles):
            sl = slice(j * LANES, (j + 1) * LANES)
            o = (w[:, sl] * o0_ref[0, :, sl].astype(_F32)
                 + w[:, D_MODEL + j * LANES:D_MODEL + (j + 1) * LANES] * o1_scr[j]
                 + w[:, 2 * D_MODEL + j * LANES:2 * D_MODEL + (j + 1) * LANES] * o2_scr[j])
            o_scr[:, sl] = o.astype(_BF16)
        out_ref[...] = h_ref[...] + _dot(o_scr[...], wo_ref[...])

    @pl.when(i >= n_merge)
    def _():
        out_ref[...] = hs_ref[...] + _dot(os_ref[...].astype(_BF16), wo_ref[...])

    hn = out_ref[...]
    _to_row_tiles(hn, ht_ref)
    _route(hn, gr_ref[...], wrh_ref[...], wrl_ref[...], idx_ref, g1_ref, g2_ref)


def _merge_proj(outs, lses, h, o_sample, h_sample, w_o, route_gain, w_router, tm):
    m, ms = h.shape[0], h_sample.shape[0]
    n, _, seq, _ = outs[0].shape
    n_merge = m // tm
    per_n = seq // tm
    row = lambda i: (jnp.minimum(i, n_merge - 1), 0)
    srow = lambda i: (jnp.maximum(i - n_merge, 0), 0)
    const = lambda i: (0, 0)

    def by_class(a):
        d, width = a.shape[1], a.shape[3]

        def index(i):
            ic = jnp.minimum(i, n_merge - 1)
            return (ic // per_n, 0, ic % per_n, 0)

        return pl.BlockSpec((None, d, tm // d, width), index)

    tiles = D_MODEL // LANES
    wr = jnp.zeros((D_MODEL, LANES), _F32).at[:, :N_EXPERTS].set(w_router)
    wr_hi = wr.astype(_BF16)
    wr_lo = (wr - wr_hi.astype(_F32)).astype(_BF16)
    tile_out = lambda width: pl.BlockSpec((tm, width), lambda i: (i, 0))
    tile_shape = lambda width: jax.ShapeDtypeStruct((m + ms, width), _F32)
    return pl.pallas_call(
        functools.partial(_merge_proj_kernel, n_merge=n_merge, tm=tm),
        grid=((m + ms) // tm,),
        in_specs=[by_class(a) for a in outs] + [by_class(a) for a in lses] + [
            pl.BlockSpec((LANES, N_ATTN_GROUPS * D_MODEL), const),
            pl.BlockSpec((tm, D_MODEL), row),
            pl.BlockSpec((tm, D_MODEL), srow),
            pl.BlockSpec((tm, D_MODEL), srow),
            pl.BlockSpec((D_MODEL, D_MODEL), const),
            pl.BlockSpec((1, D_MODEL), const),
            pl.BlockSpec((D_MODEL, LANES), const),
            pl.BlockSpec((D_MODEL, LANES), const),
        ],
        out_specs=[tile_out(D_MODEL), pl.BlockSpec((tm * ROW_TILE, LANES), lambda i: (i, 0)),
                   tile_out(LANES), tile_out(LANES), tile_out(LANES)],
        out_shape=[tile_shape(D_MODEL), jax.ShapeDtypeStruct(((m + ms) * ROW_TILE, LANES), _F32),
                   tile_shape(LANES), tile_shape(LANES), tile_shape(LANES)],
        scratch_shapes=[
            pltpu.VMEM((tiles, tm, LANES), _F32),
            pltpu.VMEM((tiles, tm, LANES), _F32),
            pltpu.VMEM((tm, LANES), _F32),
            pltpu.VMEM((tm, LANES), _F32),
            pltpu.VMEM((tm, D_MODEL), _BF16),
        ],
        compiler_params=_params(("parallel",)),
        name="merge_out_proj",
    )(*outs, *lses, _head_expand_matrix(), h, o_sample, h_sample, w_o, route_gain.reshape(1, D_MODEL), wr_hi, wr_lo)


def _cached_bias(t, n_ctx):
    rows = N_ATTN_GROUPS * N_HEADS * t
    cache = np.full((rows, n_ctx), MASK_VALUE, np.float32)
    new = np.full((rows, LANES), MASK_VALUE, np.float32)
    for g, (win, dil) in enumerate(ATTN_GROUPS):
        for j in range(t):
            idx = n_ctx + j - np.arange(win // dil + 1) * dil
            idx = idx[idx >= 0]
            for h in range(N_HEADS):
                r = (g * N_HEADS + h) * t + j
                cache[r, idx[idx < n_ctx]] = 0.0
                new[r, idx[idx >= n_ctx] - n_ctx] = 0.0
    return cache, new


def _cached_attn_kernel(q0_ref, q1_ref, q2_ref, kc_ref, vc_ref, kx_ref, vx_ref, kn_ref, vn_ref, knt_ref, vnt_ref,
                        bias_ref, biasn_ref, o_ref, kw_ref, vw_ref, qbd_ref, m_ref, l_ref, acc_ref,
                        *, t, first_chunk):
    c = pl.program_id(1)
    last = pl.num_programs(1) - 1
    gh = N_HEADS * t
    row = lax.broadcasted_iota(jnp.int32, (gh, D_MODEL), 0)
    lane = lax.broadcasted_iota(jnp.int32, (gh, D_MODEL), 1)
    own = jnp.right_shift(row, int(math.log2(t))) == jnp.right_shift(lane, int(math.log2(HEAD_DIM)))

    @pl.when(c == 0)
    def _():
        for g, q_ref in enumerate((q0_ref, q1_ref, q2_ref)):
            tiled = jnp.concatenate([q_ref[...].astype(_F32)] * N_HEADS, axis=0)
            qbd_ref[g * gh:(g + 1) * gh, :] = jnp.where(own, tiled, 0.0).astype(_BF16)
        m_ref[...] = jnp.full(m_ref.shape, M_INIT, _F32)
        l_ref[...] = jnp.zeros_like(l_ref)
        acc_ref[...] = jnp.zeros_like(acc_ref)

    def update(rows, s, weighted_values):
        m_old = m_ref[rows, :]
        m_new = jnp.maximum(m_old, jnp.max(s, axis=1, keepdims=True))
        alpha = jnp.exp(m_old - m_new)
        pr = jnp.exp(s - m_new)
        l_ref[rows, :] = alpha * l_ref[rows, :] + jnp.sum(pr, axis=1, keepdims=True)
        acc_ref[rows, :] = alpha * acc_ref[rows, :] + weighted_values(pr.astype(_BF16))
        m_ref[rows, :] = m_new

    kc, vc = kc_ref[...], vc_ref[...]
    kct, vct = kc.astype(_BF16), vc.astype(_BF16)
    for g in range(N_ATTN_GROUPS):
        rows = slice(g * gh, (g + 1) * gh)

        @pl.when(c >= first_chunk[g])
        def _():
            update(rows, _dot(qbd_ref[rows, :], kct) + bias_ref[rows, :],
                   lambda pr: lax.dot_general(pr, vct, _NT, preferred_element_type=_F32))

    chunk = kc.shape[1]
    for cur, nxt, new_t, w_ref in ((kc, kx_ref, knt_ref, kw_ref), (vc, vx_ref, vnt_ref, vw_ref)):
        tail = jnp.where(c == last, new_t[...], nxt[...])
        ext = jnp.concatenate([cur, tail], axis=1)
        w_ref[...] = pltpu.roll(ext, chunk + LANES - t, 1)[:, 0:chunk]

    @pl.when(c == last)
    def _():
        pad = jnp.zeros((LANES - t, D_MODEL), _F32)
        kn = jnp.concatenate([kn_ref[...].astype(_F32), pad], axis=0).astype(_BF16)
        vn = jnp.concatenate([vn_ref[...].astype(_F32), pad], axis=0).astype(_BF16)
        update(slice(0, N_ATTN_GROUPS * gh),
               lax.dot_general(qbd_ref[...], kn, _NT, preferred_element_type=_F32) + biasn_ref[...],
               lambda pr: _dot(pr, vn))
        ms = [m_ref[g * gh:(g + 1) * gh, :] for g in range(N_ATTN_GROUPS)]
        mx = jnp.maximum(jnp.maximum(ms[0], ms[1]), ms[2])
        num = jnp.zeros((gh, D_MODEL), _F32)
        den = jnp.zeros((gh, 1), _F32)
        for g in range(N_ATTN_GROUPS):
            w = jnp.exp(ms[g] - mx)
            num = num + w * acc_ref[g * gh:(g + 1) * gh, :]
            den = den + w * l_ref[g * gh:(g + 1) * gh, :]
        on = jnp.where(own, num * (1.0 / den), 0.0)
        o = on[0:t, :]
        for h in range(1, N_HEADS):
            o = o + on[h * t:(h + 1) * t, :]
        o_ref[...] = o


def _cached_attention(qs, k_new, v_new, k_new_f32, v_new_f32, cache_kt, cache_vt, chunk):
    n, t, _ = k_new.shape
    n_ctx = cache_kt.shape[2]
    assert n_ctx == MAX_WINDOW and n_ctx % chunk == 0 and chunk % LANES == 0 and t <= LANES
    bias_np, bias_new = _cached_bias(t, n_ctx)
    gh = N_HEADS * t
    rows = N_ATTN_GROUPS * gh
    n_chunks = n_ctx // chunk
    first_chunk = tuple(
        min(c for c in range(n_chunks) if (bias_np[g * gh:(g + 1) * gh, c * chunk:(c + 1) * chunk] == 0.0).any())
        for g in range(N_ATTN_GROUPS))
    new_t = lambda a: jnp.pad(a.transpose(0, 2, 1), ((0, 0), (0, 0), (0, LANES - t)))
    per_n = lambda i, c: (i, 0, 0)
    cur = pl.BlockSpec((None, D_MODEL, chunk), lambda i, c: (i, 0, c))
    ahead = pl.BlockSpec((None, D_MODEL, LANES),
                         lambda i, c: (i, 0, jnp.minimum((c + 1) * (chunk // LANES), n_ctx // LANES - 1)))
    small = pl.BlockSpec((None, t, D_MODEL), per_n)
    small_t = pl.BlockSpec((None, D_MODEL, LANES), per_n)
    window = jax.ShapeDtypeStruct((n, D_MODEL, n_ctx), _F32)
    return pl.pallas_call(
        functools.partial(_cached_attn_kernel, t=t, first_chunk=first_chunk),
        grid=(n, n_chunks),
        in_specs=[
            small, small, small,
            cur, cur, ahead, ahead,
            small, small, small_t, small_t,
            pl.BlockSpec((rows, chunk), lambda i, c: (0, c)),
            pl.BlockSpec((rows, LANES), lambda i, c: (0, 0)),
        ],
        out_specs=[small, cur, cur],
        out_shape=[jax.ShapeDtypeStruct((n, t, D_MODEL), _F32), window, window],
        scratch_shapes=[
            pltpu.VMEM((rows, D_MODEL), _BF16),
            pltpu.VMEM((rows, 1), _F32),
            pltpu.VMEM((rows, 1), _F32),
            pltpu.VMEM((rows, D_MODEL), _F32),
        ],
        compiler_params=_params(("parallel", "arbitrary")),
        name="cached_attn",
    )(*qs, cache_kt, cache_vt, cache_kt, cache_vt, k_new, v_new, new_t(k_new_f32), new_t(v_new_f32),
      jnp.asarray(bias_np), jnp.asarray(bias_new))


def _split_bf16(x):
    hi = x.astype(_BF16)
    return hi, (x - hi.astype(_F32)).astype(_BF16)


def _route(h, gain, wr_hi, wr_lo, idx_ref, g1_ref, g2_ref):
    u_hi, u_lo = _split_bf16(_rms_unit(h) * gain)
    logits = _dot(u_hi, wr_hi) + (_dot(u_hi, wr_lo) + _dot(u_lo, wr_hi))
    lane = lax.broadcasted_iota(jnp.int32, logits.shape, 1).astype(_F32)
    neg = -jnp.inf
    lg = jnp.where(lane < N_EXPERTS, logits, neg)
    v1 = jnp.max(lg, axis=1, keepdims=True)
    i1 = jnp.min(jnp.where(lg == v1, lane, float(LANES)), axis=1, keepdims=True)
    lg2 = jnp.where(lane == i1, neg, lg)
    v2 = jnp.max(lg2, axis=1, keepdims=True)
    i2 = jnp.min(jnp.where(lg2 == v2, lane, float(LANES)), axis=1, keepdims=True)
    e2 = jnp.exp(v2 - v1)
    den = 1.0 + e2
    idx_ref[...] = jnp.where(lane < LANES // 2, i1, i2)
    g1_ref[...] = jnp.broadcast_to(1.0 / den, g1_ref.shape)
    g2_ref[...] = jnp.broadcast_to(e2 / den, g2_ref.shape)


def _routing_tables(idx_tile, tm):
    m = idx_tile.shape[0]
    n_rows = TOP_K * m
    assert n_rows % tm == 0
    n_tiles = n_rows // tm
    n_visits = n_tiles + N_EXPERTS - 1
    e = jnp.stack([idx_tile[:, 0], idx_tile[:, LANES // 2]], axis=1).reshape(n_rows).astype(jnp.int32)
    onehot = (e[:, None] == jnp.arange(N_EXPERTS, dtype=jnp.int32)[None, :]).astype(jnp.int32)
    running = jnp.cumsum(onehot, axis=0)
    counts = running[-1]
    rank = jnp.sum(onehot * running, axis=1) - 1
    ends = jnp.cumsum(counts)
    starts = ends - counts
    pos = jnp.sum(onehot * starts[None, :], axis=1) + rank
    tile_lo = jnp.arange(n_tiles, dtype=jnp.int32)[:, None] * tm
    lo = jnp.maximum(starts[None, :], tile_lo) - tile_lo
    hi = jnp.minimum(ends[None, :], tile_lo + tm) - tile_lo
    used = (hi > lo).reshape(-1)
    n_used = jnp.sum(used.astype(jnp.int32))
    flat = jnp.nonzero(used, size=n_visits, fill_value=0)[0].astype(jnp.int32)
    flat = jnp.where(jnp.arange(n_visits) < n_used, flat, jnp.take(flat, n_used - 1))
    live = jnp.arange(n_visits) < n_used
    v_lo = jnp.where(live, jnp.take(lo.reshape(-1), flat), 0)
    v_hi = jnp.where(live, jnp.take(hi.reshape(-1), flat), 0)
    return flat // N_EXPERTS, flat % N_EXPERTS, v_lo, v_hi, n_used.reshape(1), pos.reshape(m, TOP_K)


def _to_row_tiles(x, dst_ref):
    rows = x.shape[0]
    for j in range(D_MODEL // LANES):
        dst_ref[pl.ds(j, rows, stride=ROW_TILE), :] = x[:, j * LANES:(j + 1) * LANES]


def _from_row_tiles(src_ref, base, rows):
    return jnp.concatenate(
        [src_ref[pl.ds(base * ROW_TILE + j, rows, stride=ROW_TILE), :] for j in range(D_MODEL // LANES)], axis=1)


def _start_row_gather(src_hbm, index, buf, slot, r, rows, sem, queue):
    dst = (slot * rows + r) * ROW_TILE
    pltpu.make_async_copy(src_hbm.at[pl.ds(index * ROW_TILE, ROW_TILE)], buf.at[pl.ds(dst, ROW_TILE)],
                          sem.at[slot]).start(priority=queue)


def _wait_row_gather(src_hbm, buf, slot, rows, sem):
    n = rows * ROW_TILE
    pltpu.make_async_copy(src_hbm.at[pl.ds(0, n)], buf.at[pl.ds(slot * n, n)], sem.at[slot]).wait()


def _push_rows_kernel(pos_ref, h_hbm, xs_hbm, sem, *, tq):
    j = pl.program_id(0)
    slot = lax.rem(j, 2)
    n = TOP_K * tq * ROW_TILE

    def wait_tile(s):
        pltpu.make_async_copy(h_hbm.at[pl.ds(0, n)], xs_hbm.at[pl.ds(0, n)], sem.at[s]).wait()

    for r in range(tq):
        src = h_hbm.at[pl.ds((j * tq + r) * ROW_TILE, ROW_TILE)]
        for k in range(TOP_K):
            pltpu.make_async_copy(src, xs_hbm.at[pl.ds(pos_ref[0, k, r] * ROW_TILE, ROW_TILE)],
                                  sem.at[slot]).start(priority=k)

    @pl.when(j > 0)
    def _():
        wait_tile(1 - slot)

    @pl.when(j == pl.num_programs(0) - 1)
    def _():
        wait_tile(slot)


def _push_rows(h_tiles, pos, tq):
    m = pos.shape[0]
    n_tiles = m // tq
    pos3 = pos.reshape(n_tiles, tq, TOP_K).transpose(0, 2, 1)
    return pl.pallas_call(
        functools.partial(_push_rows_kernel, tq=tq),
        grid=(n_tiles,),
        in_specs=[
            pl.BlockSpec((1, TOP_K, tq), lambda j: (j, 0, 0), memory_space=pltpu.SMEM),
            pl.BlockSpec(memory_space=pl.ANY),
        ],
        out_specs=pl.BlockSpec(memory_space=pl.ANY),
        out_shape=jax.ShapeDtypeStruct((TOP_K * m * ROW_TILE, LANES), _F32),
        scratch_shapes=[pltpu.SemaphoreType.DMA((2,))],
        compiler_params=_params(("arbitrary",)),
        name="push_rows",
    )(pos3, h_tiles)


def _expert_kernel(vt_ref, ve_ref, lo_ref, hi_ref, nv_ref, x_ref, g_ref, wg_ref, wu_ref, wd_ref, out_ref,
                   u_sc, acc_sc, *, tm):
    del ve_ref
    v = pl.program_id(0)
    c = pl.program_id(1)
    active = v < nv_ref[0]
    first = jnp.logical_or(v == 0, vt_ref[v] != vt_ref[jnp.maximum(v - 1, 0)])

    @pl.when(active)
    def _():
        @pl.when(c == 0)
        def _():
            u_sc[...] = (_rms_unit(_from_row_tiles(x_ref, 0, tm)) * g_ref[...]).astype(_BF16)

        y = _swiglu_chunk(u_sc[...], wg_ref[...], wu_ref[...], wd_ref[...])

        @pl.when(c == 0)
        def _():
            acc_sc[...] = y

        @pl.when(c > 0)
        def _():
            acc_sc[...] += y

        @pl.when(c == pl.num_programs(1) - 1)
        def _():
            row = lax.broadcasted_iota(jnp.int32, (tm, 1), 0)
            mine = jnp.where((row >= lo_ref[v]) & (row < hi_ref[v]), acc_sc[...], 0.0)

            @pl.when(first)
            def _():
                _to_row_tiles(mine, out_ref)

            @pl.when(jnp.logical_not(first))
            def _():
                _to_row_tiles(_from_row_tiles(out_ref, 0, tm) + mine, out_ref)


def _expert_layer(xs, visit_tile, visit_expert, visit_lo, visit_hi, n_visits, gain, w_gu, w_down, tm, tf):
    f = w_down.shape[1]
    nc = f // tf
    tile = lambda v, c, vt, ve, lo, hi, nv: (vt[v], 0)
    grid_spec = pltpu.PrefetchScalarGridSpec(
        num_scalar_prefetch=5,
        grid=(visit_tile.shape[0], nc),
        in_specs=[
            pl.BlockSpec((tm * ROW_TILE, LANES), tile),
            pl.BlockSpec((1, D_MODEL), lambda v, c, vt, ve, lo, hi, nv: (0, 0)),
            pl.BlockSpec((None, D_MODEL, tf), lambda v, c, vt, ve, lo, hi, nv: (ve[v], 0, c)),
            pl.BlockSpec((None, D_MODEL, tf), lambda v, c, vt, ve, lo, hi, nv: (ve[v], 0, nc + c)),
            pl.BlockSpec((None, tf, D_MODEL), lambda v, c, vt, ve, lo, hi, nv: (ve[v], c, 0)),
        ],
        out_specs=pl.BlockSpec((tm * ROW_TILE, LANES), tile),
        scratch_shapes=[pltpu.VMEM((tm, D_MODEL), _BF16), pltpu.VMEM((tm, D_MODEL), _F32)],
    )
    return pl.pallas_call(
        functools.partial(_expert_kernel, tm=tm),
        grid_spec=grid_spec,
        out_shape=jax.ShapeDtypeStruct(xs.shape, _F32),
        compiler_params=_params(("arbitrary", "arbitrary")),
        name="swiglu_routed",
    )(visit_tile, visit_expert, visit_lo, visit_hi, n_visits, xs, gain.reshape(1, D_MODEL), w_gu, w_gu, w_down)


def _combine_kernel(pos_ref, pos_next_ref, ys_hbm, h_ref, g1_ref, g2_ref, gf_ref, out_ref, buf, sem, *, tq):
    j = pl.program_id(0)
    slot = lax.rem(j, 2)
    rows = TOP_K * tq

    def start_gather(pos, dst_slot):
        for r in range(tq):
            for k in range(TOP_K):
                _start_row_gather(ys_hbm, pos[0, k, r], buf, dst_slot, k * tq + r, rows, sem, k)

    @pl.when(j == 0)
    def _():
        start_gather(pos_ref, 0)

    for s in range(2):
        @pl.when((j + 1 < pl.num_programs(0)) & (slot == s))
        def _():
            start_gather(pos_next_ref, 1 - s)

    rep = D_MODEL // LANES
    for s in range(2):
        @pl.when(slot == s)
        def _():
            _wait_row_gather(ys_hbm, buf, s, rows, sem)
            y = (jnp.tile(g1_ref[...], (1, rep)) * _from_row_tiles(buf, s * rows, tq)
                 + jnp.tile(g2_ref[...], (1, rep)) * _from_row_tiles(buf, s * rows + tq, tq))
            out_ref[...] = _rms_unit(h_ref[...] + y) * gf_ref[...]


def _moe_combine(ys, h, g1, g2, pos, final_gain, tq, tile_off, n_tiles):
    pos3 = pos[tile_off * tq:(tile_off + n_tiles) * tq].reshape(n_tiles, tq, TOP_K).transpose(0, 2, 1)
    smem_pos = lambda index_map: pl.BlockSpec((1, TOP_K, tq), index_map, memory_space=pltpu.SMEM)
    tok = lambda width: pl.BlockSpec((tq, width), lambda j: (j + tile_off, 0))
    return pl.pallas_call(
        functools.partial(_combine_kernel, tq=tq),
        grid=(n_tiles,),
        in_specs=[
            smem_pos(lambda j: (j, 0, 0)),
            smem_pos(lambda j: (jnp.minimum(j + 1, n_tiles - 1), 0, 0)),
            pl.BlockSpec(memory_space=pl.ANY),
            tok(D_MODEL), tok(LANES), tok(LANES),
            pl.BlockSpec((1, D_MODEL), lambda j: (0, 0)),
        ],
        out_specs=pl.BlockSpec((tq, D_MODEL), lambda j: (j, 0)),
        out_shape=jax.ShapeDtypeStruct((n_tiles * tq, D_MODEL), _F32),
        scratch_shapes=[pltpu.VMEM((2 * TOP_K * tq * ROW_TILE, LANES), _F32), pltpu.SemaphoreType.DMA((2,))],
        compiler_params=_params(("arbitrary",)),
        name="moe_combine",
    )(pos3, pos3, ys, h, g1, g2, final_gain.reshape(1, D_MODEL))


def kernel(x_prompt, x_sample, state_pool, cache_k_win, cache_v_win, norm_mix, norm_ffn, pool_w, pool_scale,
           norm_kv, w_kv, w_q, w_o, w_ffn_gu, w_ffn_down, w_router, w_exp_gu, w_exp_down, norm_final):
    nb, seq, _ = x_prompt.shape
    ns, ts, _ = x_sample.shape
    n_ctx = cache_k_win.shape[1]
    mp, ms = nb * seq, ns * ts
    w_kv_b = w_kv.astype(_BF16)
    w_q_b = w_q[0].astype(_BF16)
    w_o_b = w_o[0].astype(_BF16)
    w_ffn_gu_b = w_ffn_gu[0].astype(_BF16)
    w_ffn_down_b = w_ffn_down[0].astype(_BF16)
    tf_ffn = w_ffn_down_b.shape[0] // 2
    tf_exp = w_exp_down.shape[2] // 2

    dils = tuple(d for _, d in ATTN_GROUPS)
    h, pool_prompt = _pool_layer(x_prompt, None, norm_mix[0], pool_w[0], pool_scale[0], tm=TM_POOL, bn=1)
    h = _ffn_layer(h.reshape(mp, D_MODEL), norm_ffn[0], w_ffn_gu_b, w_ffn_down_b, TM_FFN, tf_ffn)
    keep = min(MAX_WINDOW, seq)
    q, k, v, k_win_p, v_win_p, (w_exp_gu_b, w_exp_down_b) = _qkv_layer(
        h.reshape(nb, seq, D_MODEL), jnp.arange(seq, dtype=jnp.int32), norm_kv, norm_mix[1], w_kv_b, w_q_b,
        tm=TM_QKV, keep=keep, q_dils=dils, cast=(w_exp_gu[0], w_exp_down[0]))
    outs, lses = [], []
    for g, d in enumerate(dils):
        o, lse = _band_attention(q[g], k[d], v[d], g, tb=ATTN_BANDS)
        outs.append(o)
        lses.append(lse)

    hs, pool_sample = _pool_layer(x_sample, state_pool[0], norm_mix[0], pool_w[0], pool_scale[0], tm=ts, bn=8)
    hs = _ffn_layer(hs.reshape(ms, D_MODEL), norm_ffn[0], w_ffn_gu_b, w_ffn_down_b, ms, tf_ffn)
    pos_s = jnp.tile(PAST_LEN + jnp.arange(ts, dtype=jnp.int32), ns)
    qs, ks, vs, ks_f, vs_f, _ = _qkv_layer(hs.reshape(1, ms, D_MODEL), pos_s, norm_kv, norm_mix[1], w_kv_b, w_q_b,
                                           tm=ms, keep=ms, q_dils=(1,) * N_ATTN_GROUPS)
    ckt = cache_k_win.transpose(0, 2, 3, 1).reshape(ns, D_MODEL, n_ctx)
    cvt = cache_v_win.transpose(0, 2, 3, 1).reshape(ns, D_MODEL, n_ctx)
    per_sample = lambda a: a.reshape(ns, ts, D_MODEL)
    os_, kwt, vwt = _cached_attention([per_sample(a) for a in qs], per_sample(ks[1]), per_sample(vs[1]),
                                      per_sample(ks_f), per_sample(vs_f), ckt, cvt, chunk=CACHE_CHUNK)
    h_all, h_tiles, idx_tile, g1, g2 = _merge_proj(outs, lses, h, os_.reshape(ms, D_MODEL), hs, w_o_b, norm_ffn[1],
                                                   w_router[0], tm=TM_TOKEN)

    visit_tile, visit_expert, visit_lo, visit_hi, n_visits, pos = _routing_tables(idx_tile, TM_EXPERT)
    xs = _push_rows(h_tiles, pos, TM_TOKEN)
    ys = _expert_layer(xs, visit_tile, visit_expert, visit_lo, visit_hi, n_visits, norm_ffn[1],
                       w_exp_gu_b, w_exp_down_b, TM_EXPERT, tf_exp)
    y_prompt = _moe_combine(ys, h_all, g1, g2, pos, norm_final, TM_TOKEN, 0, mp // TM_TOKEN)
    y_sample = _moe_combine(ys, h_all, g1, g2, pos, norm_final, TM_TOKEN, mp // TM_TOKEN, ms // TM_TOKEN)

    heads = lambda a: a.reshape(a.shape[0], a.shape[1], N_HEADS, HEAD_DIM)
    heads_t = lambda a: a.reshape(ns, N_HEADS, HEAD_DIM, a.shape[2]).transpose(0, 3, 1, 2)
    return (y_prompt.reshape(nb, seq, D_MODEL), y_sample.reshape(ns, ts, D_MODEL), pool_prompt, pool_sample,
            heads(k_win_p), heads(v_win_p), heads_t(kwt), heads_t(vwt))
```

```python
import functools
import math

import numpy as np
import jax
import jax.numpy as jnp
from jax import lax
from jax.experimental import pallas as pl
from jax.experimental.pallas import tpu as pltpu

D_MODEL = 1024
PAST_LEN = 16384
POOL_WINDOWS = (2, 4, 8, 16)
POOL_GROUP_DIM = D_MODEL // len(POOL_WINDOWS)
POOL_CTX = max(POOL_WINDOWS) - 1
HEAD_DIM = 64
N_HEADS = D_MODEL // HEAD_DIM
ATTN_GROUPS = ((128, 1), (512, 4), (2048, 16))
N_ATTN_GROUPS = len(ATTN_GROUPS)
BAND = 128
MAX_WINDOW = max(w for w, _ in ATTN_GROUPS)
ROT_DIM = HEAD_DIM // 4
ROPE_THETA = 500000.0
ATTN_SCALE = HEAD_DIM ** -0.5
N_EXPERTS = 8
TOP_K = 2
RMS_EPS = 1e-5

LANES = 128
HALO = 16
MASK_VALUE = -1e30
M_INIT = -1e20
VMEM_LIMIT = 56 * 1024 * 1024

TM_POOL = 512
TM_FFN = 512
TM_QKV = 256
TM_TOKEN = 256
TM_EXPERT = 512
MXU_COLS = 256
ROW_TILE = 8
ATTN_BANDS = 4
CACHE_CHUNK = 512

_F32 = jnp.float32
_BF16 = jnp.bfloat16
_NT = (((1,), (1,)), ((), ()))


def _params(semantics):
    return pltpu.CompilerParams(dimension_semantics=semantics, vmem_limit_bytes=VMEM_LIMIT)


def _rms_unit(x):
    return x * lax.rsqrt(jnp.mean(x * x, axis=-1, keepdims=True) + RMS_EPS)


def _dot(a, b):
    return jnp.dot(a, b, preferred_element_type=_F32)


def _swiglu_chunk(u, wg, wu, wd):
    g = _dot(u, wg)
    up = _dot(u, wu)
    return _dot((g * jax.nn.sigmoid(g) * up).astype(_BF16), wd)


def _pool_kernel(*refs, tm, n_ctx, bn):
    if n_ctx:
        x_ref, ctx_ref, g_ref, w_ref, sc_ref, h_ref, pool_ref, ext_ref = refs
    else:
        x_ref, g_ref, w_ref, sc_ref, h_ref, pool_ref, ext_ref = refs
        ctx_ref = None
    s = pl.program_id(1)
    pos = n_ctx + s * tm + lax.broadcasted_iota(jnp.int32, (tm, 1), 0)
    for b in range(bn):
        x = x_ref[b]
        u = _rms_unit(x) * g_ref[...]

        @pl.when(s == 0)
        def _():
            if n_ctx:
                ext_ref[b, 0:HALO - n_ctx, :] = jnp.zeros((HALO - n_ctx, D_MODEL), _F32)
                ext_ref[b, HALO - n_ctx:HALO, :] = ctx_ref[b]
            else:
                ext_ref[b, 0:HALO, :] = jnp.zeros((HALO, D_MODEL), _F32)

        ext_ref[b, HALO:HALO + tm, :] = u
        for g, w in enumerate(POOL_WINDOWS):
            sl = slice(g * POOL_GROUP_DIM, (g + 1) * POOL_GROUP_DIM)
            acc = u[:, sl]
            for k in range(1, w):
                acc = acc + ext_ref[b, HALO - k:HALO - k + tm, sl]
            cnt = jnp.minimum(pos + 1, w).astype(_F32)
            d = acc / cnt - u[:, sl]
            y = _dot(d.astype(_BF16), w_ref[g])
            h_ref[b, :, sl] = x[:, sl] + y * sc_ref[:, sl]

        @pl.when(s == pl.num_programs(1) - 1)
        def _():
            pool_ref[0, b] = ext_ref[b, HALO + tm - POOL_CTX:HALO + tm, :]

        ext_ref[b, 0:HALO, :] = ext_ref[b, tm:tm + HALO, :]


def _pool_layer(x, ctx, gain, w_pool, scale, tm, bn):
    n, t, _ = x.shape
    n_ctx = 0 if ctx is None else POOL_CTX
    grid = (n // bn, t // tm)
    row = lambda i, s: (i, s, 0)
    const2 = lambda i, s: (0, 0)
    in_specs = [pl.BlockSpec((bn, tm, D_MODEL), row)]
    args = [x]
    if ctx is not None:
        in_specs.append(pl.BlockSpec((bn, POOL_CTX, D_MODEL), lambda i, s: (i, 0, 0)))
        args.append(ctx)
    in_specs += [
        pl.BlockSpec((1, D_MODEL), const2),
        pl.BlockSpec((len(POOL_WINDOWS), POOL_GROUP_DIM, POOL_GROUP_DIM), lambda i, s: (0, 0, 0)),
        pl.BlockSpec((1, D_MODEL), const2),
    ]
    args += [gain.reshape(1, D_MODEL), w_pool.astype(_BF16), scale.reshape(1, D_MODEL)]
    return pl.pallas_call(
        functools.partial(_pool_kernel, tm=tm, n_ctx=n_ctx, bn=bn),
        grid=grid,
        in_specs=in_specs,
        out_specs=[
            pl.BlockSpec((bn, tm, D_MODEL), row),
            pl.BlockSpec((1, bn, POOL_CTX, D_MODEL), lambda i, s: (0, i, 0, 0)),
        ],
        out_shape=[
            jax.ShapeDtypeStruct((n, t, D_MODEL), _F32),
            jax.ShapeDtypeStruct((1, n, POOL_CTX, D_MODEL), _F32),
        ],
        scratch_shapes=[pltpu.VMEM((bn, HALO + tm, D_MODEL), _F32)],
        compiler_params=_params(("parallel", "arbitrary")),
        name="pool_layer",
    )(*args)


def _ffn_kernel(h_ref, g_ref, wg_ref, wu_ref, wd_ref, out_ref, u_sc, acc_sc):
    c = pl.program_id(1)

    @pl.when(c == 0)
    def _():
        u_sc[...] = (_rms_unit(h_ref[...]) * g_ref[...]).astype(_BF16)
        acc_sc[...] = jnp.zeros_like(acc_sc)

    acc_sc[...] += _swiglu_chunk(u_sc[...], wg_ref[...], wu_ref[...], wd_ref[...])

    @pl.when(c == pl.num_programs(1) - 1)
    def _():
        out_ref[...] = h_ref[...] + acc_sc[...]


def _ffn_layer(h, gain, w_gu, w_down, tm, tf):
    m = h.shape[0]
    f = w_down.shape[0]
    nc = f // tf
    row = lambda i, c: (i, 0)
    return pl.pallas_call(
        _ffn_kernel,
        grid=(m // tm, nc),
        in_specs=[
            pl.BlockSpec((tm, D_MODEL), row),
            pl.BlockSpec((1, D_MODEL), lambda i, c: (0, 0)),
            pl.BlockSpec((D_MODEL, tf), lambda i, c: (0, c)),
            pl.BlockSpec((D_MODEL, tf), lambda i, c: (0, nc + c)),
            pl.BlockSpec((tf, D_MODEL), lambda i, c: (c, 0)),
        ],
        out_specs=pl.BlockSpec((tm, D_MODEL), row),
        out_shape=jax.ShapeDtypeStruct((m, D_MODEL), _F32),
        scratch_shapes=[pltpu.VMEM((tm, D_MODEL), _BF16), pltpu.VMEM((tm, D_MODEL), _F32)],
        compiler_params=_params(("parallel", "arbitrary")),
        name="swiglu_dense",
    )(h, gain.reshape(1, D_MODEL), w_gu, w_gu, w_down)


def _rope_tables(pos):
    half = ROT_DIM // 2
    inv_freq = jnp.exp(-math.log(ROPE_THETA) * jnp.arange(half, dtype=_F32) / half)
    ang = pos.astype(_F32)[:, None] * inv_freq[None, :]
    cos, sin = jnp.cos(ang), jnp.sin(ang)
    t = pos.shape[0]
    rest = HEAD_DIM - ROT_DIM
    zero_h = jnp.zeros((t, half), _F32)
    zero_r = jnp.zeros((t, rest), _F32)
    c = jnp.concatenate([cos, cos, jnp.ones((t, rest), _F32)], axis=1)
    s1 = jnp.concatenate([zero_h, sin, zero_r], axis=1)
    s2 = jnp.concatenate([-sin, zero_h, zero_r], axis=1)
    rep = LANES // HEAD_DIM
    return tuple(jnp.tile(a, (1, rep)) for a in (c, s1, s2))


def _store_by_class(t, j, scr, slot, targets, tm):
    sl = slice(j * LANES, (j + 1) * LANES)
    if any(d > 1 for d, _ in targets):
        scr[slot] = t
    for d, ref in targets:
        if d == 1:
            ref[0, :, sl] = t.astype(_BF16)
        else:
            for r in range(d):
                ref[r, :, sl] = scr[slot, pl.ds(r, tm // d, stride=d), :].astype(_BF16)


def _qkv_kernel(*refs, q_dils, kv_dils, tm, n_cast):
    h_ref, gkv_ref, gq_ref, wkv_ref, wq_ref, c_ref, s1_ref, s2_ref = refs[:8]
    rest = list(refs[8:])
    cast_in = [rest.pop(0) for _ in range(n_cast)]
    q_refs = [rest.pop(0) for _ in q_dils]
    k_refs = [rest.pop(0) for _ in kv_dils]
    v_refs = [rest.pop(0) for _ in kv_dils]
    kf_ref, vf_ref = rest.pop(0), rest.pop(0)
    cast_out = [rest.pop(0) for _ in range(n_cast)]
    q_scr, k_scr, v_scr = rest
    hn = _rms_unit(h_ref[...])
    ukv = (hn * gkv_ref[...]).astype(_BF16)
    uq = (hn * gq_ref[...]).astype(_BF16)
    c, s1, s2 = c_ref[...], s1_ref[...], s2_ref[...]
    half = ROT_DIM // 2
    tiles = D_MODEL // LANES
    cols = MXU_COLS // LANES

    def rope(t):
        return t * c + pltpu.roll(t, half, 1) * s1 + pltpu.roll(t, LANES - half, 1) * s2

    for jb in range(2 * tiles // cols):
        blk = _dot(ukv, wkv_ref[:, jb * MXU_COLS:(jb + 1) * MXU_COLS])
        for jj in range(cols):
            j = jb * cols + jj
            t = blk[:, jj * LANES:(jj + 1) * LANES]
            if j < tiles:
                t = rope(t)
                kf_ref[:, j * LANES:(j + 1) * LANES] = t
                _store_by_class(t, j, k_scr, j, list(zip(kv_dils, k_refs)), tm)
            else:
                j -= tiles
                vf_ref[:, j * LANES:(j + 1) * LANES] = t
                _store_by_class(t, j, v_scr, j, list(zip(kv_dils, v_refs)), tm)
    for g, d in enumerate(q_dils):
        for jb in range(tiles // cols):
            c0 = g * D_MODEL + jb * MXU_COLS
            blk = _dot(uq, wq_ref[:, c0:c0 + MXU_COLS])
            for jj in range(cols):
                j = jb * cols + jj
                t = rope(blk[:, jj * LANES:(jj + 1) * LANES]) * ATTN_SCALE
                _store_by_class(t, j, q_scr, g * tiles + j, [(d, q_refs[g])], tm)
    for src, dst in zip(cast_in, cast_out):
        dst[...] = src[...].astype(_BF16)


def _qkv_layer(h, pos, norm_kv, norm_q, w_kv, w_q, tm, keep, q_dils, cast=()):
    n, t, _ = h.shape
    off = (t - keep) // tm
    kv_dils = tuple(sorted(set(q_dils)))
    tables = _rope_tables(pos)
    const2 = lambda i, s: (0, 0)
    tab = pl.BlockSpec((tm, LANES), lambda i, s: (s, 0))
    win = lambda i, s: (i, jnp.maximum(s - off, 0), 0)
    by_class = lambda d: pl.BlockSpec((None, d, tm // d, D_MODEL), lambda i, s: (i, 0, s, 0))
    class_shape = lambda d: jax.ShapeDtypeStruct((n, d, t // d, D_MODEL), _BF16)
    out_dils = tuple(q_dils) + kv_dils + kv_dils
    tiles = D_MODEL // LANES
    per_n = t // tm
    steps = n * per_n

    def slab(a):
        e, r, c = a.shape
        per_e = steps // e
        assert steps % e == 0 and r % per_e == 0 and (r // per_e) % 16 == 0
        return pl.BlockSpec((None, r // per_e, c),
                            lambda i, s: ((i * per_n + s) // per_e, (i * per_n + s) % per_e, 0))

    res = pl.pallas_call(
        functools.partial(_qkv_kernel, q_dils=tuple(q_dils), kv_dils=kv_dils, tm=tm, n_cast=len(cast)),
        grid=(n, per_n),
        in_specs=[
            pl.BlockSpec((None, tm, D_MODEL), lambda i, s: (i, s, 0)),
            pl.BlockSpec((1, D_MODEL), const2),
            pl.BlockSpec((1, D_MODEL), const2),
            pl.BlockSpec((D_MODEL, 2 * D_MODEL), const2, pipeline_mode=pl.Buffered(1)),
            pl.BlockSpec((D_MODEL, len(q_dils) * D_MODEL), const2, pipeline_mode=pl.Buffered(1)),
            tab, tab, tab,
        ] + [slab(a) for a in cast],
        out_specs=([by_class(d) for d in out_dils] + [pl.BlockSpec((None, tm, D_MODEL), win)] * 2
                   + [slab(a) for a in cast]),
        out_shape=([class_shape(d) for d in out_dils] + [jax.ShapeDtypeStruct((n, keep, D_MODEL), _F32)] * 2
                   + [jax.ShapeDtypeStruct(a.shape, _BF16) for a in cast]),
        scratch_shapes=[
            pltpu.VMEM((len(q_dils) * tiles, tm, LANES), _F32),
            pltpu.VMEM((tiles, tm, LANES), _F32),
            pltpu.VMEM((tiles, tm, LANES), _F32),
        ],
        compiler_params=_params(("parallel", "arbitrary")),
        name="qkv_rope",
    )(h, norm_kv.reshape(1, D_MODEL), norm_q.reshape(1, D_MODEL), w_kv, w_q, *tables, *cast)
    ng, nk = len(q_dils), len(kv_dils)
    qs = res[:ng]
    ks = dict(zip(kv_dils, res[ng:ng + nk]))
    vs = dict(zip(kv_dils, res[ng + nk:ng + 2 * nk]))
    kf, vf = res[ng + 2 * nk], res[ng + 2 * nk + 1]
    return qs, ks, vs, kf, vf, tuple(res[ng + 2 * nk + 2:])


def _band_attn_kernel(q_ref, kh_ref, k_ref, vh_ref, v_ref, o_ref, lse_ref, kx_ref, vx_ref, *, tb):
    m_id = pl.program_id(2)
    kx_ref[0:BAND, :] = kh_ref[...]
    kx_ref[BAND:, :] = k_ref[...]
    vx_ref[0:BAND, :] = vh_ref[...]
    vx_ref[BAND:, :] = v_ref[...]
    row = lax.broadcasted_iota(jnp.int32, (BAND, 2 * BAND), 0)
    col = lax.broadcasted_iota(jnp.int32, (BAND, 2 * BAND), 1)
    dist = row - col + BAND
    bias = jnp.where(dist >= 0, jnp.where(dist <= BAND, 0.0, MASK_VALUE), MASK_VALUE)
    no_prev = jnp.where(m_id > 0, 0.0, MASK_VALUE)
    bias_first = jnp.where(col < BAND, bias + no_prev, bias)
    lane = lax.broadcasted_iota(jnp.int32, (BAND, LANES), 1)
    low = lane < HEAD_DIM
    for b in range(tb):
        bb = bias_first if b == 0 else bias
        rows = slice(b * BAND, (b + 1) * BAND)
        lse_ref[rows, N_HEADS:] = jnp.zeros((BAND, LANES - N_HEADS), _F32)
        for p in range(D_MODEL // LANES):
            sl = slice(p * LANES, (p + 1) * LANES)
            q2 = q_ref[rows, sl].astype(_F32)
            k2 = kx_ref[b * BAND:(b + 2) * BAND, sl]
            v2 = vx_ref[b * BAND:(b + 2) * BAND, sl]
            outs = []
            for hh in range(2):
                qm = (jnp.where(low, q2, 0.0) if hh == 0 else jnp.where(low, 0.0, q2)).astype(_BF16)
                s = lax.dot_general(qm, k2, _NT, preferred_element_type=_F32) + bb
                mx = jnp.max(s, axis=1, keepdims=True)
                pr = jnp.exp(s - mx)
                den = jnp.sum(pr, axis=1, keepdims=True)
                o = _dot(pr.astype(_BF16), v2)
                outs.append(o * (1.0 / den))
                head = 2 * p + hh
                lse_ref[rows, head:head + 1] = mx + jnp.log(den)
            o_ref[rows, sl] = jnp.where(low, outs[0], outs[1]).astype(_BF16)


def _band_attention(q, k, v, group, tb):
    n, dil, length, _ = k.shape
    rows = tb * BAND
    cur = lambda i, r, m: (i, r, m, 0)
    halo = lambda i, r, m: (i, r, jnp.maximum(m * tb - 1, 0), 0)
    band = pl.BlockSpec((None, None, rows, D_MODEL), cur)
    prev = pl.BlockSpec((None, None, BAND, D_MODEL), halo)
    return pl.pallas_call(
        functools.partial(_band_attn_kernel, tb=tb),
        grid=(n, dil, length // rows),
        in_specs=[band, prev, band, prev, band],
        out_specs=[band, pl.BlockSpec((None, None, rows, LANES), cur)],
        out_shape=[
            jax.ShapeDtypeStruct((n, dil, length, D_MODEL), _BF16),
            jax.ShapeDtypeStruct((n, dil, length, LANES), _F32),
        ],
        scratch_shapes=[pltpu.VMEM((BAND + rows, D_MODEL), _BF16), pltpu.VMEM((BAND + rows, D_MODEL), _BF16)],
        compiler_params=_params(("parallel", "parallel", "arbitrary")),
        name="band_attn_g%d" % group,
    )(q, k, k, v, v)


def _head_expand_matrix():
    e = np.zeros((LANES, N_ATTN_GROUPS * D_MODEL), np.float32)
    for part in range(2):
        for g in range(N_ATTN_GROUPS):
            for h in range(N_HEADS):
                r = part * N_ATTN_GROUPS * N_HEADS + g * N_HEADS + h
                e[r, g * D_MODEL + h * HEAD_DIM:g * D_MODEL + (h + 1) * HEAD_DIM] = 1.0
    return jnp.asarray(e, _BF16)


def _merge_proj_kernel(o0_ref, o1_ref, o2_ref, l0_ref, l1_ref, l2_ref, e_ref, h_ref, os_ref, hs_ref, wo_ref,
                       gr_ref, wrh_ref, wrl_ref, out_ref, ht_ref, idx_ref, g1_ref, g2_ref,
                       o1_scr, o2_scr, l1_scr, l2_scr, o_scr, *, n_merge, tm):
    i = pl.program_id(0)
    tiles = D_MODEL // LANES

    @pl.when(i < n_merge)
    def _():
        for o_ref, l_ref, o_dst, l_dst in ((o1_ref, l1_ref, o1_scr, l1_scr), (o2_ref, l2_ref, o2_scr, l2_scr)):
            d = o_ref.shape[0]
            for r in range(d):
                l_dst[pl.ds(r, tm // d, stride=d), :] = l_ref[r]
                for j in range(tiles):
                    o_dst[j, pl.ds(r, tm // d, stride=d), :] = o_ref[r, :, j * LANES:(j + 1) * LANES].astype(_F32)
        l0, l1, l2 = l0_ref[0], l1_scr[...], l2_scr[...]
        mx = jnp.maximum(jnp.maximum(l0, l1), l2)
        e0, e1, e2 = jnp.exp(l0 - mx), jnp.exp(l1 - mx), jnp.exp(l2 - mx)
        inv = 1.0 / (e0 + e1 + e2)
        lane = lax.broadcasted_iota(jnp.int32, l0.shape, 1)
        head = lane < N_HEADS
        a = (jnp.where(head, e0 * inv, 0.0)
             + pltpu.roll(jnp.where(head, e1 * inv, 0.0), N_HEADS, 1)
             + pltpu.roll(jnp.where(head, e2 * inv, 0.0), 2 * N_HEADS, 1))
        hi = a.astype(_BF16).astype(_F32)
        a2 = hi + pltpu.roll(a - hi, N_ATTN_GROUPS * N_HEADS, 1)
        w = _dot(a2.astype(_BF16), e_ref[...])
        for j in range(tiles):
            sl = slice(j * LANES, (j + 1) * LANES)
            o = (w[:, sl] * o0_ref[0, :, sl].astype(_F32)
                 + w[:, D_MODEL + j * LANES:D_MODEL + (j + 1) * LANES] * o1_scr[j]
                 + w[:, 2 * D_MODEL + j * LANES:2 * D_MODEL + (j + 1) * LANES] * o2_scr[j])
            o_scr[:, sl] = o.astype(_BF16)
        out_ref[...] = h_ref[...] + _dot(o_scr[...], wo_ref[...])

    @pl.when(i >= n_merge)
    def _():
        out_ref[...] = hs_ref[...] + _dot(os_ref[...].astype(_BF16), wo_ref[...])

    hn = out_ref[...]
    _to_row_tiles(hn, ht_ref)
    _route(hn, gr_ref[...], wrh_ref[...], wrl_ref[...], idx_ref, g1_ref, g2_ref)


def _merge_proj(outs, lses, h, o_sample, h_sample, w_o, route_gain, w_router, tm):
    m, ms = h.shape[0], h_sample.shape[0]
    n, _, seq, _ = outs[0].shape
    n_merge = m // tm
    per_n = seq // tm
    row = lambda i: (jnp.minimum(i, n_merge - 1), 0)
    srow = lambda i: (jnp.maximum(i - n_merge, 0), 0)
    const = lambda i: (0, 0)

    def by_class(a):
        d, width = a.shape[1], a.shape[3]

        def index(i):
            ic = jnp.minimum(i, n_merge - 1)
            return (ic // per_n, 0, ic % per_n, 0)

        return pl.BlockSpec((None, d, tm // d, width), index)

    tiles = D_MODEL // LANES
    wr = jnp.zeros((D_MODEL, LANES), _F32).at[:, :N_EXPERTS].set(w_router)
    wr_hi = wr.astype(_BF16)
    wr_lo = (wr - wr_hi.astype(_F32)).astype(_BF16)
    tile_out = lambda width: pl.BlockSpec((tm, width), lambda i: (i, 0))
    tile_shape = lambda width: jax.ShapeDtypeStruct((m + ms, width), _F32)
    return pl.pallas_call(
        functools.partial(_merge_proj_kernel, n_merge=n_merge, tm=tm),
        grid=((m + ms) // tm,),
        in_specs=[by_class(a) for a in outs] + [by_class(a) for a in lses] + [
            pl.BlockSpec((LANES, N_ATTN_GROUPS * D_MODEL), const),
            pl.BlockSpec((tm, D_MODEL), row),
            pl.BlockSpec((tm, D_MODEL), srow),
            pl.BlockSpec((tm, D_MODEL), srow),
            pl.BlockSpec((D_MODEL, D_MODEL), const),
            pl.BlockSpec((1, D_MODEL), const),
            pl.BlockSpec((D_MODEL, LANES), const),
            pl.BlockSpec((D_MODEL, LANES), const),
        ],
        out_specs=[tile_out(D_MODEL), pl.BlockSpec((tm * ROW_TILE, LANES), lambda i: (i, 0)),
                   tile_out(LANES), tile_out(LANES), tile_out(LANES)],
        out_shape=[tile_shape(D_MODEL), jax.ShapeDtypeStruct(((m + ms) * ROW_TILE, LANES), _F32),
                   tile_shape(LANES), tile_shape(LANES), tile_shape(LANES)],
        scratch_shapes=[
            pltpu.VMEM((tiles, tm, LANES), _F32),
            pltpu.VMEM((tiles, tm, LANES), _F32),
            pltpu.VMEM((tm, LANES), _F32),
            pltpu.VMEM((tm, LANES), _F32),
            pltpu.VMEM((tm, D_MODEL), _BF16),
        ],
        compiler_params=_params(("parallel",)),
        name="merge_out_proj",
    )(*outs, *lses, _head_expand_matrix(), h, o_sample, h_sample, w_o, route_gain.reshape(1, D_MODEL), wr_hi, wr_lo)


def _cached_bias(t, n_ctx):
    rows = N_ATTN_GROUPS * N_HEADS * t
    cache = np.full((rows, n_ctx), MASK_VALUE, np.float32)
    new = np.full((rows, LANES), MASK_VALUE, np.float32)
    for g, (win, dil) in enumerate(ATTN_GROUPS):
        for j in range(t):
            idx = n_ctx + j - np.arange(win // dil + 1) * dil
            idx = idx[idx >= 0]
            for h in range(N_HEADS):
                r = (g * N_HEADS + h) * t + j
                cache[r, idx[idx < n_ctx]] = 0.0
                new[r, idx[idx >= n_ctx] - n_ctx] = 0.0
    return cache, new


def _cached_attn_kernel(q0_ref, q1_ref, q2_ref, kc_ref, vc_ref, kx_ref, vx_ref, kn_ref, vn_ref, knt_ref, vnt_ref,
                        bias_ref, biasn_ref, o_ref, kw_ref, vw_ref, qbd_ref, m_ref, l_ref, acc_ref,
                        *, t, first_chunk):
    c = pl.program_id(1)
    last = pl.num_programs(1) - 1
    gh = N_HEADS * t
    row = lax.broadcasted_iota(jnp.int32, (gh, D_MODEL), 0)
    lane = lax.broadcasted_iota(jnp.int32, (gh, D_MODEL), 1)
    own = jnp.right_shift(row, int(math.log2(t))) == jnp.right_shift(lane, int(math.log2(HEAD_DIM)))

    @pl.when(c == 0)
    def _():
        for g, q_ref in enumerate((q0_ref, q1_ref, q2_ref)):
            tiled = jnp.concatenate([q_ref[...].astype(_F32)] * N_HEADS, axis=0)
            qbd_ref[g * gh:(g + 1) * gh, :] = jnp.where(own, tiled, 0.0).astype(_BF16)
        m_ref[...] = jnp.full(m_ref.shape, M_INIT, _F32)
        l_ref[...] = jnp.zeros_like(l_ref)
        acc_ref[...] = jnp.zeros_like(acc_ref)

    def update(rows, s, weighted_values):
        m_old = m_ref[rows, :]
        m_new = jnp.maximum(m_old, jnp.max(s, axis=1, keepdims=True))
        alpha = jnp.exp(m_old - m_new)
        pr = jnp.exp(s - m_new)
        l_ref[rows, :] = alpha * l_ref[rows, :] + jnp.sum(pr, axis=1, keepdims=True)
        acc_ref[rows, :] = alpha * acc_ref[rows, :] + weighted_values(pr.astype(_BF16))
        m_ref[rows, :] = m_new

    kc, vc = kc_ref[...], vc_ref[...]
    kct, vct = kc.astype(_BF16), vc.astype(_BF16)
    for g in range(N_ATTN_GROUPS):
        rows = slice(g * gh, (g + 1) * gh)

        @pl.when(c >= first_chunk[g])
        def _():
            update(rows, _dot(qbd_ref[rows, :], kct) + bias_ref[rows, :],
                   lambda pr: lax.dot_general(pr, vct, _NT, preferred_element_type=_F32))

    chunk = kc.shape[1]
    for cur, nxt, new_t, w_ref in ((kc, kx_ref, knt_ref, kw_ref), (vc, vx_ref, vnt_ref, vw_ref)):
        tail = jnp.where(c == last, new_t[...], nxt[...])
        ext = jnp.concatenate([cur, tail], axis=1)
        w_ref[...] = pltpu.roll(ext, chunk + LANES - t, 1)[:, 0:chunk]

    @pl.when(c == last)
    def _():
        pad = jnp.zeros((LANES - t, D_MODEL), _F32)
        kn = jnp.concatenate([kn_ref[...].astype(_F32), pad], axis=0).astype(_BF16)
        vn = jnp.concatenate([vn_ref[...].astype(_F32), pad], axis=0).astype(_BF16)
        update(slice(0, N_ATTN_GROUPS * gh),
               lax.dot_general(qbd_ref[...], kn, _NT, preferred_element_type=_F32) + biasn_ref[...],
               lambda pr: _dot(pr, vn))
        ms = [m_ref[g * gh:(g + 1) * gh, :] for g in range(N_ATTN_GROUPS)]
        mx = jnp.maximum(jnp.maximum(ms[0], ms[1]), ms[2])
        num = jnp.zeros((gh, D_MODEL), _F32)
        den = jnp.zeros((gh, 1), _F32)
        for g in range(N_ATTN_GROUPS):
            w = jnp.exp(ms[g] - mx)
            num = num + w * acc_ref[g * gh:(g + 1) * gh, :]
            den = den + w * l_ref[g * gh:(g + 1) * gh, :]
        on = jnp.where(own, num * (1.0 / den), 0.0)
        o = on[0:t, :]
        for h in range(1, N_HEADS):
            o = o + on[h * t:(h + 1) * t, :]
        o_ref[...] = o


def _cached_attention(qs, k_new, v_new, k_new_f32, v_new_f32, cache_kt, cache_vt, chunk):
    n, t, _ = k_new.shape
    n_ctx = cache_kt.shape[2]
    assert n_ctx == MAX_WINDOW and n_ctx % chunk == 0 and chunk % LANES == 0 and t <= LANES
    bias_np, bias_new = _cached_bias(t, n_ctx)
    gh = N_HEADS * t
    rows = N_ATTN_GROUPS * gh
    n_chunks = n_ctx // chunk
    first_chunk = tuple(
        min(c for c in range(n_chunks) if (bias_np[g * gh:(g + 1) * gh, c * chunk:(c + 1) * chunk] == 0.0).any())
        for g in range(N_ATTN_GROUPS))
    new_t = lambda a: jnp.pad(a.transpose(0, 2, 1), ((0, 0), (0, 0), (0, LANES - t)))
    per_n = lambda i, c: (i, 0, 0)
    cur = pl.BlockSpec((None, D_MODEL, chunk), lambda i, c: (i, 0, c))
    ahead = pl.BlockSpec((None, D_MODEL, LANES),
                         lambda i, c: (i, 0, jnp.minimum((c + 1) * (chunk // LANES), n_ctx // LANES - 1)))
    small = pl.BlockSpec((None, t, D_MODEL), per_n)
    small_t = pl.BlockSpec((None, D_MODEL, LANES), per_n)
    window = jax.ShapeDtypeStruct((n, D_MODEL, n_ctx), _F32)
    return pl.pallas_call(
        functools.partial(_cached_attn_kernel, t=t, first_chunk=first_chunk),
        grid=(n, n_chunks),
        in_specs=[
            small, small, small,
            cur, cur, ahead, ahead,
            small, small, small_t, small_t,
            pl.BlockSpec((rows, chunk), lambda i, c: (0, c)),
            pl.BlockSpec((rows, LANES), lambda i, c: (0, 0)),
        ],
        out_specs=[small, cur, cur],
        out_shape=[jax.ShapeDtypeStruct((n, t, D_MODEL), _F32), window, window],
        scratch_shapes=[
            pltpu.VMEM((rows, D_MODEL), _BF16),
            pltpu.VMEM((rows, 1), _F32),
            pltpu.VMEM((rows, 1), _F32),
            pltpu.VMEM((rows, D_MODEL), _F32),
        ],
        compiler_params=_params(("parallel", "arbitrary")),
        name="cached_attn",
    )(*qs, cache_kt, cache_vt, cache_kt, cache_vt, k_new, v_new, new_t(k_new_f32), new_t(v_new_f32),
      jnp.asarray(bias_np), jnp.asarray(bias_new))


def _split_bf16(x):
    hi = x.astype(_BF16)
    return hi, (x - hi.astype(_F32)).astype(_BF16)


def _route(h, gain, wr_hi, wr_lo, idx_ref, g1_ref, g2_ref):
    u_hi, u_lo = _split_bf16(_rms_unit(h) * gain)
    logits = _dot(u_hi, wr_hi) + (_dot(u_hi, wr_lo) + _dot(u_lo, wr_hi))
    lane = lax.broadcasted_iota(jnp.int32, logits.shape, 1).astype(_F32)
    neg = -jnp.inf
    lg = jnp.where(lane < N_EXPERTS, logits, neg)
    v1 = jnp.max(lg, axis=1, keepdims=True)
    i1 = jnp.min(jnp.where(lg == v1, lane, float(LANES)), axis=1, keepdims=True)
    lg2 = jnp.where(lane == i1, neg, lg)
    v2 = jnp.max(lg2, axis=1, keepdims=True)
    i2 = jnp.min(jnp.where(lg2 == v2, lane, float(LANES)), axis=1, keepdims=True)
    e2 = jnp.exp(v2 - v1)
    den = 1.0 + e2
    idx_ref[...] = jnp.where(lane < LANES // 2, i1, i2)
    g1_ref[...] = jnp.broadcast_to(1.0 / den, g1_ref.shape)
    g2_ref[...] = jnp.broadcast_to(e2 / den, g2_ref.shape)


def _routing_tables(idx_tile, tm):
    m = idx_tile.shape[0]
    n_rows = TOP_K * m
    assert n_rows % tm == 0
    n_tiles = n_rows // tm
    n_visits = n_tiles + N_EXPERTS - 1
    e = jnp.stack([idx_tile[:, 0], idx_tile[:, LANES // 2]], axis=1).reshape(n_rows).astype(jnp.int32)
    onehot = (e[:, None] == jnp.arange(N_EXPERTS, dtype=jnp.int32)[None, :]).astype(jnp.int32)
    running = jnp.cumsum(onehot, axis=0)
    counts = running[-1]
    rank = jnp.sum(onehot * running, axis=1) - 1
    ends = jnp.cumsum(counts)
    starts = ends - counts
    pos = jnp.sum(onehot * starts[None, :], axis=1) + rank
    tile_lo = jnp.arange(n_tiles, dtype=jnp.int32)[:, None] * tm
    lo = jnp.maximum(starts[None, :], tile_lo) - tile_lo
    hi = jnp.minimum(ends[None, :], tile_lo + tm) - tile_lo
    used = (hi > lo).reshape(-1)
    n_used = jnp.sum(used.astype(jnp.int32))
    flat = jnp.nonzero(used, size=n_visits, fill_value=0)[0].astype(jnp.int32)
    flat = jnp.where(jnp.arange(n_visits) < n_used, flat, jnp.take(flat, n_used - 1))
    live = jnp.arange(n_visits) < n_used
    v_lo = jnp.where(live, jnp.take(lo.reshape(-1), flat), 0)
    v_hi = jnp.where(live, jnp.take(hi.reshape(-1), flat), 0)
    return flat // N_EXPERTS, flat % N_EXPERTS, v_lo, v_hi, n_used.reshape(1), pos.reshape(m, TOP_K)


def _to_row_tiles(x, dst_ref):
    rows = x.shape[0]
    for j in range(D_MODEL // LANES):
        dst_ref[pl.ds(j, rows, stride=ROW_TILE), :] = x[:, j * LANES:(j + 1) * LANES]


def _from_row_tiles(src_ref, base, rows):
    return jnp.concatenate(
        [src_ref[pl.ds(base * ROW_TILE + j, rows, stride=ROW_TILE), :] for j in range(D_MODEL // LANES)], axis=1)


def _start_row_gather(src_hbm, index, buf, slot, r, rows, sem, queue):
    dst = (slot * rows + r) * ROW_TILE
    pltpu.make_async_copy(src_hbm.at[pl.ds(index * ROW_TILE, ROW_TILE)], buf.at[pl.ds(dst, ROW_TILE)],
                          sem.at[slot]).start(priority=queue)


def _wait_row_gather(src_hbm, buf, slot, rows, sem):
    n = rows * ROW_TILE
    pltpu.make_async_copy(src_hbm.at[pl.ds(0, n)], buf.at[pl.ds(slot * n, n)], sem.at[slot]).wait()


def _push_rows_kernel(pos_ref, h_ref, xs_hbm, stage, sem, *, tq):
    j = pl.program_id(0)
    last = pl.num_programs(0) - 1
    slot = lax.rem(j, 2)
    n = tq * ROW_TILE

    def wait_slot(s):
        for _ in range(TOP_K):
            pltpu.make_async_copy(stage.at[pl.ds(s * n, n)], xs_hbm.at[pl.ds(0, n)], sem.at[s]).wait()

    for s in range(2):
        @pl.when(slot == s)
        def _():
            @pl.when(j >= 2)
            def _():
                wait_slot(s)

            stage[pl.ds(s * n, n), :] = h_ref[...]
            for r in range(tq):
                src = stage.at[pl.ds(s * n + r * ROW_TILE, ROW_TILE)]
                for k in range(TOP_K):
                    pltpu.make_async_copy(src, xs_hbm.at[pl.ds(pos_ref[0, k, r] * ROW_TILE, ROW_TILE)],
                                          sem.at[s]).start(priority=k)

            @pl.when(j == last)
            def _():
                wait_slot(s)

                @pl.when(j >= 1)
                def _():
                    wait_slot(1 - s)


def _push_rows(h_tiles, pos, tq):
    m = pos.shape[0]
    n_tiles = m // tq
    pos3 = pos.reshape(n_tiles, tq, TOP_K).transpose(0, 2, 1)
    return pl.pallas_call(
        functools.partial(_push_rows_kernel, tq=tq),
        grid=(n_tiles,),
        in_specs=[
            pl.BlockSpec((1, TOP_K, tq), lambda j: (j, 0, 0), memory_space=pltpu.SMEM),
            pl.BlockSpec((tq * ROW_TILE, LANES), lambda j: (j, 0)),
        ],
        out_specs=pl.BlockSpec(memory_space=pl.ANY),
        out_shape=jax.ShapeDtypeStruct((TOP_K * m * ROW_TILE, LANES), _F32),
        scratch_shapes=[pltpu.VMEM((2 * tq * ROW_TILE, LANES), _F32), pltpu.SemaphoreType.DMA((2,))],
        compiler_params=_params(("arbitrary",)),
        name="push_rows",
    )(pos3, h_tiles)


def _expert_kernel(vt_ref, ve_ref, lo_ref, hi_ref, nv_ref, x_ref, g_ref, wg_ref, wu_ref, wd_ref, out_ref,
                   u_sc, acc_sc, *, tm):
    del ve_ref
    v = pl.program_id(0)
    c = pl.program_id(1)
    active = v < nv_ref[0]
    first = jnp.logical_or(v == 0, vt_ref[v] != vt_ref[jnp.maximum(v - 1, 0)])

    @pl.when(active)
    def _():
        @pl.when(c == 0)
        def _():
            u_sc[...] = (_rms_unit(_from_row_tiles(x_ref, 0, tm)) * g_ref[...]).astype(_BF16)

        y = _swiglu_chunk(u_sc[...], wg_ref[...], wu_ref[...], wd_ref[...])

        @pl.when(c == 0)
        def _():
            acc_sc[...] = y

        @pl.when(c > 0)
        def _():
            acc_sc[...] += y

        @pl.when(c == pl.num_programs(1) - 1)
        def _():
            row = lax.broadcasted_iota(jnp.int32, (tm, 1), 0)
            mine = jnp.where((row >= lo_ref[v]) & (row < hi_ref[v]), acc_sc[...], 0.0)

            @pl.when(first)
            def _():
                _to_row_tiles(mine, out_ref)

            @pl.when(jnp.logical_not(first))
            def _():
                _to_row_tiles(_from_row_tiles(out_ref, 0, tm) + mine, out_ref)


def _expert_layer(xs, visit_tile, visit_expert, visit_lo, visit_hi, n_visits, gain, w_gu, w_down, tm, tf):
    f = w_down.shape[1]
    nc = f // tf
    tile = lambda v, c, vt, ve, lo, hi, nv: (vt[v], 0)
    grid_spec = pltpu.PrefetchScalarGridSpec(
        num_scalar_prefetch=5,
        grid=(visit_tile.shape[0], nc),
        in_specs=[
            pl.BlockSpec((tm * ROW_TILE, LANES), tile),
            pl.BlockSpec((1, D_MODEL), lambda v, c, vt, ve, lo, hi, nv: (0, 0)),
            pl.BlockSpec((None, D_MODEL, tf), lambda v, c, vt, ve, lo, hi, nv: (ve[v], 0, c)),
            pl.BlockSpec((None, D_MODEL, tf), lambda v, c, vt, ve, lo, hi, nv: (ve[v], 0, nc + c)),
            pl.BlockSpec((None, tf, D_MODEL), lambda v, c, vt, ve, lo, hi, nv: (ve[v], c, 0)),
        ],
        out_specs=pl.BlockSpec((tm * ROW_TILE, LANES), tile),
        scratch_shapes=[pltpu.VMEM((tm, D_MODEL), _BF16), pltpu.VMEM((tm, D_MODEL), _F32)],
    )
    return pl.pallas_call(
        functools.partial(_expert_kernel, tm=tm),
        grid_spec=grid_spec,
        out_shape=jax.ShapeDtypeStruct(xs.shape, _F32),
        compiler_params=_params(("arbitrary", "arbitrary")),
        name="swiglu_routed",
    )(visit_tile, visit_expert, visit_lo, visit_hi, n_visits, xs, gain.reshape(1, D_MODEL), w_gu, w_gu, w_down)


def _combine_kernel(pos_ref, pos_next_ref, ys_hbm, h_ref, g1_ref, g2_ref, gf_ref, out_ref, buf, sem, *, tq):
    j = pl.program_id(0)
    slot = lax.rem(j, 2)
    rows = TOP_K * tq

    def start_gather(pos, dst_slot):
        for r in range(tq):
            for k in range(TOP_K):
                _start_row_gather(ys_hbm, pos[0, k, r], buf, dst_slot, k * tq + r, rows, sem, k)

    @pl.when(j == 0)
    def _():
        start_gather(pos_ref, 0)

    for s in range(2):
        @pl.when((j + 1 < pl.num_programs(0)) & (slot == s))
        def _():
            start_gather(pos_next_ref, 1 - s)

    rep = D_MODEL // LANES
    for s in range(2):
        @pl.when(slot == s)
        def _():
            _wait_row_gather(ys_hbm, buf, s, rows, sem)
            y = (jnp.tile(g1_ref[...], (1, rep)) * _from_row_tiles(buf, s * rows, tq)
                 + jnp.tile(g2_ref[...], (1, rep)) * _from_row_tiles(buf, s * rows + tq, tq))
            out_ref[...] = _rms_unit(h_ref[...] + y) * gf_ref[...]


def _moe_combine(ys, h, g1, g2, pos, final_gain, tq, tile_off, n_tiles):
    pos3 = pos[tile_off * tq:(tile_off + n_tiles) * tq].reshape(n_tiles, tq, TOP_K).transpose(0, 2, 1)
    smem_pos = lambda index_map: pl.BlockSpec((1, TOP_K, tq), index_map, memory_space=pltpu.SMEM)
    tok = lambda width: pl.BlockSpec((tq, width), lambda j: (j + tile_off, 0))
    return pl.pallas_call(
        functools.partial(_combine_kernel, tq=tq),
        grid=(n_tiles,),
        in_specs=[
            smem_pos(lambda j: (j, 0, 0)),
            smem_pos(lambda j: (jnp.minimum(j + 1, n_tiles - 1), 0, 0)),
            pl.BlockSpec(memory_space=pl.ANY),
            tok(D_MODEL), tok(LANES), tok(LANES),
            pl.BlockSpec((1, D_MODEL), lambda j: (0, 0)),
        ],
        out_specs=pl.BlockSpec((tq, D_MODEL), lambda j: (j, 0)),
        out_shape=jax.ShapeDtypeStruct((n_tiles * tq, D_MODEL), _F32),
        scratch_shapes=[pltpu.VMEM((2 * TOP_K * tq * ROW_TILE, LANES), _F32), pltpu.SemaphoreType.DMA((2,))],
        compiler_params=_params(("arbitrary",)),
        name="moe_combine",
    )(pos3, pos3, ys, h, g1, g2, final_gain.reshape(1, D_MODEL))


def kernel(x_prompt, x_sample, state_pool, cache_k_win, cache_v_win, norm_mix, norm_ffn, pool_w, pool_scale,
           norm_kv, w_kv, w_q, w_o, w_ffn_gu, w_ffn_down, w_router, w_exp_gu, w_exp_down, norm_final):
    nb, seq, _ = x_prompt.shape
    ns, ts, _ = x_sample.shape
    n_ctx = cache_k_win.shape[1]
    mp, ms = nb * seq, ns * ts
    w_kv_b = w_kv.astype(_BF16)
    w_q_b = w_q[0].astype(_BF16)
    w_o_b = w_o[0].astype(_BF16)
    w_ffn_gu_b = w_ffn_gu[0].astype(_BF16)
    w_ffn_down_b = w_ffn_down[0].astype(_BF16)
    tf_ffn = w_ffn_down_b.shape[0] // 2
    tf_exp = w_exp_down.shape[2] // 2

    dils = tuple(d for _, d in ATTN_GROUPS)
    h, pool_prompt = _pool_layer(x_prompt, None, norm_mix[0], pool_w[0], pool_scale[0], tm=TM_POOL, bn=1)
    h = _ffn_layer(h.reshape(mp, D_MODEL), norm_ffn[0], w_ffn_gu_b, w_ffn_down_b, TM_FFN, tf_ffn)
    keep = min(MAX_WINDOW, seq)
    q, k, v, k_win_p, v_win_p, (w_exp_gu_b, w_exp_down_b) = _qkv_layer(
        h.reshape(nb, seq, D_MODEL), jnp.arange(seq, dtype=jnp.int32), norm_kv, norm_mix[1], w_kv_b, w_q_b,
        tm=TM_QKV, keep=keep, q_dils=dils, cast=(w_exp_gu[0], w_exp_down[0]))
    outs, lses = [], []
    for g, d in enumerate(dils):
        o, lse = _band_attention(q[g], k[d], v[d], g, tb=min(ATTN_BANDS, seq // d // BAND))
        outs.append(o)
        lses.append(lse)

    hs, pool_sample = _pool_layer(x_sample, state_pool[0], norm_mix[0], pool_w[0], pool_scale[0], tm=ts, bn=8)
    hs = _ffn_layer(hs.reshape(ms, D_MODEL), norm_ffn[0], w_ffn_gu_b, w_ffn_down_b, ms, tf_ffn)
    pos_s = jnp.tile(PAST_LEN + jnp.arange(ts, dtype=jnp.int32), ns)
    qs, ks, vs, ks_f, vs_f, _ = _qkv_layer(hs.reshape(1, ms, D_MODEL), pos_s, norm_kv, norm_mix[1], w_kv_b, w_q_b,
                                           tm=ms, keep=ms, q_dils=(1,) * N_ATTN_GROUPS)
    ckt = cache_k_win.transpose(0, 2, 3, 1).reshape(ns, D_MODEL, n_ctx)
    cvt = cache_v_win.transpose(0, 2, 3, 1).reshape(ns, D_MODEL, n_ctx)
    per_sample = lambda a: a.reshape(ns, ts, D_MODEL)
    os_, kwt, vwt = _cached_attention([per_sample(a) for a in qs], per_sample(ks[1]), per_sample(vs[1]),
                                      per_sample(ks_f), per_sample(vs_f), ckt, cvt, chunk=CACHE_CHUNK)
    h_all, h_tiles, idx_tile, g1, g2 = _merge_proj(outs, lses, h, os_.reshape(ms, D_MODEL), hs, w_o_b, norm_ffn[1],
                                                   w_router[0], tm=TM_TOKEN)

    visit_tile, visit_expert, visit_lo, visit_hi, n_visits, pos = _routing_tables(idx_tile, TM_EXPERT)
    xs = _push_rows(h_tiles, pos, TM_TOKEN)
    ys = _expert_layer(xs, visit_tile, visit_expert, visit_lo, visit_hi, n_visits, norm_ffn[1],
                       w_exp_gu_b, w_exp_down_b, TM_EXPERT, tf_exp)
    y_prompt = _moe_combine(ys, h_all, g1, g2, pos, norm_final, TM_TOKEN, 0, mp // TM_TOKEN)
    y_sample = _moe_combine(ys, h_all, g1, g2, pos, norm_final, TM_TOKEN, mp // TM_TOKEN, ms // TM_TOKEN)

    heads = lambda a: a.reshape(a.shape[0], a.shape[1], N_HEADS, HEAD_DIM)
    heads_t = lambda a: a.reshape(ns, N_HEADS, HEAD_DIM, a.shape[2]).transpose(0, 3, 1, 2)
    return (y_prompt.reshape(nb, seq, D_MODEL), y_sample.reshape(ns, ts, D_MODEL), pool_prompt, pool_sample,
            heads(k_win_p), heads(v_win_p), heads_t(kwt), heads_t(vwt))
```

```python
import functools
import math

import numpy as np
import jax
import jax.numpy as jnp
from jax import lax
from jax.experimental import pallas as pl
from jax.experimental.pallas import tpu as pltpu

D_MODEL = 1024
PAST_LEN = 16384
POOL_WINDOWS = (2, 4, 8, 16)
POOL_GROUP_DIM = D_MODEL // len(POOL_WINDOWS)
POOL_CTX = max(POOL_WINDOWS) - 1
HEAD_DIM = 64
N_HEADS = D_MODEL // HEAD_DIM
ATTN_GROUPS = ((128, 1), (512, 4), (2048, 16))
N_ATTN_GROUPS = len(ATTN_GROUPS)
BAND = 128
MAX_WINDOW = max(w for w, _ in ATTN_GROUPS)
ROT_DIM = HEAD_DIM // 4
ROPE_THETA = 500000.0
ATTN_SCALE = HEAD_DIM ** -0.5
N_EXPERTS = 8
TOP_K = 2
RMS_EPS = 1e-5

LANES = 128
HALO = 16
MASK_VALUE = -1e30
M_INIT = -1e20
VMEM_LIMIT = 56 * 1024 * 1024

TM_POOL = 512
TM_FFN = 512
TM_QKV = 256
TM_TOKEN = 256
TM_EXPERT = 512
MXU_COLS = 256
ROW_TILE = 8
ATTN_BANDS = 4
CACHE_CHUNK = 512

_F32 = jnp.float32
_BF16 = jnp.bfloat16
_NT = (((1,), (1,)), ((), ()))


def _params(semantics):
    return pltpu.CompilerParams(dimension_semantics=semantics, vmem_limit_bytes=VMEM_LIMIT)


def _rms_unit(x):
    return x * lax.rsqrt(jnp.mean(x * x, axis=-1, keepdims=True) + RMS_EPS)


def _dot(a, b):
    return jnp.dot(a, b, preferred_element_type=_F32)


def _swiglu_chunk(u, wg, wu, wd):
    g = _dot(u, wg)
    up = _dot(u, wu)
    return _dot((g * jax.nn.sigmoid(g) * up).astype(_BF16), wd)


def _pool_kernel(*refs, tm, n_ctx, bn):
    if n_ctx:
        x_ref, ctx_ref, g_ref, w_ref, sc_ref, h_ref, pool_ref, ext_ref = refs
    else:
        x_ref, g_ref, w_ref, sc_ref, h_ref, pool_ref, ext_ref = refs
        ctx_ref = None
    s = pl.program_id(1)
    pos = n_ctx + s * tm + lax.broadcasted_iota(jnp.int32, (tm, 1), 0)
    for b in range(bn):
        x = x_ref[b]
        u = _rms_unit(x) * g_ref[...]

        @pl.when(s == 0)
        def _():
            if n_ctx:
                ext_ref[b, 0:HALO - n_ctx, :] = jnp.zeros((HALO - n_ctx, D_MODEL), _F32)
                ext_ref[b, HALO - n_ctx:HALO, :] = ctx_ref[b]
            else:
                ext_ref[b, 0:HALO, :] = jnp.zeros((HALO, D_MODEL), _F32)

        ext_ref[b, HALO:HALO + tm, :] = u
        for g, w in enumerate(POOL_WINDOWS):
            sl = slice(g * POOL_GROUP_DIM, (g + 1) * POOL_GROUP_DIM)
            acc = u[:, sl]
            for k in range(1, w):
                acc = acc + ext_ref[b, HALO - k:HALO - k + tm, sl]
            cnt = jnp.minimum(pos + 1, w).astype(_F32)
            d = acc / cnt - u[:, sl]
            y = _dot(d.astype(_BF16), w_ref[g])
            h_ref[b, :, sl] = x[:, sl] + y * sc_ref[:, sl]

        @pl.when(s == pl.num_programs(1) - 1)
        def _():
            pool_ref[0, b] = ext_ref[b, HALO + tm - POOL_CTX:HALO + tm, :]

        ext_ref[b, 0:HALO, :] = ext_ref[b, tm:tm + HALO, :]


def _pool_layer(x, ctx, gain, w_pool, scale, tm, bn):
    n, t, _ = x.shape
    n_ctx = 0 if ctx is None else POOL_CTX
    grid = (n // bn, t // tm)
    row = lambda i, s: (i, s, 0)
    const2 = lambda i, s: (0, 0)
    in_specs = [pl.BlockSpec((bn, tm, D_MODEL), row)]
    args = [x]
    if ctx is not None:
        in_specs.append(pl.BlockSpec((bn, POOL_CTX, D_MODEL), lambda i, s: (i, 0, 0)))
        args.append(ctx)
    in_specs += [
        pl.BlockSpec((1, D_MODEL), const2),
        pl.BlockSpec((len(POOL_WINDOWS), POOL_GROUP_DIM, POOL_GROUP_DIM), lambda i, s: (0, 0, 0)),
        pl.BlockSpec((1, D_MODEL), const2),
    ]
    args += [gain.reshape(1, D_MODEL), w_pool.astype(_BF16), scale.reshape(1, D_MODEL)]
    return pl.pallas_call(
        functools.partial(_pool_kernel, tm=tm, n_ctx=n_ctx, bn=bn),
        grid=grid,
        in_specs=in_specs,
        out_specs=[
            pl.BlockSpec((bn, tm, D_MODEL), row),
            pl.BlockSpec((1, bn, POOL_CTX, D_MODEL), lambda i, s: (0, i, 0, 0)),
        ],
        out_shape=[
            jax.ShapeDtypeStruct((n, t, D_MODEL), _F32),
            jax.ShapeDtypeStruct((1, n, POOL_CTX, D_MODEL), _F32),
        ],
        scratch_shapes=[pltpu.VMEM((bn, HALO + tm, D_MODEL), _F32)],
        compiler_params=_params(("parallel", "arbitrary")),
        name="pool_layer",
    )(*args)


def _ffn_kernel(h_ref, g_ref, wg_ref, wu_ref, wd_ref, out_ref, u_sc, acc_sc):
    c = pl.program_id(1)

    @pl.when(c == 0)
    def _():
        u_sc[...] = (_rms_unit(h_ref[...]) * g_ref[...]).astype(_BF16)
        acc_sc[...] = jnp.zeros_like(acc_sc)

    acc_sc[...] += _swiglu_chunk(u_sc[...], wg_ref[...], wu_ref[...], wd_ref[...])

    @pl.when(c == pl.num_programs(1) - 1)
    def _():
        out_ref[...] = h_ref[...] + acc_sc[...]


def _ffn_layer(h, gain, w_gu, w_down, tm, tf):
    m = h.shape[0]
    f = w_down.shape[0]
    nc = f // tf
    row = lambda i, c: (i, 0)
    return pl.pallas_call(
        _ffn_kernel,
        grid=(m // tm, nc),
        in_specs=[
            pl.BlockSpec((tm, D_MODEL), row),
            pl.BlockSpec((1, D_MODEL), lambda i, c: (0, 0)),
            pl.BlockSpec((D_MODEL, tf), lambda i, c: (0, c)),
            pl.BlockSpec((D_MODEL, tf), lambda i, c: (0, nc + c)),
            pl.BlockSpec((tf, D_MODEL), lambda i, c: (c, 0)),
        ],
        out_specs=pl.BlockSpec((tm, D_MODEL), row),
        out_shape=jax.ShapeDtypeStruct((m, D_MODEL), _F32),
        scratch_shapes=[pltpu.VMEM((tm, D_MODEL), _BF16), pltpu.VMEM((tm, D_MODEL), _F32)],
        compiler_params=_params(("parallel", "arbitrary")),
        name="swiglu_dense",
    )(h, gain.reshape(1, D_MODEL), w_gu, w_gu, w_down)


def _rope_tables(pos):
    half = ROT_DIM // 2
    inv_freq = jnp.exp(-math.log(ROPE_THETA) * jnp.arange(half, dtype=_F32) / half)
    ang = pos.astype(_F32)[:, None] * inv_freq[None, :]
    cos, sin = jnp.cos(ang), jnp.sin(ang)
    t = pos.shape[0]
    rest = HEAD_DIM - ROT_DIM
    zero_h = jnp.zeros((t, half), _F32)
    zero_r = jnp.zeros((t, rest), _F32)
    c = jnp.concatenate([cos, cos, jnp.ones((t, rest), _F32)], axis=1)
    s1 = jnp.concatenate([zero_h, sin, zero_r], axis=1)
    s2 = jnp.concatenate([-sin, zero_h, zero_r], axis=1)
    rep = LANES // HEAD_DIM
    return tuple(jnp.tile(a, (1, rep)) for a in (c, s1, s2))


def _store_by_class(t, j, scr, slot, targets, tm):
    sl = slice(j * LANES, (j + 1) * LANES)
    if any(d > 1 for d, _ in targets):
        scr[slot] = t
    for d, ref in targets:
        if d == 1:
            ref[0, :, sl] = t.astype(_BF16)
        else:
            for r in range(d):
                ref[r, :, sl] = scr[slot, pl.ds(r, tm // d, stride=d), :].astype(_BF16)


def _qkv_kernel(*refs, q_dils, kv_dils, tm, n_cast):
    h_ref, gkv_ref, gq_ref, wkv_ref, wq_ref, c_ref, s1_ref, s2_ref = refs[:8]
    rest = list(refs[8:])
    cast_in = [rest.pop(0) for _ in range(n_cast)]
    q_refs = [rest.pop(0) for _ in q_dils]
    k_refs = [rest.pop(0) for _ in kv_dils]
    v_refs = [rest.pop(0) for _ in kv_dils]
    kf_ref, vf_ref = rest.pop(0), rest.pop(0)
    cast_out = [rest.pop(0) for _ in range(n_cast)]
    q_scr, k_scr, v_scr = rest
    hn = _rms_unit(h_ref[...])
    ukv = (hn * gkv_ref[...]).astype(_BF16)
    uq = (hn * gq_ref[...]).astype(_BF16)
    c, s1, s2 = c_ref[...], s1_ref[...], s2_ref[...]
    half = ROT_DIM // 2
    tiles = D_MODEL // LANES
    cols = MXU_COLS // LANES

    def rope(t):
        return t * c + pltpu.roll(t, half, 1) * s1 + pltpu.roll(t, LANES - half, 1) * s2

    for jb in range(2 * tiles // cols):
        blk = _dot(ukv, wkv_ref[:, jb * MXU_COLS:(jb + 1) * MXU_COLS])
        for jj in range(cols):
            j = jb * cols + jj
            t = blk[:, jj * LANES:(jj + 1) * LANES]
            if j < tiles:
                t = rope(t)
                kf_ref[:, j * LANES:(j + 1) * LANES] = t
                _store_by_class(t, j, k_scr, j, list(zip(kv_dils, k_refs)), tm)
            else:
                j -= tiles
                vf_ref[:, j * LANES:(j + 1) * LANES] = t
                _store_by_class(t, j, v_scr, j, list(zip(kv_dils, v_refs)), tm)
    for g, d in enumerate(q_dils):
        for jb in range(tiles // cols):
            c0 = g * D_MODEL + jb * MXU_COLS
            blk = _dot(uq, wq_ref[:, c0:c0 + MXU_COLS])
            for jj in range(cols):
                j = jb * cols + jj
                t = rope(blk[:, jj * LANES:(jj + 1) * LANES]) * ATTN_SCALE
                _store_by_class(t, j, q_scr, g * tiles + j, [(d, q_refs[g])], tm)
    for src, dst in zip(cast_in, cast_out):
        dst[...] = src[...].astype(_BF16)


def _qkv_layer(h, pos, norm_kv, norm_q, w_kv, w_q, tm, keep, q_dils, cast=()):
    n, t, _ = h.shape
    off = (t - keep) // tm
    kv_dils = tuple(sorted(set(q_dils)))
    tables = _rope_tables(pos)
    const2 = lambda i, s: (0, 0)
    tab = pl.BlockSpec((tm, LANES), lambda i, s: (s, 0))
    win = lambda i, s: (i, jnp.maximum(s - off, 0), 0)
    by_class = lambda d: pl.BlockSpec((None, d, tm // d, D_MODEL), lambda i, s: (i, 0, s, 0))
    class_shape = lambda d: jax.ShapeDtypeStruct((n, d, t // d, D_MODEL), _BF16)
    out_dils = tuple(q_dils) + kv_dils + kv_dils
    tiles = D_MODEL // LANES
    per_n = t // tm
    steps = n * per_n

    def slab(a):
        e, r, c = a.shape
        per_e = steps // e
        assert steps % e == 0 and r % per_e == 0 and (r // per_e) % 16 == 0
        return pl.BlockSpec((None, r // per_e, c),
                            lambda i, s: ((i * per_n + s) // per_e, (i * per_n + s) % per_e, 0))

    res = pl.pallas_call(
        functools.partial(_qkv_kernel, q_dils=tuple(q_dils), kv_dils=kv_dils, tm=tm, n_cast=len(cast)),
        grid=(n, per_n),
        in_specs=[
            pl.BlockSpec((None, tm, D_MODEL), lambda i, s: (i, s, 0)),
            pl.BlockSpec((1, D_MODEL), const2),
            pl.BlockSpec((1, D_MODEL), const2),
            pl.BlockSpec((D_MODEL, 2 * D_MODEL), const2, pipeline_mode=pl.Buffered(1)),
            pl.BlockSpec((D_MODEL, len(q_dils) * D_MODEL), const2, pipeline_mode=pl.Buffered(1)),
            tab, tab, tab,
        ] + [slab(a) for a in cast],
        out_specs=([by_class(d) for d in out_dils] + [pl.BlockSpec((None, tm, D_MODEL), win)] * 2
                   + [slab(a) for a in cast]),
        out_shape=([class_shape(d) for d in out_dils] + [jax.ShapeDtypeStruct((n, keep, D_MODEL), _F32)] * 2
                   + [jax.ShapeDtypeStruct(a.shape, _BF16) for a in cast]),
        scratch_shapes=[
            pltpu.VMEM((len(q_dils) * tiles, tm, LANES), _F32),
            pltpu.VMEM((tiles, tm, LANES), _F32),
            pltpu.VMEM((tiles, tm, LANES), _F32),
        ],
        compiler_params=_params(("parallel", "arbitrary")),
        name="qkv_rope",
    )(h, norm_kv.reshape(1, D_MODEL), norm_q.reshape(1, D_MODEL), w_kv, w_q, *tables, *cast)
    ng, nk = len(q_dils), len(kv_dils)
    qs = res[:ng]
    ks = dict(zip(kv_dils, res[ng:ng + nk]))
    vs = dict(zip(kv_dils, res[ng + nk:ng + 2 * nk]))
    kf, vf = res[ng + 2 * nk], res[ng + 2 * nk + 1]
    return qs, ks, vs, kf, vf, tuple(res[ng + 2 * nk + 2:])


def _band_attn_kernel(q_ref, kh_ref, k_ref, vh_ref, v_ref, o_ref, lse_ref, kx_ref, vx_ref, *, tb):
    m_id = pl.program_id(2)
    kx_ref[0:BAND, :] = kh_ref[...]
    kx_ref[BAND:, :] = k_ref[...]
    vx_ref[0:BAND, :] = vh_ref[...]
    vx_ref[BAND:, :] = v_ref[...]
    row = lax.broadcasted_iota(jnp.int32, (BAND, 2 * BAND), 0)
    col = lax.broadcasted_iota(jnp.int32, (BAND, 2 * BAND), 1)
    dist = row - col + BAND
    bias = jnp.where(dist >= 0, jnp.where(dist <= BAND, 0.0, MASK_VALUE), MASK_VALUE)
    no_prev = jnp.where(m_id > 0, 0.0, MASK_VALUE)
    bias_first = jnp.where(col < BAND, bias + no_prev, bias)
    lane = lax.broadcasted_iota(jnp.int32, (BAND, LANES), 1)
    low = lane < HEAD_DIM
    for b in range(tb):
        bb = bias_first if b == 0 else bias
        rows = slice(b * BAND, (b + 1) * BAND)
        lse_ref[rows, N_HEADS:] = jnp.zeros((BAND, LANES - N_HEADS), _F32)
        for p in range(D_MODEL // LANES):
            sl = slice(p * LANES, (p + 1) * LANES)
            q2 = q_ref[rows, sl].astype(_F32)
            k2 = kx_ref[b * BAND:(b + 2) * BAND, sl]
            v2 = vx_ref[b * BAND:(b + 2) * BAND, sl]
            outs = []
            for hh in range(2):
                qm = (jnp.where(low, q2, 0.0) if hh == 0 else jnp.where(low, 0.0, q2)).astype(_BF16)
                s = lax.dot_general(qm, k2, _NT, preferred_element_type=_F32) + bb
                mx = jnp.max(s, axis=1, keepdims=True)
                pr = jnp.exp(s - mx)
                den = jnp.sum(pr, axis=1, keepdims=True)
                o = _dot(pr.astype(_BF16), v2)
                outs.append(o * (1.0 / den))
                head = 2 * p + hh
                lse_ref[rows, head:head + 1] = mx + jnp.log(den)
            o_ref[rows, sl] = jnp.where(low, outs[0], outs[1]).astype(_BF16)


def _band_attention(q, k, v, group, tb):
    n, dil, length, _ = k.shape
    rows = tb * BAND
    cur = lambda i, r, m: (i, r, m, 0)
    halo = lambda i, r, m: (i, r, jnp.maximum(m * tb - 1, 0), 0)
    band = pl.BlockSpec((None, None, rows, D_MODEL), cur)
    prev = pl.BlockSpec((None, None, BAND, D_MODEL), halo)
    return pl.pallas_call(
        functools.partial(_band_attn_kernel, tb=tb),
        grid=(n, dil, length // rows),
        in_specs=[band, prev, band, prev, band],
        out_specs=[band, pl.BlockSpec((None, None, rows, LANES), cur)],
        out_shape=[
            jax.ShapeDtypeStruct((n, dil, length, D_MODEL), _BF16),
            jax.ShapeDtypeStruct((n, dil, length, LANES), _F32),
        ],
        scratch_shapes=[pltpu.VMEM((BAND + rows, D_MODEL), _BF16), pltpu.VMEM((BAND + rows, D_MODEL), _BF16)],
        compiler_params=_params(("parallel", "parallel", "arbitrary")),
        name="band_attn_g%d" % group,
    )(q, k, k, v, v)


def _head_expand_matrix():
    e = np.zeros((LANES, N_ATTN_GROUPS * D_MODEL), np.float32)
    for part in range(2):
        for g in range(N_ATTN_GROUPS):
            for h in range(N_HEADS):
                r = part * N_ATTN_GROUPS * N_HEADS + g * N_HEADS + h
                e[r, g * D_MODEL + h * HEAD_DIM:g * D_MODEL + (h + 1) * HEAD_DIM] = 1.0
    return jnp.asarray(e, _BF16)


def _merge_proj_kernel(o0_ref, o1_ref, o2_ref, l0_ref, l1_ref, l2_ref, e_ref, h_ref, os_ref, hs_ref, wo_ref,
                       gr_ref, wrh_ref, wrl_ref, out_ref, idx_ref, g1_ref, g2_ref,
                       o1_scr, o2_scr, l1_scr, l2_scr, o_scr, *, n_merge, tm):
    i = pl.program_id(0)
    tiles = D_MODEL // LANES

    @pl.when(i < n_merge)
    def _():
        for o_ref, l_ref, o_dst, l_dst in ((o1_ref, l1_ref, o1_scr, l1_scr), (o2_ref, l2_ref, o2_scr, l2_scr)):
            d = o_ref.shape[0]
            for r in range(d):
                l_dst[pl.ds(r, tm // d, stride=d), :] = l_ref[r]
                for j in range(tiles):
                    o_dst[j, pl.ds(r, tm // d, stride=d), :] = o_ref[r, :, j * LANES:(j + 1) * LANES].astype(_F32)
        l0, l1, l2 = l0_ref[0], l1_scr[...], l2_scr[...]
        mx = jnp.maximum(jnp.maximum(l0, l1), l2)
        e0, e1, e2 = jnp.exp(l0 - mx), jnp.exp(l1 - mx), jnp.exp(l2 - mx)
        inv = 1.0 / (e0 + e1 + e2)
        lane = lax.broadcasted_iota(jnp.int32, l0.shape, 1)
        head = lane < N_HEADS
        a = (jnp.where(head, e0 * inv, 0.0)
             + pltpu.roll(jnp.where(head, e1 * inv, 0.0), N_HEADS, 1)
             + pltpu.roll(jnp.where(head, e2 * inv, 0.0), 2 * N_HEADS, 1))
        hi = a.astype(_BF16).astype(_F32)
        a2 = hi + pltpu.roll(a - hi, N_ATTN_GROUPS * N_HEADS, 1)
        w = _dot(a2.astype(_BF16), e_ref[...])
        for j in range(tiles):
            sl = slice(j * LANES, (j + 1) * LANES)
            o = (w[:, sl] * o0_ref[0, :, sl].astype(_F32)
                 + w[:, D_MODEL + j * LANES:D_MODEL + (j + 1) * LANES] * o1_scr[j]
                 + w[:, 2 * D_MODEL + j * LANES:2 * D_MODEL + (j + 1) * LANES] * o2_scr[j])
            o_scr[:, sl] = o.astype(_BF16)
        out_ref[...] = h_ref[...] + _dot(o_scr[...], wo_ref[...])

    @pl.when(i >= n_merge)
    def _():
        out_ref[...] = hs_ref[...] + _dot(os_ref[...].astype(_BF16), wo_ref[...])

    _route(out_ref[...], gr_ref[...], wrh_ref[...], wrl_ref[...], idx_ref, g1_ref, g2_ref)


def _merge_proj(outs, lses, h, o_sample, h_sample, w_o, route_gain, w_router, tm):
    m, ms = h.shape[0], h_sample.shape[0]
    n, _, seq, _ = outs[0].shape
    n_merge = m // tm
    per_n = seq // tm
    row = lambda i: (jnp.minimum(i, n_merge - 1), 0)
    srow = lambda i: (jnp.maximum(i - n_merge, 0), 0)
    const = lambda i: (0, 0)

    def by_class(a):
        d, width = a.shape[1], a.shape[3]

        def index(i):
            ic = jnp.minimum(i, n_merge - 1)
            return (ic // per_n, 0, ic % per_n, 0)

        return pl.BlockSpec((None, d, tm // d, width), index)

    tiles = D_MODEL // LANES
    wr = jnp.zeros((D_MODEL, LANES), _F32).at[:, :N_EXPERTS].set(w_router)
    wr_hi = wr.astype(_BF16)
    wr_lo = (wr - wr_hi.astype(_F32)).astype(_BF16)
    tile_out = lambda width: pl.BlockSpec((tm, width), lambda i: (i, 0))
    tile_shape = lambda width: jax.ShapeDtypeStruct((m + ms, width), _F32)
    return pl.pallas_call(
        functools.partial(_merge_proj_kernel, n_merge=n_merge, tm=tm),
        grid=((m + ms) // tm,),
        in_specs=[by_class(a) for a in outs] + [by_class(a) for a in lses] + [
            pl.BlockSpec((LANES, N_ATTN_GROUPS * D_MODEL), const),
            pl.BlockSpec((tm, D_MODEL), row),
            pl.BlockSpec((tm, D_MODEL), srow),
            pl.BlockSpec((tm, D_MODEL), srow),
            pl.BlockSpec((D_MODEL, D_MODEL), const),
            pl.BlockSpec((1, D_MODEL), const),
            pl.BlockSpec((D_MODEL, LANES), const),
            pl.BlockSpec((D_MODEL, LANES), const),
        ],
        out_specs=[tile_out(D_MODEL), tile_out(LANES), tile_out(LANES), tile_out(LANES)],
        out_shape=[tile_shape(D_MODEL), tile_shape(LANES), tile_shape(LANES), tile_shape(LANES)],
        scratch_shapes=[
            pltpu.VMEM((tiles, tm, LANES), _F32),
            pltpu.VMEM((tiles, tm, LANES), _F32),
            pltpu.VMEM((tm, LANES), _F32),
            pltpu.VMEM((tm, LANES), _F32),
            pltpu.VMEM((tm, D_MODEL), _BF16),
        ],
        compiler_params=_params(("parallel",)),
        name="merge_out_proj",
    )(*outs, *lses, _head_expand_matrix(), h, o_sample, h_sample, w_o, route_gain.reshape(1, D_MODEL), wr_hi, wr_lo)


def _cached_bias(t, n_ctx):
    rows = N_ATTN_GROUPS * N_HEADS * t
    cache = np.full((rows, n_ctx), MASK_VALUE, np.float32)
    new = np.full((rows, LANES), MASK_VALUE, np.float32)
    for g, (win, dil) in enumerate(ATTN_GROUPS):
        for j in range(t):
            idx = n_ctx + j - np.arange(win // dil + 1) * dil
            idx = idx[idx >= 0]
            for h in range(N_HEADS):
                r = (g * N_HEADS + h) * t + j
                cache[r, idx[idx < n_ctx]] = 0.0
                new[r, idx[idx >= n_ctx] - n_ctx] = 0.0
    return cache, new


def _cached_attn_kernel(q0_ref, q1_ref, q2_ref, kc_ref, vc_ref, kx_ref, vx_ref, kn_ref, vn_ref, knt_ref, vnt_ref,
                        bias_ref, biasn_ref, o_ref, kw_ref, vw_ref, qbd_ref, m_ref, l_ref, acc_ref,
                        *, t, first_chunk):
    c = pl.program_id(1)
    last = pl.num_programs(1) - 1
    gh = N_HEADS * t
    row = lax.broadcasted_iota(jnp.int32, (gh, D_MODEL), 0)
    lane = lax.broadcasted_iota(jnp.int32, (gh, D_MODEL), 1)
    own = jnp.right_shift(row, int(math.log2(t))) == jnp.right_shift(lane, int(math.log2(HEAD_DIM)))

    @pl.when(c == 0)
    def _():
        for g, q_ref in enumerate((q0_ref, q1_ref, q2_ref)):
            tiled = jnp.concatenate([q_ref[...].astype(_F32)] * N_HEADS, axis=0)
            qbd_ref[g * gh:(g + 1) * gh, :] = jnp.where(own, tiled, 0.0).astype(_BF16)
        m_ref[...] = jnp.full(m_ref.shape, M_INIT, _F32)
        l_ref[...] = jnp.zeros_like(l_ref)
        acc_ref[...] = jnp.zeros_like(acc_ref)

    def update(rows, s, weighted_values):
        m_old = m_ref[rows, :]
        m_new = jnp.maximum(m_old, jnp.max(s, axis=1, keepdims=True))
        alpha = jnp.exp(m_old - m_new)
        pr = jnp.exp(s - m_new)
        l_ref[rows, :] = alpha * l_ref[rows, :] + jnp.sum(pr, axis=1, keepdims=True)
        acc_ref[rows, :] = alpha * acc_ref[rows, :] + weighted_values(pr.astype(_BF16))
        m_ref[rows, :] = m_new

    kc, vc = kc_ref[...], vc_ref[...]
    kct, vct = kc.astype(_BF16), vc.astype(_BF16)
    spans = []
    for g in range(N_ATTN_GROUPS):
        if spans and spans[-1][2] == first_chunk[g]:
            spans[-1][1] = g + 1
        else:
            spans.append([g, g + 1, first_chunk[g]])
    for g0, g1, start in spans:
        rows = slice(g0 * gh, g1 * gh)

        @pl.when(c >= start)
        def _():
            update(rows, _dot(qbd_ref[rows, :], kct) + bias_ref[rows, :],
                   lambda pr: lax.dot_general(pr, vct, _NT, preferred_element_type=_F32))

    chunk = kc.shape[1]
    for cur, nxt, new_t, w_ref in ((kc, kx_ref, knt_ref, kw_ref), (vc, vx_ref, vnt_ref, vw_ref)):
        tail = jnp.where(c == last, new_t[...], nxt[...])
        ext = jnp.concatenate([cur, tail], axis=1)
        w_ref[...] = pltpu.roll(ext, chunk + LANES - t, 1)[:, 0:chunk]

    @pl.when(c == last)
    def _():
        pad = jnp.zeros((LANES - t, D_MODEL), _F32)
        kn = jnp.concatenate([kn_ref[...].astype(_F32), pad], axis=0).astype(_BF16)
        vn = jnp.concatenate([vn_ref[...].astype(_F32), pad], axis=0).astype(_BF16)
        update(slice(0, N_ATTN_GROUPS * gh),
               lax.dot_general(qbd_ref[...], kn, _NT, preferred_element_type=_F32) + biasn_ref[...],
               lambda pr: _dot(pr, vn))
        ms = [m_ref[g * gh:(g + 1) * gh, :] for g in range(N_ATTN_GROUPS)]
        mx = jnp.maximum(jnp.maximum(ms[0], ms[1]), ms[2])
        num = jnp.zeros((gh, D_MODEL), _F32)
        den = jnp.zeros((gh, 1), _F32)
        for g in range(N_ATTN_GROUPS):
            w = jnp.exp(ms[g] - mx)
            num = num + w * acc_ref[g * gh:(g + 1) * gh, :]
            den = den + w * l_ref[g * gh:(g + 1) * gh, :]
        on = jnp.where(own, num * (1.0 / den), 0.0)
        o = on[0:t, :]
        for h in range(1, N_HEADS):
            o = o + on[h * t:(h + 1) * t, :]
        o_ref[...] = o


def _cached_attention(qs, k_new, v_new, k_new_f32, v_new_f32, cache_kt, cache_vt, chunk):
    n, t, _ = k_new.shape
    n_ctx = cache_kt.shape[2]
    assert n_ctx == MAX_WINDOW and n_ctx % chunk == 0 and chunk % LANES == 0 and t <= LANES
    bias_np, bias_new = _cached_bias(t, n_ctx)
    gh = N_HEADS * t
    rows = N_ATTN_GROUPS * gh
    n_chunks = n_ctx // chunk
    first_chunk = tuple(
        min(c for c in range(n_chunks) if (bias_np[g * gh:(g + 1) * gh, c * chunk:(c + 1) * chunk] == 0.0).any())
        for g in range(N_ATTN_GROUPS))
    new_t = lambda a: jnp.pad(a.transpose(0, 2, 1), ((0, 0), (0, 0), (0, LANES - t)))
    per_n = lambda i, c: (i, 0, 0)
    cur = pl.BlockSpec((None, D_MODEL, chunk), lambda i, c: (i, 0, c))
    ahead = pl.BlockSpec((None, D_MODEL, LANES),
                         lambda i, c: (i, 0, jnp.minimum((c + 1) * (chunk // LANES), n_ctx // LANES - 1)))
    small = pl.BlockSpec((None, t, D_MODEL), per_n)
    small_t = pl.BlockSpec((None, D_MODEL, LANES), per_n)
    window = jax.ShapeDtypeStruct((n, D_MODEL, n_ctx), _F32)
    return pl.pallas_call(
        functools.partial(_cached_attn_kernel, t=t, first_chunk=first_chunk),
        grid=(n, n_chunks),
        in_specs=[
            small, small, small,
            cur, cur, ahead, ahead,
            small, small, small_t, small_t,
            pl.BlockSpec((rows, chunk), lambda i, c: (0, c)),
            pl.BlockSpec((rows, LANES), lambda i, c: (0, 0)),
        ],
        out_specs=[small, cur, cur],
        out_shape=[jax.ShapeDtypeStruct((n, t, D_MODEL), _F32), window, window],
        scratch_shapes=[
            pltpu.VMEM((rows, D_MODEL), _BF16),
            pltpu.VMEM((rows, 1), _F32),
            pltpu.VMEM((rows, 1), _F32),
            pltpu.VMEM((rows, D_MODEL), _F32),
        ],
        compiler_params=_params(("parallel", "arbitrary")),
        name="cached_attn",
    )(*qs, cache_kt, cache_vt, cache_kt, cache_vt, k_new, v_new, new_t(k_new_f32), new_t(v_new_f32),
      jnp.asarray(bias_np), jnp.asarray(bias_new))


def _split_bf16(x):
    hi = x.astype(_BF16)
    return hi, (x - hi.astype(_F32)).astype(_BF16)


def _route(h, gain, wr_hi, wr_lo, idx_ref, g1_ref, g2_ref):
    u_hi, u_lo = _split_bf16(_rms_unit(h) * gain)
    logits = _dot(u_hi, wr_hi) + (_dot(u_hi, wr_lo) + _dot(u_lo, wr_hi))
    lane = lax.broadcasted_iota(jnp.int32, logits.shape, 1).astype(_F32)
    neg = -jnp.inf
    lg = jnp.where(lane < N_EXPERTS, logits, neg)
    v1 = jnp.max(lg, axis=1, keepdims=True)
    i1 = jnp.min(jnp.where(lg == v1, lane, float(LANES)), axis=1, keepdims=True)
    lg2 = jnp.where(lane == i1, neg, lg)
    v2 = jnp.max(lg2, axis=1, keepdims=True)
    i2 = jnp.min(jnp.where(lg2 == v2, lane, float(LANES)), axis=1, keepdims=True)
    e2 = jnp.exp(v2 - v1)
    den = 1.0 + e2
    idx_ref[...] = jnp.where(lane < LANES // 2, i1, i2)
    g1_ref[...] = jnp.broadcast_to(1.0 / den, g1_ref.shape)
    g2_ref[...] = jnp.broadcast_to(e2 / den, g2_ref.shape)


def _routing_tables(idx_tile, tm):
    m = idx_tile.shape[0]
    n_rows = TOP_K * m
    assert n_rows % tm == 0
    n_tiles = n_rows // tm
    n_visits = n_tiles + N_EXPERTS - 1
    e = jnp.stack([idx_tile[:, 0], idx_tile[:, LANES // 2]], axis=1).reshape(n_rows).astype(jnp.int32)
    onehot = (e[:, None] == jnp.arange(N_EXPERTS, dtype=jnp.int32)[None, :]).astype(jnp.int32)
    running = jnp.cumsum(onehot, axis=0)
    counts = running[-1]
    rank = jnp.sum(onehot * running, axis=1) - 1
    ends = jnp.cumsum(counts)
    starts = ends - counts
    pos = jnp.sum(onehot * starts[None, :], axis=1) + rank
    tile_lo = jnp.arange(n_tiles, dtype=jnp.int32)[:, None] * tm
    lo = jnp.maximum(starts[None, :], tile_lo) - tile_lo
    hi = jnp.minimum(ends[None, :], tile_lo + tm) - tile_lo
    used = (hi > lo).reshape(-1)
    n_used = jnp.sum(used.astype(jnp.int32))
    flat = jnp.nonzero(used, size=n_visits, fill_value=0)[0].astype(jnp.int32)
    flat = jnp.where(jnp.arange(n_visits) < n_used, flat, jnp.take(flat, n_used - 1))
    live = jnp.arange(n_visits) < n_used
    v_lo = jnp.where(live, jnp.take(lo.reshape(-1), flat), 0)
    v_hi = jnp.where(live, jnp.take(hi.reshape(-1), flat), 0)
    return flat // N_EXPERTS, flat % N_EXPERTS, v_lo, v_hi, n_used.reshape(1), pos.reshape(m, TOP_K)


def _to_row_tiles(x, dst_ref):
    rows = x.shape[0]
    for j in range(D_MODEL // LANES):
        dst_ref[pl.ds(j, rows, stride=ROW_TILE), :] = x[:, j * LANES:(j + 1) * LANES]


def _from_row_tiles(src_ref, base, rows):
    return jnp.concatenate(
        [src_ref[pl.ds(base * ROW_TILE + j, rows, stride=ROW_TILE), :] for j in range(D_MODEL // LANES)], axis=1)


def _start_row_gather(src_hbm, index, buf, slot, r, rows, sem, queue):
    dst = (slot * rows + r) * ROW_TILE
    pltpu.make_async_copy(src_hbm.at[pl.ds(index * ROW_TILE, ROW_TILE)], buf.at[pl.ds(dst, ROW_TILE)],
                          sem.at[slot]).start(priority=queue)


def _wait_row_gather(src_hbm, buf, slot, rows, sem):
    n = rows * ROW_TILE
    pltpu.make_async_copy(src_hbm.at[pl.ds(0, n)], buf.at[pl.ds(slot * n, n)], sem.at[slot]).wait()


def _push_rows_kernel(pos_ref, h_ref, g_ref, xs_hbm, stage, sem, *, tq):
    j = pl.program_id(0)
    last = pl.num_programs(0) - 1
    slot = lax.rem(j, 2)
    n = tq * ROW_TILE

    def wait_slot(s):
        for _ in range(TOP_K):
            pltpu.make_async_copy(stage.at[pl.ds(s * n, n)], xs_hbm.at[pl.ds(0, n)], sem.at[s]).wait()

    for s in range(2):
        @pl.when(slot == s)
        def _():
            @pl.when(j >= 2)
            def _():
                wait_slot(s)

            _to_row_tiles(_rms_unit(h_ref[...]) * g_ref[...], stage.at[pl.ds(s * n, n)])
            for r in range(tq):
                src = stage.at[pl.ds(s * n + r * ROW_TILE, ROW_TILE)]
                for k in range(TOP_K):
                    pltpu.make_async_copy(src, xs_hbm.at[pl.ds(pos_ref[0, k, r] * ROW_TILE, ROW_TILE)],
                                          sem.at[s]).start(priority=k)

            @pl.when(j == last)
            def _():
                wait_slot(s)

                @pl.when(j >= 1)
                def _():
                    wait_slot(1 - s)


def _push_rows(h, gain, pos, tq):
    m = pos.shape[0]
    n_tiles = m // tq
    pos3 = pos.reshape(n_tiles, tq, TOP_K).transpose(0, 2, 1)
    return pl.pallas_call(
        functools.partial(_push_rows_kernel, tq=tq),
        grid=(n_tiles,),
        in_specs=[
            pl.BlockSpec((1, TOP_K, tq), lambda j: (j, 0, 0), memory_space=pltpu.SMEM),
            pl.BlockSpec((tq, D_MODEL), lambda j: (j, 0)),
            pl.BlockSpec((1, D_MODEL), lambda j: (0, 0)),
        ],
        out_specs=pl.BlockSpec(memory_space=pl.ANY),
        out_shape=jax.ShapeDtypeStruct((TOP_K * m * ROW_TILE, LANES), _F32),
        scratch_shapes=[pltpu.VMEM((2 * tq * ROW_TILE, LANES), _F32), pltpu.SemaphoreType.DMA((2,))],
        compiler_params=_params(("arbitrary",)),
        name="push_rows",
    )(pos3, h, gain.reshape(1, D_MODEL))


def _expert_kernel(vt_ref, ve_ref, lo_ref, hi_ref, nv_ref, x_ref, wg_ref, wu_ref, wd_ref, out_ref,
                   u_sc, acc_sc, *, tm):
    del ve_ref
    v = pl.program_id(0)
    c = pl.program_id(1)
    active = v < nv_ref[0]
    first = jnp.logical_or(v == 0, vt_ref[v] != vt_ref[jnp.maximum(v - 1, 0)])

    @pl.when(active)
    def _():
        @pl.when(c == 0)
        def _():
            u_sc[...] = _from_row_tiles(x_ref, 0, tm).astype(_BF16)

        y = _swiglu_chunk(u_sc[...], wg_ref[...], wu_ref[...], wd_ref[...])

        @pl.when(c == 0)
        def _():
            acc_sc[...] = y

        @pl.when(c > 0)
        def _():
            acc_sc[...] += y

        @pl.when(c == pl.num_programs(1) - 1)
        def _():
            row = lax.broadcasted_iota(jnp.int32, (tm, 1), 0)
            mine = jnp.where((row >= lo_ref[v]) & (row < hi_ref[v]), acc_sc[...], 0.0)

            @pl.when(first)
            def _():
                _to_row_tiles(mine, out_ref)

            @pl.when(jnp.logical_not(first))
            def _():
                _to_row_tiles(_from_row_tiles(out_ref, 0, tm) + mine, out_ref)


def _expert_layer(xs, visit_tile, visit_expert, visit_lo, visit_hi, n_visits, w_gu, w_down, tm, tf):
    f = w_down.shape[1]
    nc = f // tf
    tile = lambda v, c, vt, ve, lo, hi, nv: (vt[v], 0)
    grid_spec = pltpu.PrefetchScalarGridSpec(
        num_scalar_prefetch=5,
        grid=(visit_tile.shape[0], nc),
        in_specs=[
            pl.BlockSpec((tm * ROW_TILE, LANES), tile),
            pl.BlockSpec((None, D_MODEL, tf), lambda v, c, vt, ve, lo, hi, nv: (ve[v], 0, c)),
            pl.BlockSpec((None, D_MODEL, tf), lambda v, c, vt, ve, lo, hi, nv: (ve[v], 0, nc + c)),
            pl.BlockSpec((None, tf, D_MODEL), lambda v, c, vt, ve, lo, hi, nv: (ve[v], c, 0)),
        ],
        out_specs=pl.BlockSpec((tm * ROW_TILE, LANES), tile),
        scratch_shapes=[pltpu.VMEM((tm, D_MODEL), _BF16), pltpu.VMEM((tm, D_MODEL), _F32)],
    )
    return pl.pallas_call(
        functools.partial(_expert_kernel, tm=tm),
        grid_spec=grid_spec,
        out_shape=jax.ShapeDtypeStruct(xs.shape, _F32),
        compiler_params=_params(("arbitrary", "arbitrary")),
        name="swiglu_routed",
    )(visit_tile, visit_expert, visit_lo, visit_hi, n_visits, xs, w_gu, w_gu, w_down)


def _combine_kernel(pos_ref, pos_next_ref, ys_hbm, h_ref, g1_ref, g2_ref, gf_ref, out_ref, buf, sem, *, tq):
    j = pl.program_id(0)
    slot = lax.rem(j, 2)
    rows = TOP_K * tq

    def start_gather(pos, dst_slot):
        for r in range(tq):
            for k in range(TOP_K):
                _start_row_gather(ys_hbm, pos[0, k, r], buf, dst_slot, k * tq + r, rows, sem, k)

    @pl.when(j == 0)
    def _():
        start_gather(pos_ref, 0)

    for s in range(2):
        @pl.when((j + 1 < pl.num_programs(0)) & (slot == s))
        def _():
            start_gather(pos_next_ref, 1 - s)

    rep = D_MODEL // LANES
    for s in range(2):
        @pl.when(slot == s)
        def _():
            _wait_row_gather(ys_hbm, buf, s, rows, sem)
            y = (jnp.tile(g1_ref[...], (1, rep)) * _from_row_tiles(buf, s * rows, tq)
                 + jnp.tile(g2_ref[...], (1, rep)) * _from_row_tiles(buf, s * rows + tq, tq))
            out_ref[...] = _rms_unit(h_ref[...] + y) * gf_ref[...]


def _moe_combine(ys, h, g1, g2, pos, final_gain, tq, tile_off, n_tiles):
    pos3 = pos[tile_off * tq:(tile_off + n_tiles) * tq].reshape(n_tiles, tq, TOP_K).transpose(0, 2, 1)
    smem_pos = lambda index_map: pl.BlockSpec((1, TOP_K, tq), index_map, memory_space=pltpu.SMEM)
    tok = lambda width: pl.BlockSpec((tq, width), lambda j: (j + tile_off, 0))
    return pl.pallas_call(
        functools.partial(_combine_kernel, tq=tq),
        grid=(n_tiles,),
        in_specs=[
            smem_pos(lambda j: (j, 0, 0)),
            smem_pos(lambda j: (jnp.minimum(j + 1, n_tiles - 1), 0, 0)),
            pl.BlockSpec(memory_space=pl.ANY),
            tok(D_MODEL), tok(LANES), tok(LANES),
            pl.BlockSpec((1, D_MODEL), lambda j: (0, 0)),
        ],
        out_specs=pl.BlockSpec((tq, D_MODEL), lambda j: (j, 0)),
        out_shape=jax.ShapeDtypeStruct((n_tiles * tq, D_MODEL), _F32),
        scratch_shapes=[pltpu.VMEM((2 * TOP_K * tq * ROW_TILE, LANES), _F32), pltpu.SemaphoreType.DMA((2,))],
        compiler_params=_params(("arbitrary",)),
        name="moe_combine",
    )(pos3, pos3, ys, h, g1, g2, final_gain.reshape(1, D_MODEL))


def kernel(x_prompt, x_sample, state_pool, cache_k_win, cache_v_win, norm_mix, norm_ffn, pool_w, pool_scale,
           norm_kv, w_kv, w_q, w_o, w_ffn_gu, w_ffn_down, w_router, w_exp_gu, w_exp_down, norm_final):
    nb, seq, _ = x_prompt.shape
    ns, ts, _ = x_sample.shape
    n_ctx = cache_k_win.shape[1]
    mp, ms = nb * seq, ns * ts
    w_kv_b = w_kv.astype(_BF16)
    w_q_b = w_q[0].astype(_BF16)
    w_o_b = w_o[0].astype(_BF16)
    w_ffn_gu_b = w_ffn_gu[0].astype(_BF16)
    w_ffn_down_b = w_ffn_down[0].astype(_BF16)
    tf_ffn = w_ffn_down_b.shape[0] // 2
    tf_exp = w_exp_down.shape[2] // 2

    dils = tuple(d for _, d in ATTN_GROUPS)
    h, pool_prompt = _pool_layer(x_prompt, None, norm_mix[0], pool_w[0], pool_scale[0], tm=TM_POOL, bn=1)
    h = _ffn_layer(h.reshape(mp, D_MODEL), norm_ffn[0], w_ffn_gu_b, w_ffn_down_b, TM_FFN, tf_ffn)
    keep = min(MAX_WINDOW, seq)
    q, k, v, k_win_p, v_win_p, (w_exp_gu_b, w_exp_down_b) = _qkv_layer(
        h.reshape(nb, seq, D_MODEL), jnp.arange(seq, dtype=jnp.int32), norm_kv, norm_mix[1], w_kv_b, w_q_b,
        tm=TM_QKV, keep=keep, q_dils=dils, cast=(w_exp_gu[0], w_exp_down[0]))
    outs, lses = [], []
    for g, d in enumerate(dils):
        o, lse = _band_attention(q[g], k[d], v[d], g, tb=min(ATTN_BANDS, seq // d // BAND))
        outs.append(o)
        lses.append(lse)

    hs, pool_sample = _pool_layer(x_sample, state_pool[0], norm_mix[0], pool_w[0], pool_scale[0], tm=ts, bn=8)
    hs = _ffn_layer(hs.reshape(ms, D_MODEL), norm_ffn[0], w_ffn_gu_b, w_ffn_down_b, ms, tf_ffn)
    pos_s = jnp.tile(PAST_LEN + jnp.arange(ts, dtype=jnp.int32), ns)
    qs, ks, vs, ks_f, vs_f, _ = _qkv_layer(hs.reshape(1, ms, D_MODEL), pos_s, norm_kv, norm_mix[1], w_kv_b, w_q_b,
                                           tm=ms, keep=ms, q_dils=(1,) * N_ATTN_GROUPS)
    ckt = cache_k_win.transpose(0, 2, 3, 1).reshape(ns, D_MODEL, n_ctx)
    cvt = cache_v_win.transpose(0, 2, 3, 1).reshape(ns, D_MODEL, n_ctx)
    per_sample = lambda a: a.reshape(ns, ts, D_MODEL)
    os_, kwt, vwt = _cached_attention([per_sample(a) for a in qs], per_sample(ks[1]), per_sample(vs[1]),
                                      per_sample(ks_f), per_sample(vs_f), ckt, cvt, chunk=CACHE_CHUNK)
    h_all, idx_tile, g1, g2 = _merge_proj(outs, lses, h, os_.reshape(ms, D_MODEL), hs, w_o_b, norm_ffn[1],
                                          w_router[0], tm=TM_TOKEN)

    visit_tile, visit_expert, visit_lo, visit_hi, n_visits, pos = _routing_tables(idx_tile, TM_EXPERT)
    xs = _push_rows(h_all, norm_ffn[1], pos, TM_TOKEN)
    ys = _expert_layer(xs, visit_tile, visit_expert, visit_lo, visit_hi, n_visits,
                       w_exp_gu_b, w_exp_down_b, TM_EXPERT, tf_exp)
    y_prompt = _moe_combine(ys, h_all, g1, g2, pos, norm_final, TM_TOKEN, 0, mp // TM_TOKEN)
    y_sample = _moe_combine(ys, h_all, g1, g2, pos, norm_final, TM_TOKEN, mp // TM_TOKEN, ms // TM_TOKEN)

    heads = lambda a: a.reshape(a.shape[0], a.shape[1], N_HEADS, HEAD_DIM)
    heads_t = lambda a: a.reshape(ns, N_HEADS, HEAD_DIM, a.shape[2]).transpose(0, 3, 1, 2)
    return (y_prompt.reshape(nb, seq, D_MODEL), y_sample.reshape(ns, ts, D_MODEL), pool_prompt, pool_sample,
            heads(k_win_p), heads(v_win_p), heads_t(kwt), heads_t(vwt))
```

```python
import functools
import math

import numpy as np
import jax
import jax.numpy as jnp
from jax import lax
from jax.experimental import pallas as pl
from jax.experimental.pallas import tpu as pltpu

D_MODEL = 1024
PAST_LEN = 16384
POOL_WINDOWS = (2, 4, 8, 16)
POOL_GROUP_DIM = D_MODEL // len(POOL_WINDOWS)
POOL_CTX = max(POOL_WINDOWS) - 1
HEAD_DIM = 64
N_HEADS = D_MODEL // HEAD_DIM
ATTN_GROUPS = ((128, 1), (512, 4), (2048, 16))
N_ATTN_GROUPS = len(ATTN_GROUPS)
BAND = 128
MAX_WINDOW = max(w for w, _ in ATTN_GROUPS)
ROT_DIM = HEAD_DIM // 4
ROPE_THETA = 500000.0
ATTN_SCALE = HEAD_DIM ** -0.5
N_EXPERTS = 8
TOP_K = 2
RMS_EPS = 1e-5

LANES = 128
HALO = 16
POOL_FRONT = 8
MASK_VALUE = -1e30
M_INIT = -1e20
VMEM_LIMIT = 56 * 1024 * 1024

TM_POOL = 512
TM_FFN = 512
TM_QKV = 256
TM_TOKEN = 256
TM_EXPERT = 512
MXU_COLS = 256
ROW_TILE = 8
ATTN_BANDS = 4
CACHE_CHUNK = 512

_F32 = jnp.float32
_BF16 = jnp.bfloat16
_NT = (((1,), (1,)), ((), ()))


def _params(semantics):
    return pltpu.CompilerParams(dimension_semantics=semantics, vmem_limit_bytes=VMEM_LIMIT)


def _rms_unit(x):
    return x * lax.rsqrt(jnp.mean(x * x, axis=-1, keepdims=True) + RMS_EPS)


def _dot(a, b):
    return jnp.dot(a, b, preferred_element_type=_F32)


def _swiglu_chunk(u, wg, wu, wd):
    g = _dot(u, wg)
    up = _dot(u, wu)
    return _dot((g * jax.nn.sigmoid(g) * up).astype(_BF16), wd)


def _pool_kernel(*refs, tm, n_ctx, bn):
    if n_ctx:
        x_ref, ctx_ref, g_ref, w_ref, sc_ref, h_ref, pool_ref, ext_ref, lvl_ref = refs
    else:
        x_ref, g_ref, w_ref, sc_ref, h_ref, pool_ref, ext_ref, lvl_ref = refs
        ctx_ref = None
    s = pl.program_id(1)
    pos = n_ctx + s * tm + lax.broadcasted_iota(jnp.int32, (tm, 1), 0)
    t0 = POOL_FRONT + HALO
    end = t0 + tm
    lvl_ref[:, 0:POOL_FRONT, :] = jnp.zeros((lvl_ref.shape[0], POOL_FRONT, POOL_GROUP_DIM), _F32)

    def window_sum(b, sl, w):
        read = lambda lo, hi: ext_ref[b, lo:hi, sl]
        shift = 1
        for level in range(int(math.log2(w)) - 1):
            lvl_ref[level, POOL_FRONT:end, :] = read(POOL_FRONT, end) + read(POOL_FRONT - shift, end - shift)
            read = functools.partial(lambda lv, lo, hi: lvl_ref[lv, lo:hi, :], level)
            shift *= 2
        return read(t0, end) + read(t0 - shift, end - shift)

    for b in range(bn):
        x = x_ref[b]
        u = _rms_unit(x) * g_ref[...]

        @pl.when(s == 0)
        def _():
            ext_ref[b, 0:t0 - n_ctx, :] = jnp.zeros((t0 - n_ctx, D_MODEL), _F32)
            if n_ctx:
                ext_ref[b, t0 - n_ctx:t0, :] = ctx_ref[b]

        ext_ref[b, t0:end, :] = u
        for g, w in enumerate(POOL_WINDOWS):
            sl = slice(g * POOL_GROUP_DIM, (g + 1) * POOL_GROUP_DIM)
            cnt = jnp.minimum(pos + 1, w).astype(_F32)
            d = window_sum(b, sl, w) / cnt - u[:, sl]
            y = _dot(d.astype(_BF16), w_ref[g])
            h_ref[b, :, sl] = x[:, sl] + y * sc_ref[:, sl]

        @pl.when(s == pl.num_programs(1) - 1)
        def _():
            pool_ref[0, b] = ext_ref[b, end - POOL_CTX:end, :]

        ext_ref[b, POOL_FRONT:t0, :] = ext_ref[b, POOL_FRONT + tm:end, :]


def _pool_layer(x, ctx, gain, w_pool, scale, tm, bn):
    n, t, _ = x.shape
    n_ctx = 0 if ctx is None else POOL_CTX
    grid = (n // bn, t // tm)
    row = lambda i, s: (i, s, 0)
    const2 = lambda i, s: (0, 0)
    in_specs = [pl.BlockSpec((bn, tm, D_MODEL), row)]
    args = [x]
    if ctx is not None:
        in_specs.append(pl.BlockSpec((bn, POOL_CTX, D_MODEL), lambda i, s: (i, 0, 0)))
        args.append(ctx)
    in_specs += [
        pl.BlockSpec((1, D_MODEL), const2),
        pl.BlockSpec((len(POOL_WINDOWS), POOL_GROUP_DIM, POOL_GROUP_DIM), lambda i, s: (0, 0, 0)),
        pl.BlockSpec((1, D_MODEL), const2),
    ]
    args += [gain.reshape(1, D_MODEL), w_pool.astype(_BF16), scale.reshape(1, D_MODEL)]
    return pl.pallas_call(
        functools.partial(_pool_kernel, tm=tm, n_ctx=n_ctx, bn=bn),
        grid=grid,
        in_specs=in_specs,
        out_specs=[
            pl.BlockSpec((bn, tm, D_MODEL), row),
            pl.BlockSpec((1, bn, POOL_CTX, D_MODEL), lambda i, s: (0, i, 0, 0)),
        ],
        out_shape=[
            jax.ShapeDtypeStruct((n, t, D_MODEL), _F32),
            jax.ShapeDtypeStruct((1, n, POOL_CTX, D_MODEL), _F32),
        ],
        scratch_shapes=[
            pltpu.VMEM((bn, POOL_FRONT + HALO + tm, D_MODEL), _F32),
            pltpu.VMEM((int(math.log2(max(POOL_WINDOWS))) - 1, POOL_FRONT + HALO + tm, POOL_GROUP_DIM), _F32),
        ],
        compiler_params=_params(("parallel", "arbitrary")),
        name="pool_layer",
    )(*args)


def _ffn_kernel(h_ref, g_ref, wg_ref, wu_ref, wd_ref, out_ref, u_sc, acc_sc):
    c = pl.program_id(1)

    @pl.when(c == 0)
    def _():
        u_sc[...] = (_rms_unit(h_ref[...]) * g_ref[...]).astype(_BF16)
        acc_sc[...] = jnp.zeros_like(acc_sc)

    acc_sc[...] += _swiglu_chunk(u_sc[...], wg_ref[...], wu_ref[...], wd_ref[...])

    @pl.when(c == pl.num_programs(1) - 1)
    def _():
        out_ref[...] = h_ref[...] + acc_sc[...]


def _ffn_layer(h, gain, w_gu, w_down, tm, tf):
    m = h.shape[0]
    f = w_down.shape[0]
    nc = f // tf
    row = lambda i, c: (i, 0)
    return pl.pallas_call(
        _ffn_kernel,
        grid=(m // tm, nc),
        in_specs=[
            pl.BlockSpec((tm, D_MODEL), row),
            pl.BlockSpec((1, D_MODEL), lambda i, c: (0, 0)),
            pl.BlockSpec((D_MODEL, tf), lambda i, c: (0, c)),
            pl.BlockSpec((D_MODEL, tf), lambda i, c: (0, nc + c)),
            pl.BlockSpec((tf, D_MODEL), lambda i, c: (c, 0)),
        ],
        out_specs=pl.BlockSpec((tm, D_MODEL), row),
        out_shape=jax.ShapeDtypeStruct((m, D_MODEL), _F32),
        scratch_shapes=[pltpu.VMEM((tm, D_MODEL), _BF16), pltpu.VMEM((tm, D_MODEL), _F32)],
        compiler_params=_params(("parallel", "arbitrary")),
        name="swiglu_dense",
    )(h, gain.reshape(1, D_MODEL), w_gu, w_gu, w_down)


def _rope_tables(pos):
    half = ROT_DIM // 2
    inv_freq = jnp.exp(-math.log(ROPE_THETA) * jnp.arange(half, dtype=_F32) / half)
    ang = pos.astype(_F32)[:, None] * inv_freq[None, :]
    cos, sin = jnp.cos(ang), jnp.sin(ang)
    t = pos.shape[0]
    rest = HEAD_DIM - ROT_DIM
    zero_h = jnp.zeros((t, half), _F32)
    zero_r = jnp.zeros((t, rest), _F32)
    c = jnp.concatenate([cos, cos, jnp.ones((t, rest), _F32)], axis=1)
    s1 = jnp.concatenate([zero_h, sin, zero_r], axis=1)
    s2 = jnp.concatenate([-sin, zero_h, zero_r], axis=1)
    rep = LANES // HEAD_DIM
    return tuple(jnp.tile(a, (1, rep)) for a in (c, s1, s2))


def _store_by_class(t, j, scr, slot, targets, tm):
    sl = slice(j * LANES, (j + 1) * LANES)
    if any(d > 1 for d, _ in targets):
        scr[slot] = t
    for d, ref in targets:
        if d == 1:
            ref[0, :, sl] = t.astype(_BF16)
        else:
            for r in range(d):
                ref[r, :, sl] = scr[slot, pl.ds(r, tm // d, stride=d), :].astype(_BF16)


def _qkv_kernel(*refs, q_dils, kv_dils, tm, n_cast):
    h_ref, gkv_ref, gq_ref, wkv_ref, wq_ref, c_ref, s1_ref, s2_ref = refs[:8]
    rest = list(refs[8:])
    cast_in = [rest.pop(0) for _ in range(n_cast)]
    q_refs = [rest.pop(0) for _ in q_dils]
    k_refs = [rest.pop(0) for _ in kv_dils]
    v_refs = [rest.pop(0) for _ in kv_dils]
    kf_ref, vf_ref = rest.pop(0), rest.pop(0)
    cast_out = [rest.pop(0) for _ in range(n_cast)]
    q_scr, k_scr, v_scr = rest
    hn = _rms_unit(h_ref[...])
    ukv = (hn * gkv_ref[...]).astype(_BF16)
    uq = (hn * gq_ref[...]).astype(_BF16)
    c, s1, s2 = c_ref[...], s1_ref[...], s2_ref[...]
    half = ROT_DIM // 2
    tiles = D_MODEL // LANES
    cols = MXU_COLS // LANES

    def rope(t):
        return t * c + pltpu.roll(t, half, 1) * s1 + pltpu.roll(t, LANES - half, 1) * s2

    for jb in range(2 * tiles // cols):
        blk = _dot(ukv, wkv_ref[:, jb * MXU_COLS:(jb + 1) * MXU_COLS])
        for jj in range(cols):
            j = jb * cols + jj
            t = blk[:, jj * LANES:(jj + 1) * LANES]
            if j < tiles:
                t = rope(t)
                kf_ref[:, j * LANES:(j + 1) * LANES] = t
                _store_by_class(t, j, k_scr, j, list(zip(kv_dils, k_refs)), tm)
            else:
                j -= tiles
                vf_ref[:, j * LANES:(j + 1) * LANES] = t
                _store_by_class(t, j, v_scr, j, list(zip(kv_dils, v_refs)), tm)
    for g, d in enumerate(q_dils):
        for jb in range(tiles // cols):
            c0 = g * D_MODEL + jb * MXU_COLS
            blk = _dot(uq, wq_ref[:, c0:c0 + MXU_COLS])
            for jj in range(cols):
                j = jb * cols + jj
                t = rope(blk[:, jj * LANES:(jj + 1) * LANES]) * ATTN_SCALE
                _store_by_class(t, j, q_scr, g * tiles + j, [(d, q_refs[g])], tm)
    for src, dst in zip(cast_in, cast_out):
        dst[...] = src[...].astype(_BF16)


def _qkv_layer(h, pos, norm_kv, norm_q, w_kv, w_q, tm, keep, q_dils, cast=()):
    n, t, _ = h.shape
    off = (t - keep) // tm
    kv_dils = tuple(sorted(set(q_dils)))
    tables = _rope_tables(pos)
    const2 = lambda i, s: (0, 0)
    tab = pl.BlockSpec((tm, LANES), lambda i, s: (s, 0))
    win = lambda i, s: (i, jnp.maximum(s - off, 0), 0)
    by_class = lambda d: pl.BlockSpec((None, d, tm // d, D_MODEL), lambda i, s: (i, 0, s, 0))
    class_shape = lambda d: jax.ShapeDtypeStruct((n, d, t // d, D_MODEL), _BF16)
    out_dils = tuple(q_dils) + kv_dils + kv_dils
    tiles = D_MODEL // LANES
    per_n = t // tm
    steps = n * per_n

    def slab(a):
        e, r, c = a.shape
        per_e = steps // e
        assert steps % e == 0 and r % per_e == 0 and (r // per_e) % 16 == 0
        return pl.BlockSpec((None, r // per_e, c),
                            lambda i, s: ((i * per_n + s) // per_e, (i * per_n + s) % per_e, 0))

    res = pl.pallas_call(
        functools.partial(_qkv_kernel, q_dils=tuple(q_dils), kv_dils=kv_dils, tm=tm, n_cast=len(cast)),
        grid=(n, per_n),
        in_specs=[
            pl.BlockSpec((None, tm, D_MODEL), lambda i, s: (i, s, 0)),
            pl.BlockSpec((1, D_MODEL), const2),
            pl.BlockSpec((1, D_MODEL), const2),
            pl.BlockSpec((D_MODEL, 2 * D_MODEL), const2, pipeline_mode=pl.Buffered(1)),
            pl.BlockSpec((D_MODEL, len(q_dils) * D_MODEL), const2, pipeline_mode=pl.Buffered(1)),
            tab, tab, tab,
        ] + [slab(a) for a in cast],
        out_specs=([by_class(d) for d in out_dils] + [pl.BlockSpec((None, tm, D_MODEL), win)] * 2
                   + [slab(a) for a in cast]),
        out_shape=([class_shape(d) for d in out_dils] + [jax.ShapeDtypeStruct((n, keep, D_MODEL), _F32)] * 2
                   + [jax.ShapeDtypeStruct(a.shape, _BF16) for a in cast]),
        scratch_shapes=[
            pltpu.VMEM((len(q_dils) * tiles, tm, LANES), _F32),
            pltpu.VMEM((tiles, tm, LANES), _F32),
            pltpu.VMEM((tiles, tm, LANES), _F32),
        ],
        compiler_params=_params(("parallel", "arbitrary")),
        name="qkv_rope",
    )(h, norm_kv.reshape(1, D_MODEL), norm_q.reshape(1, D_MODEL), w_kv, w_q, *tables, *cast)
    ng, nk = len(q_dils), len(kv_dils)
    qs = res[:ng]
    ks = dict(zip(kv_dils, res[ng:ng + nk]))
    vs = dict(zip(kv_dils, res[ng + nk:ng + 2 * nk]))
    kf, vf = res[ng + 2 * nk], res[ng + 2 * nk + 1]
    return qs, ks, vs, kf, vf, tuple(res[ng + 2 * nk + 2:])


def _band_attn_kernel(q_ref, kh_ref, k_ref, vh_ref, v_ref, o_ref, lse_ref, kx_ref, vx_ref, *, tb):
    m_id = pl.program_id(2)
    kx_ref[0:BAND, :] = kh_ref[...]
    kx_ref[BAND:, :] = k_ref[...]
    vx_ref[0:BAND, :] = vh_ref[...]
    vx_ref[BAND:, :] = v_ref[...]
    row = lax.broadcasted_iota(jnp.int32, (BAND, 2 * BAND), 0)
    col = lax.broadcasted_iota(jnp.int32, (BAND, 2 * BAND), 1)
    dist = row - col + BAND
    bias = jnp.where(dist >= 0, jnp.where(dist <= BAND, 0.0, MASK_VALUE), MASK_VALUE)
    no_prev = jnp.where(m_id > 0, 0.0, MASK_VALUE)
    bias_first = jnp.where(col < BAND, bias + no_prev, bias)
    lane = lax.broadcasted_iota(jnp.int32, (BAND, LANES), 1)
    low = lane < HEAD_DIM
    for b in range(tb):
        bb = bias_first if b == 0 else bias
        rows = slice(b * BAND, (b + 1) * BAND)
        lse_ref[rows, N_HEADS:] = jnp.zeros((BAND, LANES - N_HEADS), _F32)
        for p in range(D_MODEL // LANES):
            sl = slice(p * LANES, (p + 1) * LANES)
            q2 = q_ref[rows, sl].astype(_F32)
            k2 = kx_ref[b * BAND:(b + 2) * BAND, sl]
            v2 = vx_ref[b * BAND:(b + 2) * BAND, sl]
            outs = []
            for hh in range(2):
                qm = (jnp.where(low, q2, 0.0) if hh == 0 else jnp.where(low, 0.0, q2)).astype(_BF16)
                s = lax.dot_general(qm, k2, _NT, preferred_element_type=_F32) + bb
                mx = jnp.max(s, axis=1, keepdims=True)
                pr = jnp.exp(s - mx)
                den = jnp.sum(pr, axis=1, keepdims=True)
                o = _dot(pr.astype(_BF16), v2)
                outs.append(o * (1.0 / den))
                head = 2 * p + hh
                lse_ref[rows, head:head + 1] = mx + jnp.log(den)
            o_ref[rows, sl] = jnp.where(low, outs[0], outs[1]).astype(_BF16)


def _band_attention(q, k, v, group, tb):
    n, dil, length, _ = k.shape
    rows = tb * BAND
    cur = lambda i, r, m: (i, r, m, 0)
    halo = lambda i, r, m: (i, r, jnp.maximum(m * tb - 1, 0), 0)
    band = pl.BlockSpec((None, None, rows, D_MODEL), cur)
    prev = pl.BlockSpec((None, None, BAND, D_MODEL), halo)
    return pl.pallas_call(
        functools.partial(_band_attn_kernel, tb=tb),
        grid=(n, dil, length // rows),
        in_specs=[band, prev, band, prev, band],
        out_specs=[band, pl.BlockSpec((None, None, rows, LANES), cur)],
        out_shape=[
            jax.ShapeDtypeStruct((n, dil, length, D_MODEL), _BF16),
            jax.ShapeDtypeStruct((n, dil, length, LANES), _F32),
        ],
        scratch_shapes=[pltpu.VMEM((BAND + rows, D_MODEL), _BF16), pltpu.VMEM((BAND + rows, D_MODEL), _BF16)],
        compiler_params=_params(("parallel", "parallel", "arbitrary")),
        name="band_attn_g%d" % group,
    )(q, k, k, v, v)


def _head_expand_matrix():
    e = np.zeros((LANES, N_ATTN_GROUPS * D_MODEL), np.float32)
    for part in range(2):
        for g in range(N_ATTN_GROUPS):
            for h in range(N_HEADS):
                r = part * N_ATTN_GROUPS * N_HEADS + g * N_HEADS + h
                e[r, g * D_MODEL + h * HEAD_DIM:g * D_MODEL + (h + 1) * HEAD_DIM] = 1.0
    return jnp.asarray(e, _BF16)


def _merge_proj_kernel(o0_ref, o1_ref, o2_ref, l0_ref, l1_ref, l2_ref, e_ref, h_ref, os_ref, hs_ref, wo_ref,
                       gr_ref, wrh_ref, wrl_ref, out_ref, idx_ref, g1_ref, g2_ref,
                       o1_scr, o2_scr, l1_scr, l2_scr, o_scr, *, n_merge, tm):
    i = pl.program_id(0)
    tiles = D_MODEL // LANES

    @pl.when(i < n_merge)
    def _():
        for o_ref, l_ref, o_dst, l_dst in ((o1_ref, l1_ref, o1_scr, l1_scr), (o2_ref, l2_ref, o2_scr, l2_scr)):
            d = o_ref.shape[0]
            for r in range(d):
                l_dst[pl.ds(r, tm // d, stride=d), :] = l_ref[r]
                for j in range(tiles):
                    o_dst[j, pl.ds(r, tm // d, stride=d), :] = o_ref[r, :, j * LANES:(j + 1) * LANES].astype(_F32)
        l0, l1, l2 = l0_ref[0], l1_scr[...], l2_scr[...]
        mx = jnp.maximum(jnp.maximum(l0, l1), l2)
        e0, e1, e2 = jnp.exp(l0 - mx), jnp.exp(l1 - mx), jnp.exp(l2 - mx)
        inv = 1.0 / (e0 + e1 + e2)
        lane = lax.broadcasted_iota(jnp.int32, l0.shape, 1)
        head = lane < N_HEADS
        a = (jnp.where(head, e0 * inv, 0.0)
             + pltpu.roll(jnp.where(head, e1 * inv, 0.0), N_HEADS, 1)
             + pltpu.roll(jnp.where(head, e2 * inv, 0.0), 2 * N_HEADS, 1))
        hi = a.astype(_BF16).astype(_F32)
        a2 = hi + pltpu.roll(a - hi, N_ATTN_GROUPS * N_HEADS, 1)
        w = _dot(a2.astype(_BF16), e_ref[...])
        for j in range(tiles):
            sl = slice(j * LANES, (j + 1) * LANES)
            o = (w[:, sl] * o0_ref[0, :, sl].astype(_F32)
                 + w[:, D_MODEL + j * LANES:D_MODEL + (j + 1) * LANES] * o1_scr[j]
                 + w[:, 2 * D_MODEL + j * LANES:2 * D_MODEL + (j + 1) * LANES] * o2_scr[j])
            o_scr[:, sl] = o.astype(_BF16)
        out_ref[...] = h_ref[...] + _dot(o_scr[...], wo_ref[...])

    @pl.when(i >= n_merge)
    def _():
        out_ref[...] = hs_ref[...] + _dot(os_ref[...].astype(_BF16), wo_ref[...])

    _route(out_ref[...], gr_ref[...], wrh_ref[...], wrl_ref[...], idx_ref, g1_ref, g2_ref)


def _merge_proj(outs, lses, h, o_sample, h_sample, w_o, route_gain, w_router, tm):
    m, ms = h.shape[0], h_sample.shape[0]
    n, _, seq, _ = outs[0].shape
    n_merge = m // tm
    per_n = seq // tm
    row = lambda i: (jnp.minimum(i, n_merge - 1), 0)
    srow = lambda i: (jnp.maximum(i - n_merge, 0), 0)
    const = lambda i: (0, 0)

    def by_class(a):
        d, width = a.shape[1], a.shape[3]

        def index(i):
            ic = jnp.minimum(i, n_merge - 1)
            return (ic // per_n, 0, ic % per_n, 0)

        return pl.BlockSpec((None, d, tm // d, width), index)

    tiles = D_MODEL // LANES
    wr = jnp.zeros((D_MODEL, LANES), _F32).at[:, :N_EXPERTS].set(w_router)
    wr_hi = wr.astype(_BF16)
    wr_lo = (wr - wr_hi.astype(_F32)).astype(_BF16)
    tile_out = lambda width: pl.BlockSpec((tm, width), lambda i: (i, 0))
    tile_shape = lambda width: jax.ShapeDtypeStruct((m + ms, width), _F32)
    return pl.pallas_call(
        functools.partial(_merge_proj_kernel, n_merge=n_merge, tm=tm),
        grid=((m + ms) // tm,),
        in_specs=[by_class(a) for a in outs] + [by_class(a) for a in lses] + [
            pl.BlockSpec((LANES, N_ATTN_GROUPS * D_MODEL), const),
            pl.BlockSpec((tm, D_MODEL), row),
            pl.BlockSpec((tm, D_MODEL), srow),
            pl.BlockSpec((tm, D_MODEL), srow),
            pl.BlockSpec((D_MODEL, D_MODEL), const),
            pl.BlockSpec((1, D_MODEL), const),
            pl.BlockSpec((D_MODEL, LANES), const),
            pl.BlockSpec((D_MODEL, LANES), const),
        ],
        out_specs=[tile_out(D_MODEL), tile_out(LANES), tile_out(LANES), tile_out(LANES)],
        out_shape=[tile_shape(D_MODEL), tile_shape(LANES), tile_shape(LANES), tile_shape(LANES)],
        scratch_shapes=[
            pltpu.VMEM((tiles, tm, LANES), _F32),
            pltpu.VMEM((tiles, tm, LANES), _F32),
            pltpu.VMEM((tm, LANES), _F32),
            pltpu.VMEM((tm, LANES), _F32),
            pltpu.VMEM((tm, D_MODEL), _BF16),
        ],
        compiler_params=_params(("parallel",)),
        name="merge_out_proj",
    )(*outs, *lses, _head_expand_matrix(), h, o_sample, h_sample, w_o, route_gain.reshape(1, D_MODEL), wr_hi, wr_lo)


def _cached_bias(t, n_ctx):
    rows = N_ATTN_GROUPS * N_HEADS * t
    cache = np.full((rows, n_ctx), MASK_VALUE, np.float32)
    new = np.full((rows, LANES), MASK_VALUE, np.float32)
    for g, (win, dil) in enumerate(ATTN_GROUPS):
        for j in range(t):
            idx = n_ctx + j - np.arange(win // dil + 1) * dil
            idx = idx[idx >= 0]
            for h in range(N_HEADS):
                r = (g * N_HEADS + h) * t + j
                cache[r, idx[idx < n_ctx]] = 0.0
                new[r, idx[idx >= n_ctx] - n_ctx] = 0.0
    return cache, new


def _cached_attn_kernel(q0_ref, q1_ref, q2_ref, kc_ref, vc_ref, kx_ref, vx_ref, kn_ref, vn_ref, knt_ref, vnt_ref,
                        bias_ref, biasn_ref, o_ref, kw_ref, vw_ref, qbd_ref, m_ref, l_ref, acc_ref,
                        *, t, first_chunk):
    c = pl.program_id(1)
    last = pl.num_programs(1) - 1
    gh = N_HEADS * t
    row = lax.broadcasted_iota(jnp.int32, (gh, D_MODEL), 0)
    lane = lax.broadcasted_iota(jnp.int32, (gh, D_MODEL), 1)
    own = jnp.right_shift(row, int(math.log2(t))) == jnp.right_shift(lane, int(math.log2(HEAD_DIM)))

    @pl.when(c == 0)
    def _():
        for g, q_ref in enumerate((q0_ref, q1_ref, q2_ref)):
            tiled = jnp.concatenate([q_ref[...].astype(_F32)] * N_HEADS, axis=0)
            qbd_ref[g * gh:(g + 1) * gh, :] = jnp.where(own, tiled, 0.0).astype(_BF16)
        m_ref[...] = jnp.full(m_ref.shape, M_INIT, _F32)
        l_ref[...] = jnp.zeros_like(l_ref)
        acc_ref[...] = jnp.zeros_like(acc_ref)

    def update(rows, s, weighted_values):
        m_old = m_ref[rows, :]
        m_new = jnp.maximum(m_old, jnp.max(s, axis=1, keepdims=True))
        alpha = jnp.exp(m_old - m_new)
        pr = jnp.exp(s - m_new)
        l_ref[rows, :] = alpha * l_ref[rows, :] + jnp.sum(pr, axis=1, keepdims=True)
        acc_ref[rows, :] = alpha * acc_ref[rows, :] + weighted_values(pr.astype(_BF16))
        m_ref[rows, :] = m_new

    kc, vc = kc_ref[...], vc_ref[...]
    kct, vct = kc.astype(_BF16), vc.astype(_BF16)
    spans = []
    for g in range(N_ATTN_GROUPS):
        if spans and spans[-1][2] == first_chunk[g]:
            spans[-1][1] = g + 1
        else:
            spans.append([g, g + 1, first_chunk[g]])
    for g0, g1, start in spans:
        rows = slice(g0 * gh, g1 * gh)

        @pl.when(c >= start)
        def _():
            update(rows, _dot(qbd_ref[rows, :], kct) + bias_ref[rows, :],
                   lambda pr: lax.dot_general(pr, vct, _NT, preferred_element_type=_F32))

    chunk = kc.shape[1]
    for cur, nxt, new_t, w_ref in ((kc, kx_ref, knt_ref, kw_ref), (vc, vx_ref, vnt_ref, vw_ref)):
        tail = jnp.where(c == last, new_t[...], nxt[...])
        ext = jnp.concatenate([cur, tail], axis=1)
        w_ref[...] = pltpu.roll(ext, chunk + LANES - t, 1)[:, 0:chunk]

    @pl.when(c == last)
    def _():
        pad = jnp.zeros((LANES - t, D_MODEL), _F32)
        kn = jnp.concatenate([kn_ref[...].astype(_F32), pad], axis=0).astype(_BF16)
        vn = jnp.concatenate([vn_ref[...].astype(_F32), pad], axis=0).astype(_BF16)
        update(slice(0, N_ATTN_GROUPS * gh),
               lax.dot_general(qbd_ref[...], kn, _NT, preferred_element_type=_F32) + biasn_ref[...],
               lambda pr: _dot(pr, vn))
        ms = [m_ref[g * gh:(g + 1) * gh, :] for g in range(N_ATTN_GROUPS)]
        mx = jnp.maximum(jnp.maximum(ms[0], ms[1]), ms[2])
        num = jnp.zeros((gh, D_MODEL), _F32)
        den = jnp.zeros((gh, 1), _F32)
        for g in range(N_ATTN_GROUPS):
            w = jnp.exp(ms[g] - mx)
            num = num + w * acc_ref[g * gh:(g + 1) * gh, :]
            den = den + w * l_ref[g * gh:(g + 1) * gh, :]
        on = jnp.where(own, num * (1.0 / den), 0.0)
        o = on[0:t, :]
        for h in range(1, N_HEADS):
            o = o + on[h * t:(h + 1) * t, :]
        o_ref[...] = o


def _cached_attention(qs, k_new, v_new, k_new_f32, v_new_f32, cache_kt, cache_vt, chunk):
    n, t, _ = k_new.shape
    n_ctx = cache_kt.shape[2]
    assert n_ctx == MAX_WINDOW and n_ctx % chunk == 0 and chunk % LANES == 0 and t <= LANES
    bias_np, bias_new = _cached_bias(t, n_ctx)
    gh = N_HEADS * t
    rows = N_ATTN_GROUPS * gh
    n_chunks = n_ctx // chunk
    first_chunk = tuple(
        min(c for c in range(n_chunks) if (bias_np[g * gh:(g + 1) * gh, c * chunk:(c + 1) * chunk] == 0.0).any())
        for g in range(N_ATTN_GROUPS))
    new_t = lambda a: jnp.pad(a.transpose(0, 2, 1), ((0, 0), (0, 0), (0, LANES - t)))
    per_n = lambda i, c: (i, 0, 0)
    cur = pl.BlockSpec((None, D_MODEL, chunk), lambda i, c: (i, 0, c))
    ahead = pl.BlockSpec((None, D_MODEL, LANES),
                         lambda i, c: (i, 0, jnp.minimum((c + 1) * (chunk // LANES), n_ctx // LANES - 1)))
    small = pl.BlockSpec((None, t, D_MODEL), per_n)
    small_t = pl.BlockSpec((None, D_MODEL, LANES), per_n)
    window = jax.ShapeDtypeStruct((n, D_MODEL, n_ctx), _F32)
    return pl.pallas_call(
        functools.partial(_cached_attn_kernel, t=t, first_chunk=first_chunk),
        grid=(n, n_chunks),
        in_specs=[
            small, small, small,
            cur, cur, ahead, ahead,
            small, small, small_t, small_t,
            pl.BlockSpec((rows, chunk), lambda i, c: (0, c)),
            pl.BlockSpec((rows, LANES), lambda i, c: (0, 0)),
        ],
        out_specs=[small, cur, cur],
        out_shape=[jax.ShapeDtypeStruct((n, t, D_MODEL), _F32), window, window],
        scratch_shapes=[
            pltpu.VMEM((rows, D_MODEL), _BF16),
            pltpu.VMEM((rows, 1), _F32),
            pltpu.VMEM((rows, 1), _F32),
            pltpu.VMEM((rows, D_MODEL), _F32),
        ],
        compiler_params=_params(("parallel", "arbitrary")),
        name="cached_attn",
    )(*qs, cache_kt, cache_vt, cache_kt, cache_vt, k_new, v_new, new_t(k_new_f32), new_t(v_new_f32),
      jnp.asarray(bias_np), jnp.asarray(bias_new))


def _split_bf16(x):
    hi = x.astype(_BF16)
    return hi, (x - hi.astype(_F32)).astype(_BF16)


def _route(h, gain, wr_hi, wr_lo, idx_ref, g1_ref, g2_ref):
    u_hi, u_lo = _split_bf16(_rms_unit(h) * gain)
    logits = _dot(u_hi, wr_hi) + (_dot(u_hi, wr_lo) + _dot(u_lo, wr_hi))
    lane = lax.broadcasted_iota(jnp.int32, logits.shape, 1).astype(_F32)
    neg = -jnp.inf
    lg = jnp.where(lane < N_EXPERTS, logits, neg)
    v1 = jnp.max(lg, axis=1, keepdims=True)
    i1 = jnp.min(jnp.where(lg == v1, lane, float(LANES)), axis=1, keepdims=True)
    lg2 = jnp.where(lane == i1, neg, lg)
    v2 = jnp.max(lg2, axis=1, keepdims=True)
    i2 = jnp.min(jnp.where(lg2 == v2, lane, float(LANES)), axis=1, keepdims=True)
    e2 = jnp.exp(v2 - v1)
    den = 1.0 + e2
    idx_ref[...] = jnp.where(lane < LANES // 2, i1, i2)
    g1_ref[...] = jnp.broadcast_to(1.0 / den, g1_ref.shape)
    g2_ref[...] = jnp.broadcast_to(e2 / den, g2_ref.shape)


def _routing_tables(idx_tile, tm):
    m = idx_tile.shape[0]
    n_rows = TOP_K * m
    assert n_rows % tm == 0
    n_tiles = n_rows // tm
    n_visits = n_tiles + N_EXPERTS - 1
    e = jnp.stack([idx_tile[:, 0], idx_tile[:, LANES // 2]], axis=1).reshape(n_rows).astype(jnp.int32)
    onehot = (e[:, None] == jnp.arange(N_EXPERTS, dtype=jnp.int32)[None, :]).astype(jnp.int32)
    running = jnp.cumsum(onehot, axis=0)
    counts = running[-1]
    rank = jnp.sum(onehot * running, axis=1) - 1
    ends = jnp.cumsum(counts)
    starts = ends - counts
    pos = jnp.sum(onehot * starts[None, :], axis=1) + rank
    tile_lo = jnp.arange(n_tiles, dtype=jnp.int32)[:, None] * tm
    lo = jnp.maximum(starts[None, :], tile_lo) - tile_lo
    hi = jnp.minimum(ends[None, :], tile_lo + tm) - tile_lo
    used = (hi > lo).reshape(-1)
    n_used = jnp.sum(used.astype(jnp.int32))
    flat = jnp.nonzero(used, size=n_visits, fill_value=0)[0].astype(jnp.int32)
    flat = jnp.where(jnp.arange(n_visits) < n_used, flat, jnp.take(flat, n_used - 1))
    live = jnp.arange(n_visits) < n_used
    v_lo = jnp.where(live, jnp.take(lo.reshape(-1), flat), 0)
    v_hi = jnp.where(live, jnp.take(hi.reshape(-1), flat), 0)
    return flat // N_EXPERTS, flat % N_EXPERTS, v_lo, v_hi, n_used.reshape(1), pos.reshape(m, TOP_K)


def _to_row_tiles(x, dst_ref):
    rows = x.shape[0]
    for j in range(D_MODEL // LANES):
        dst_ref[pl.ds(j, rows, stride=ROW_TILE), :] = x[:, j * LANES:(j + 1) * LANES]


def _from_row_tiles(src_ref, base, rows):
    return jnp.concatenate(
        [src_ref[pl.ds(base * ROW_TILE + j, rows, stride=ROW_TILE), :] for j in range(D_MODEL // LANES)], axis=1)


def _start_row_gather(src_hbm, index, buf, slot, r, rows, sem, queue):
    dst = (slot * rows + r) * ROW_TILE
    pltpu.make_async_copy(src_hbm.at[pl.ds(index * ROW_TILE, ROW_TILE)], buf.at[pl.ds(dst, ROW_TILE)],
                          sem.at[slot]).start(priority=queue)


def _wait_row_gather(src_hbm, buf, slot, rows, sem):
    n = rows * ROW_TILE
    pltpu.make_async_copy(src_hbm.at[pl.ds(0, n)], buf.at[pl.ds(slot * n, n)], sem.at[slot]).wait()


def _push_rows_kernel(pos_ref, h_ref, g_ref, xs_hbm, stage, sem, *, tq):
    j = pl.program_id(0)
    last = pl.num_programs(0) - 1
    slot = lax.rem(j, 2)
    n = tq * ROW_TILE

    def wait_slot(s):
        for _ in range(TOP_K):
            pltpu.make_async_copy(stage.at[pl.ds(s * n, n)], xs_hbm.at[pl.ds(0, n)], sem.at[s]).wait()

    for s in range(2):
        @pl.when(slot == s)
        def _():
            @pl.when(j >= 2)
            def _():
                wait_slot(s)

            _to_row_tiles(_rms_unit(h_ref[...]) * g_ref[...], stage.at[pl.ds(s * n, n)])
            for r in range(tq):
                src = stage.at[pl.ds(s * n + r * ROW_TILE, ROW_TILE)]
                for k in range(TOP_K):
                    pltpu.make_async_copy(src, xs_hbm.at[pl.ds(pos_ref[0, k, r] * ROW_TILE, ROW_TILE)],
                                          sem.at[s]).start(priority=k)

            @pl.when(j == last)
            def _():
                wait_slot(s)

                @pl.when(j >= 1)
                def _():
                    wait_slot(1 - s)


def _push_rows(h, gain, pos, tq):
    m = pos.shape[0]
    n_tiles = m // tq
    pos3 = pos.reshape(n_tiles, tq, TOP_K).transpose(0, 2, 1)
    return pl.pallas_call(
        functools.partial(_push_rows_kernel, tq=tq),
        grid=(n_tiles,),
        in_specs=[
            pl.BlockSpec((1, TOP_K, tq), lambda j: (j, 0, 0), memory_space=pltpu.SMEM),
            pl.BlockSpec((tq, D_MODEL), lambda j: (j, 0)),
            pl.BlockSpec((1, D_MODEL), lambda j: (0, 0)),
        ],
        out_specs=pl.BlockSpec(memory_space=pl.ANY),
        out_shape=jax.ShapeDtypeStruct((TOP_K * m * ROW_TILE, LANES), _F32),
        scratch_shapes=[pltpu.VMEM((2 * tq * ROW_TILE, LANES), _F32), pltpu.SemaphoreType.DMA((2,))],
        compiler_params=_params(("arbitrary",)),
        name="push_rows",
    )(pos3, h, gain.reshape(1, D_MODEL))


def _expert_kernel(vt_ref, ve_ref, lo_ref, hi_ref, nv_ref, x_ref, wg_ref, wu_ref, wd_ref, out_ref,
                   u_sc, acc_sc, *, tm):
    del ve_ref
    v = pl.program_id(0)
    c = pl.program_id(1)
    active = v < nv_ref[0]
    first = jnp.logical_or(v == 0, vt_ref[v] != vt_ref[jnp.maximum(v - 1, 0)])

    @pl.when(active)
    def _():
        @pl.when(c == 0)
        def _():
            u_sc[...] = _from_row_tiles(x_ref, 0, tm).astype(_BF16)

        y = _swiglu_chunk(u_sc[...], wg_ref[...], wu_ref[...], wd_ref[...])

        @pl.when(c == 0)
        def _():
            acc_sc[...] = y

        @pl.when(c > 0)
        def _():
            acc_sc[...] += y

        @pl.when(c == pl.num_programs(1) - 1)
        def _():
            row = lax.broadcasted_iota(jnp.int32, (tm, 1), 0)
            mine = jnp.where((row >= lo_ref[v]) & (row < hi_ref[v]), acc_sc[...], 0.0)

            @pl.when(first)
            def _():
                _to_row_tiles(mine, out_ref)

            @pl.when(jnp.logical_not(first))
            def _():
                _to_row_tiles(_from_row_tiles(out_ref, 0, tm) + mine, out_ref)


def _expert_layer(xs, visit_tile, visit_expert, visit_lo, visit_hi, n_visits, w_gu, w_down, tm, tf):
    f = w_down.shape[1]
    nc = f // tf
    tile = lambda v, c, vt, ve, lo, hi, nv: (vt[v], 0)
    grid_spec = pltpu.PrefetchScalarGridSpec(
        num_scalar_prefetch=5,
        grid=(visit_tile.shape[0], nc),
        in_specs=[
            pl.BlockSpec((tm * ROW_TILE, LANES), tile),
            pl.BlockSpec((None, D_MODEL, tf), lambda v, c, vt, ve, lo, hi, nv: (ve[v], 0, c)),
            pl.BlockSpec((None, D_MODEL, tf), lambda v, c, vt, ve, lo, hi, nv: (ve[v], 0, nc + c)),
            pl.BlockSpec((None, tf, D_MODEL), lambda v, c, vt, ve, lo, hi, nv: (ve[v], c, 0)),
        ],
        out_specs=pl.BlockSpec((tm * ROW_TILE, LANES), tile),
        scratch_shapes=[pltpu.VMEM((tm, D_MODEL), _BF16), pltpu.VMEM((tm, D_MODEL), _F32)],
    )
    return pl.pallas_call(
        functools.partial(_expert_kernel, tm=tm),
        grid_spec=grid_spec,
        out_shape=jax.ShapeDtypeStruct(xs.shape, _F32),
        compiler_params=_params(("arbitrary", "arbitrary")),
        name="swiglu_routed",
    )(visit_tile, visit_expert, visit_lo, visit_hi, n_visits, xs, w_gu, w_gu, w_down)


def _combine_kernel(pos_ref, pos_next_ref, ys_hbm, h_ref, g1_ref, g2_ref, gf_ref, out_ref, buf, sem, *, tq):
    j = pl.program_id(0)
    slot = lax.rem(j, 2)
    rows = TOP_K * tq

    def start_gather(pos, dst_slot):
        for r in range(tq):
            for k in range(TOP_K):
                _start_row_gather(ys_hbm, pos[0, k, r], buf, dst_slot, k * tq + r, rows, sem, k)

    @pl.when(j == 0)
    def _():
        start_gather(pos_ref, 0)

    for s in range(2):
        @pl.when((j + 1 < pl.num_programs(0)) & (slot == s))
        def _():
            start_gather(pos_next_ref, 1 - s)

    rep = D_MODEL // LANES
    for s in range(2):
        @pl.when(slot == s)
        def _():
            _wait_row_gather(ys_hbm, buf, s, rows, sem)
            y = (jnp.tile(g1_ref[...], (1, rep)) * _from_row_tiles(buf, s * rows, tq)
                 + jnp.tile(g2_ref[...], (1, rep)) * _from_row_tiles(buf, s * rows + tq, tq))
            out_ref[...] = _rms_unit(h_ref[...] + y) * gf_ref[...]


def _moe_combine(ys, h, g1, g2, pos, final_gain, tq, tile_off, n_tiles):
    pos3 = pos[tile_off * tq:(tile_off + n_tiles) * tq].reshape(n_tiles, tq, TOP_K).transpose(0, 2, 1)
    smem_pos = lambda index_map: pl.BlockSpec((1, TOP_K, tq), index_map, memory_space=pltpu.SMEM)
    tok = lambda width: pl.BlockSpec((tq, width), lambda j: (j + tile_off, 0))
    return pl.pallas_call(
        functools.partial(_combine_kernel, tq=tq),
        grid=(n_tiles,),
        in_specs=[
            smem_pos(lambda j: (j, 0, 0)),
            smem_pos(lambda j: (jnp.minimum(j + 1, n_tiles - 1), 0, 0)),
            pl.BlockSpec(memory_space=pl.ANY),
            tok(D_MODEL), tok(LANES), tok(LANES),
            pl.BlockSpec((1, D_MODEL), lambda j: (0, 0)),
        ],
        out_specs=pl.BlockSpec((tq, D_MODEL), lambda j: (j, 0)),
        out_shape=jax.ShapeDtypeStruct((n_tiles * tq, D_MODEL), _F32),
        scratch_shapes=[pltpu.VMEM((2 * TOP_K * tq * ROW_TILE, LANES), _F32), pltpu.SemaphoreType.DMA((2,))],
        compiler_params=_params(("arbitrary",)),
        name="moe_combine",
    )(pos3, pos3, ys, h, g1, g2, final_gain.reshape(1, D_MODEL))


def kernel(x_prompt, x_sample, state_pool, cache_k_win, cache_v_win, norm_mix, norm_ffn, pool_w, pool_scale,
           norm_kv, w_kv, w_q, w_o, w_ffn_gu, w_ffn_down, w_router, w_exp_gu, w_exp_down, norm_final):
    nb, seq, _ = x_prompt.shape
    ns, ts, _ = x_sample.shape
    n_ctx = cache_k_win.shape[1]
    mp, ms = nb * seq, ns * ts
    w_kv_b = w_kv.astype(_BF16)
    w_q_b = w_q[0].astype(_BF16)
    w_o_b = w_o[0].astype(_BF16)
    w_ffn_gu_b = w_ffn_gu[0].astype(_BF16)
    w_ffn_down_b = w_ffn_down[0].astype(_BF16)
    tf_ffn = w_ffn_down_b.shape[0] // 2
    tf_exp = w_exp_down.shape[2] // 2

    dils = tuple(d for _, d in ATTN_GROUPS)
    h, pool_prompt = _pool_layer(x_prompt, None, norm_mix[0], pool_w[0], pool_scale[0], tm=TM_POOL, bn=1)
    h = _ffn_layer(h.reshape(mp, D_MODEL), norm_ffn[0], w_ffn_gu_b, w_ffn_down_b, TM_FFN, tf_ffn)
    keep = min(MAX_WINDOW, seq)
    q, k, v, k_win_p, v_win_p, (w_exp_gu_b, w_exp_down_b) = _qkv_layer(
        h.reshape(nb, seq, D_MODEL), jnp.arange(seq, dtype=jnp.int32), norm_kv, norm_mix[1], w_kv_b, w_q_b,
        tm=TM_QKV, keep=keep, q_dils=dils, cast=(w_exp_gu[0], w_exp_down[0]))
    outs, lses = [], []
    for g, d in enumerate(dils):
        o, lse = _band_attention(q[g], k[d], v[d], g, tb=min(ATTN_BANDS, seq // d // BAND))
        outs.append(o)
        lses.append(lse)

    hs, pool_sample = _pool_layer(x_sample, state_pool[0], norm_mix[0], pool_w[0], pool_scale[0], tm=ts, bn=8)
    hs = _ffn_layer(hs.reshape(ms, D_MODEL), norm_ffn[0], w_ffn_gu_b, w_ffn_down_b, ms, tf_ffn)
    pos_s = jnp.tile(PAST_LEN + jnp.arange(ts, dtype=jnp.int32), ns)
    qs, ks, vs, ks_f, vs_f, _ = _qkv_layer(hs.reshape(1, ms, D_MODEL), pos_s, norm_kv, norm_mix[1], w_kv_b, w_q_b,
                                           tm=ms, keep=ms, q_dils=(1,) * N_ATTN_GROUPS)
    ckt = cache_k_win.transpose(0, 2, 3, 1).reshape(ns, D_MODEL, n_ctx)
    cvt = cache_v_win.transpose(0, 2, 3, 1).reshape(ns, D_MODEL, n_ctx)
    per_sample = lambda a: a.reshape(ns, ts, D_MODEL)
    os_, kwt, vwt = _cached_attention([per_sample(a) for a in qs], per_sample(ks[1]), per_sample(vs[1]),
                                      per_sample(ks_f), per_sample(vs_f), ckt, cvt, chunk=CACHE_CHUNK)
    h_all, idx_tile, g1, g2 = _merge_proj(outs, lses, h, os_.reshape(ms, D_MODEL), hs, w_o_b, norm_ffn[1],
                                          w_router[0], tm=TM_TOKEN)

    visit_tile, visit_expert, visit_lo, visit_hi, n_visits, pos = _routing_tables(idx_tile, TM_EXPERT)
    xs = _push_rows(h_all, norm_ffn[1], pos, TM_TOKEN)
    ys = _expert_layer(xs, visit_tile, visit_expert, visit_lo, visit_hi, n_visits,
                       w_exp_gu_b, w_exp_down_b, TM_EXPERT, tf_exp)
    y_prompt = _moe_combine(ys, h_all, g1, g2, pos, norm_final, TM_TOKEN, 0, mp // TM_TOKEN)
    y_sample = _moe_combine(ys, h_all, g1, g2, pos, norm_final, TM_TOKEN, mp // TM_TOKEN, ms // TM_TOKEN)

    heads = lambda a: a.reshape(a.shape[0], a.shape[1], N_HEADS, HEAD_DIM)
    heads_t = lambda a: a.reshape(ns, N_HEADS, HEAD_DIM, a.shape[2]).transpose(0, 3, 1, 2)
    return (y_prompt.reshape(nb, seq, D_MODEL), y_sample.reshape(ns, ts, D_MODEL), pool_prompt, pool_sample,
            heads(k_win_p), heads(v_win_p), heads_t(kwt), heads_t(vwt))
```
